```python
import math
import jax, jax.numpy as jnp
from jax import lax
import numpy as np

D_MODEL = 2048
BATCH = 8
SEQ = 2048
DEPTH = 1
DEC_BATCH = 4
DEC_SEQ = 2048
PAST_LEN = 128

N_META = 16
Q_BLOCK = 128
PAD = Q_BLOCK - N_META
A_HEADS = 8
A_HEAD_DIM = 64
A_WIDTH = A_HEADS * 2 * A_HEAD_DIM
H_HEADS = 8
H_EXPAND = 128
H_DV = 128
H_FDIM = H_HEADS * H_EXPAND
H_WIDTH = H_HEADS * H_DV
H_CHUNK = 64
N_EXPERTS = 256
TOP_K = 8
N_GROUPS = 8
TOPK_GROUPS = 4
EXPERT_FF = 512
SHARED_FF = 512
ROUTED_SCALE = 2.5
ROW_BLOCK = 128
DN_ALPHA = (2 * DEPTH) ** 0.25
DN_BETA = (8 * DEPTH) ** -0.25
EPS = 1e-5
IN_SIZES = (A_WIDTH, A_WIDTH, A_WIDTH, H_FDIM, H_FDIM, H_FDIM, H_WIDTH, H_WIDTH, D_MODEL, D_MODEL)
IN_COLS = sum(IN_SIZES)

kernel_name = 'hybrid_diffattn_hgrn2_moe_encoder'


def layer_norm(x, g, b):
    xf = x.astype(jnp.float32)
    mu = jnp.mean(xf, axis=-1, keepdims=True)
    var = jnp.mean(jnp.square(xf - mu), axis=-1, keepdims=True)
    y = (xf - mu) * lax.rsqrt(var + EPS) * g.astype(jnp.float32) + b.astype(jnp.float32)
    return y.astype(x.dtype)


def rms_norm(x, g):
    xf = x.astype(jnp.float32)
    y = xf * lax.rsqrt(jnp.mean(jnp.square(xf), axis=-1, keepdims=True) + EPS) * g.astype(jnp.float32)
    return y.astype(x.dtype)


def swiglu(x, w_g, w_u, w_d):
    return (jax.nn.silu(x @ w_g) * (x @ w_u)) @ w_d


def diff_attention(q1, q2, k1, k2, v, lam):
    bsz, heads, seq, hd = q1.shape
    n_blk = (seq + PAD) // Q_BLOCK

    def to_blocks(t):
        t = jnp.pad(t, ((0, 0), (0, 0), (PAD, 0), (0, 0)))
        return jnp.moveaxis(t.reshape(bsz, heads, n_blk, Q_BLOCK, hd), 2, 0)

    slopes = 2.0 ** (-8.0 * jnp.arange(1, heads + 1, dtype=jnp.float32) / heads)
    k_pos = jnp.arange(seq, dtype=jnp.int32)
    scale = hd ** -0.5

    def one_block(args):
        blk, qb1, qb2 = args
        q_pos = blk * Q_BLOCK + jnp.arange(Q_BLOCK, dtype=jnp.int32) - PAD
        dist = jnp.abs(q_pos[:, None] - k_pos[None, :]).astype(jnp.float32)
        bias = -slopes[:, None, None] * dist
        p1 = jax.nn.softmax(jnp.einsum('bhqd,bhkd->bhqk', qb1, k1).astype(jnp.float32) * scale + bias, axis=-1)
        p2 = jax.nn.softmax(jnp.einsum('bhqd,bhkd->bhqk', qb2, k2).astype(jnp.float32) * scale + bias, axis=-1)
        w = (p1 - lam * p2).astype(v.dtype)
        return jnp.einsum('bhqk,bhke->bhqe', w, v)

    out = lax.map(one_block, (jnp.arange(n_blk, dtype=jnp.int32), to_blocks(q1), to_blocks(q2)))
    out = jnp.moveaxis(out, 0, 2).reshape(bsz, heads, n_blk * Q_BLOCK, v.shape[-1])
    return out[:, :, PAD:]


def hgrn_chunk(q, k, logf, v):
    bsz, heads, lp, dk = q.shape
    dv = v.shape[-1]
    n = lp // H_CHUNK
    q = q.reshape(bsz, heads, n, H_CHUNK, dk)
    k = k.reshape(bsz, heads, n, H_CHUNK, dk)
    logf = logf.reshape(bsz, heads, n, H_CHUNK, dk)
    v = v.reshape(bsz, heads, n, H_CHUNK, dv)
    b = jnp.cumsum(logf, axis=3)
    b_ref = b[:, :, :, H_CHUNK // 2:H_CHUNK // 2 + 1]
    b_last = b[:, :, :, -1:]
    scores = jnp.einsum('bhncd,bhnsd->bhncs', q * jnp.exp(b - b_ref), k * jnp.exp(b_ref - b))
    lower = jnp.tril(jnp.ones((H_CHUNK, H_CHUNK), dtype=bool))
    scores = jnp.where(lower, scores, 0.0)
    o_intra = jnp.einsum('bhncs,bhnse->bhnce', scores, v)
    d_state = jnp.einsum('bhncd,bhnce->bhnde', k * jnp.exp(b_last - b), v)
    decay = jnp.exp(b_last[:, :, :, 0])

    def step(state, inp):
        dec, ds = inp
        return dec[..., None] * state + ds, state

    s0 = jnp.zeros((bsz, heads, dk, dv), q.dtype)
    _, s_prev = lax.scan(step, s0, (jnp.moveaxis(decay, 2, 0), jnp.moveaxis(d_state, 2, 0)))
    s_prev = jnp.moveaxis(s_prev, 0, 2)
    o_inter = jnp.einsum('bhncd,bhnde->bhnce', q * jnp.exp(b), s_prev)
    return (o_intra + o_inter).reshape(bsz, heads, lp, dv)


def lower_bounds(gamma):
    return jnp.cumsum(jax.nn.softmax(gamma.astype(jnp.float32), axis=0), axis=0)


def hgrn2_bidir(hq, hf_fwd, hf_bwd, hi, lb_fwd, lb_bwd):
    bsz, seq, _ = hq.shape

    def heads(t, d):
        return t.astype(jnp.float32).reshape(bsz, seq, H_HEADS, d)

    def prep(t):
        return jnp.transpose(jnp.pad(t, ((0, 0), (PAD, 0), (0, 0), (0, 0))), (0, 2, 1, 3))

    def gates(logit, lb):
        lb = lb.reshape(H_HEADS, H_EXPAND)
        f = lb + (1.0 - lb) * jax.nn.sigmoid(heads(logit, H_EXPAND))
        return prep(1.0 - f), prep(jnp.log(f))

    q = prep(jax.nn.silu(heads(hq, H_EXPAND)))
    v = prep(heads(hi, H_DV))
    k_f, lf_f = gates(hf_fwd, lb_fwd)
    k_b, lf_b = gates(hf_bwd, lb_bwd)
    flip = lambda t: jnp.flip(t, axis=2)
    o = hgrn_chunk(q, k_f, lf_f, v) + flip(hgrn_chunk(flip(q), flip(k_b), flip(lf_b), flip(v)))
    return o[:, :, PAD:]


def routed_experts(h, idx, gate, w_ex_gate, w_ex_up, w_ex_down, layer):
    n_tok = h.shape[0]
    n_assign = n_tok * TOP_K
    flat_e = idx.reshape(-1)
    flat_tok = jnp.repeat(jnp.arange(n_tok, dtype=jnp.int32), TOP_K)
    flat_w = gate.reshape(-1).astype(h.dtype)
    order = jnp.argsort(flat_e)
    sorted_e = flat_e[order]
    counts = jnp.bincount(flat_e, length=N_EXPERTS)
    start = jnp.cumsum(counts) - counts
    pcounts = (counts + ROW_BLOCK - 1) // ROW_BLOCK * ROW_BLOCK
    pend = jnp.cumsum(pcounts)
    pstart = pend - pcounts
    dest = pstart[sorted_e] + jnp.arange(n_assign, dtype=jnp.int32) - start[sorted_e]
    n_blk = (n_assign + N_EXPERTS * (ROW_BLOCK - 1) + ROW_BLOCK - 1) // ROW_BLOCK
    n_rows = n_blk * ROW_BLOCK
    row_tok = jnp.zeros((n_rows,), jnp.int32).at[dest].set(flat_tok[order])
    row_w = jnp.zeros((n_rows,), h.dtype).at[dest].set(flat_w[order])
    blk_e = jnp.minimum(jnp.searchsorted(pend, jnp.arange(n_blk, dtype=jnp.int32) * ROW_BLOCK, side='right'), N_EXPERTS - 1)

    def step(acc, inp):
        tok, wt, e = inp
        xb = h[tok]
        hid = jax.nn.silu(xb @ w_ex_gate[layer, e]) * (xb @ w_ex_up[layer, e])
        return acc.at[tok].add((hid @ w_ex_down[layer, e]) * wt[:, None]), None

    acc, _ = lax.scan(step, jnp.zeros_like(h), (row_tok.reshape(n_blk, ROW_BLOCK), row_w.reshape(n_blk, ROW_BLOCK), blk_e))
    return acc


def moe_ffn(h, layer, w_router, router_bias, w_sh_gate, w_sh_up, w_sh_down, w_ex_gate, w_ex_up, w_ex_down):
    n_tok = h.shape[0]
    scores = jax.nn.sigmoid((h @ w_router[layer]).astype(jnp.float32))
    biased = scores + router_bias[layer].astype(jnp.float32)
    grp_score = lax.top_k(biased.reshape(n_tok, N_GROUPS, N_EXPERTS // N_GROUPS), 2)[0].sum(-1)
    _, grp_idx = lax.top_k(grp_score, TOPK_GROUPS)
    grp_keep = jnp.any(grp_idx[:, :, None] == jnp.arange(N_GROUPS)[None, None, :], axis=1)
    keep = jnp.repeat(grp_keep, N_EXPERTS // N_GROUPS, axis=1)
    _, idx = lax.top_k(jnp.where(keep, biased, -jnp.inf), TOP_K)
    gate = jnp.take_along_axis(scores, idx, axis=1)
    gate = gate / jnp.sum(gate, axis=-1, keepdims=True) * ROUTED_SCALE
    shared = swiglu(h, w_sh_gate[layer], w_sh_up[layer], w_sh_down[layer])
    return shared + routed_experts(h, idx, gate, w_ex_gate, w_ex_up, w_ex_down, layer)


def encoder_layer(x, layer, lb_fwd, lb_bwd, w_in, lambda_q1, lambda_k1, lambda_q2, lambda_k2, attn_subln_g,
                  hg_norm_g, w_branch_attn, w_branch_hgrn, w_out, ln1_g, ln1_b, w_router, router_bias,
                  w_sh_gate, w_sh_up, w_sh_down, w_ex_gate, w_ex_up, w_ex_down, ln2_g, ln2_b):
    bsz, seq, d = x.shape
    splits = np.cumsum(IN_SIZES)[:-1].tolist()
    aq, ak, av, hq, hf_fwd, hf_bwd, hi, hg, g_attn, g_hgrn = jnp.split(x @ w_in[layer], splits, axis=-1)
    aq = aq.reshape(bsz, seq, A_HEADS, 2, A_HEAD_DIM)
    ak = ak.reshape(bsz, seq, A_HEADS, 2, A_HEAD_DIM)
    tr = lambda t: jnp.transpose(t, (0, 2, 1, 3))
    q1, q2 = tr(aq[:, :, :, 0]), tr(aq[:, :, :, 1])
    k1, k2 = tr(ak[:, :, :, 0]), tr(ak[:, :, :, 1])
    vv = tr(av.reshape(bsz, seq, A_HEADS, 2 * A_HEAD_DIM))
    lambda_init = 0.8 - 0.6 * math.exp(-0.3 * layer)
    lam = (jnp.exp(jnp.sum(lambda_q1[layer].astype(jnp.float32) * lambda_k1[layer].astype(jnp.float32)))
           - jnp.exp(jnp.sum(lambda_q2[layer].astype(jnp.float32) * lambda_k2[layer].astype(jnp.float32)))
           + lambda_init)
    ao = diff_attention(q1, q2, k1, k2, vv, lam)
    ao = rms_norm(ao, attn_subln_g[layer]) * (1.0 - lambda_init)
    ao = tr(ao).reshape(bsz, seq, A_WIDTH)
    ho = hgrn2_bidir(hq, hf_fwd, hf_bwd, hi, lb_fwd, lb_bwd)
    ho = tr(rms_norm(ho, hg_norm_g[layer])).reshape(bsz, seq, H_WIDTH).astype(x.dtype) * jax.nn.sigmoid(hg)
    merged = jax.nn.sigmoid(g_attn) * (ao @ w_branch_attn[layer]) + jax.nn.sigmoid(g_hgrn) * (ho @ w_branch_hgrn[layer])
    h = layer_norm(DN_ALPHA * x + merged @ w_out[layer], ln1_g[layer], ln1_b[layer])
    ff = moe_ffn(h.reshape(bsz * seq, d), layer, w_router, router_bias, w_sh_gate, w_sh_up, w_sh_down,
                 w_ex_gate, w_ex_up, w_ex_down).reshape(bsz, seq, d)
    return layer_norm(DN_ALPHA * h + ff, ln2_g[layer], ln2_b[layer])


def encode(x, meta_tokens, ln_in_g, ln_in_b, hg_gamma_fwd, hg_gamma_bwd, layer_params):
    bsz = x.shape[0]
    meta = jnp.broadcast_to(meta_tokens.astype(x.dtype)[None], (bsz, N_META, x.shape[-1]))
    h = layer_norm(jnp.concatenate([meta, x], axis=1), ln_in_g, ln_in_b)
    lb_fwd = lower_bounds(hg_gamma_fwd)
    lb_bwd = lower_bounds(hg_gamma_bwd)
    for layer in range(DEPTH):
        h = encoder_layer(h, layer, lb_fwd[layer], lb_bwd[layer], *layer_params)
    return h[:, N_META:]


def setup_inputs(seed: int = 0) -> dict:
    key = jax.random.key(seed)
    keys = jax.random.split(key, 29)

    def nrm(i, shape, scale):
        return jax.random.normal(keys[i], shape, jnp.float32) * scale

    D = D_MODEL
    return {
        'x_prompt': nrm(0, (BATCH, SEQ, D), 1.0),
        'x_sample': nrm(1, (DEC_BATCH, DEC_SEQ, D), 1.0),
        'meta_tokens': nrm(2, (N_META, D), 1.0),
        'ln_in_g': 1.0 + nrm(3, (D,), 0.02),
        'ln_in_b': nrm(4, (D,), 0.02),
        'hg_gamma_fwd': nrm(5, (DEPTH + 1, H_FDIM), 0.1),
        'hg_gamma_bwd': nrm(6, (DEPTH + 1, H_FDIM), 0.1),
        'w_in': nrm(7, (DEPTH, D, IN_COLS), D ** -0.5),
        'lambda_q1': nrm(8, (DEPTH, A_HEAD_DIM), 0.1),
        'lambda_k1': nrm(9, (DEPTH, A_HEAD_DIM), 0.1),
        'lambda_q2': nrm(10, (DEPTH, A_HEAD_DIM), 0.1),
        'lambda_k2': nrm(11, (DEPTH, A_HEAD_DIM), 0.1),
        'attn_subln_g': 1.0 + nrm(12, (DEPTH, 2 * A_HEAD_DIM), 0.02),
        'hg_norm_g': 1.0 + nrm(13, (DEPTH, H_DV), 0.02),
        'w_branch_attn': nrm(14, (DEPTH, A_WIDTH, D), DN_BETA * A_WIDTH ** -0.5),
        'w_branch_hgrn': nrm(15, (DEPTH, H_WIDTH, D), DN_BETA * H_WIDTH ** -0.5),
        'w_out': nrm(16, (DEPTH, D, D), DN_BETA * D ** -0.5),
        'ln1_g': 1.0 + nrm(17, (DEPTH, D), 0.02),
        'ln1_b': nrm(18, (DEPTH, D), 0.02),
        'w_router': nrm(19, (DEPTH, D, N_EXPERTS), D ** -0.5),
        'router_bias': nrm(20, (DEPTH, N_EXPERTS), 0.01),
        'w_sh_gate': nrm(21, (DEPTH, D, SHARED_FF), D ** -0.5),
        'w_sh_up': nrm(22, (DEPTH, D, SHARED_FF), D ** -0.5),
        'w_sh_down': nrm(23, (DEPTH, SHARED_FF, D), DN_BETA * SHARED_FF ** -0.5),
        'w_ex_gate': nrm(24, (DEPTH, N_EXPERTS, D, EXPERT_FF), D ** -0.5),
        'w_ex_up': nrm(25, (DEPTH, N_EXPERTS, D, EXPERT_FF), D ** -0.5),
        'w_ex_down': nrm(26, (DEPTH, N_EXPERTS, EXPERT_FF, D), DN_BETA * EXPERT_FF ** -0.5),
        'ln2_g': 1.0 + nrm(27, (DEPTH, D), 0.02),
        'ln2_b': nrm(28, (DEPTH, D), 0.02),
    }


def reference(x_prompt, x_sample, meta_tokens, ln_in_g, ln_in_b, hg_gamma_fwd, hg_gamma_bwd, w_in,
              lambda_q1, lambda_k1, lambda_q2, lambda_k2, attn_subln_g, hg_norm_g, w_branch_attn,
              w_branch_hgrn, w_out, ln1_g, ln1_b, w_router, router_bias, w_sh_gate, w_sh_up, w_sh_down,
              w_ex_gate, w_ex_up, w_ex_down, ln2_g, ln2_b):
    layer_params = (w_in, lambda_q1, lambda_k1, lambda_q2, lambda_k2, attn_subln_g, hg_norm_g,
                    w_branch_attn, w_branch_hgrn, w_out, ln1_g, ln1_b, w_router, router_bias,
                    w_sh_gate, w_sh_up, w_sh_down, w_ex_gate, w_ex_up, w_ex_down, ln2_g, ln2_b)
    y_prompt = encode(x_prompt, meta_tokens, ln_in_g, ln_in_b, hg_gamma_fwd, hg_gamma_bwd, layer_params)
    y_sample = encode(x_sample, meta_tokens, ln_in_g, ln_in_b, hg_gamma_fwd, hg_gamma_bwd, layer_params)
    return (y_prompt, y_sample)
```

```python
import functools
import math

import jax
import jax.numpy as jnp
from jax import lax
from jax.experimental import pallas as pl
from jax.experimental.pallas import tpu as pltpu

F32 = jnp.float32
BF16 = jnp.bfloat16

N_META = 16
A_HEADS = 8
A_HEAD_DIM = 64
H_HEADS = 8
H_EXPAND = 128
H_DV = 128
H_CHUNK = 64
N_EXPERTS = 256
TOP_K = 8
N_GROUPS = 8
TOPK_GROUPS = 4
GROUP_SIZE = N_EXPERTS // N_GROUPS
ROUTED_SCALE = 2.5
DEPTH = 1
DN_ALPHA = (2 * DEPTH) ** 0.25
EPS = 1e-5
LAMBDA_INIT = 0.8 - 0.6 * math.exp(-0.3 * 0)

LANES = 128
VMEM_BYTES = 64 * 1024 * 1024

META_PAD = 128
NEG_BIG = -1e30
EXPERT_ROWS = 256


def _cparams(semantics, vmem_mb):
    return pltpu.CompilerParams(dimension_semantics=semantics, vmem_limit_bytes=vmem_mb * 1024 * 1024)


def _layer_norm_f32(x, g, b):
    mu = jnp.mean(x, axis=-1, keepdims=True)
    xc = x - mu
    var = jnp.mean(xc * xc, axis=-1, keepdims=True)
    return xc * lax.rsqrt(var + EPS) * g + b


def _resident(shape):
    nd = len(shape)
    return pl.BlockSpec(shape, lambda *_: (0,) * nd, pipeline_mode=pl.Buffered(1))


def _ln_matmul_kernel(x_ref, g_ref, b_ref, w_ref, o_ref, xn_ref):
    @pl.when(pl.program_id(1) == 0)
    def _():
        xn_ref[...] = _layer_norm_f32(x_ref[...], g_ref[...], b_ref[...]).astype(BF16)

    o_ref[...] = jnp.dot(xn_ref[...], w_ref[...], preferred_element_type=F32).astype(o_ref.dtype)


def _ln_matmul(x, g, b, w, tm, tn):
    m, d = x.shape
    n = w.shape[1]
    return pl.pallas_call(
        _ln_matmul_kernel,
        grid=(m // tm, n // tn),
        in_specs=[
            pl.BlockSpec((tm, d), lambda i, j: (i, 0)),
            pl.BlockSpec((1, d), lambda i, j: (0, 0)),
            pl.BlockSpec((1, d), lambda i, j: (0, 0)),
            pl.BlockSpec((d, tn), lambda i, j: (0, j)),
        ],
        out_specs=pl.BlockSpec((tm, tn), lambda i, j: (i, j)),
        out_shape=jax.ShapeDtypeStruct((m, n), BF16),
        scratch_shapes=[pltpu.VMEM((tm, d), BF16)],
        compiler_params=_cparams(("parallel", "arbitrary"), 48),
        name="ln_matmul",
    )(x, g, b, w)


def _diff_attn_kernel(par_ref, q_ref, k_ref, v_ref, km_ref, vm_ref, g_ref, o_ref, kall_ref, vall_ref):
    h = pl.program_id(1)
    qi = pl.program_id(2)
    tq = q_ref.shape[0]
    n_keys = kall_ref.shape[0]

    @pl.when(qi == 0)
    def _():
        kall_ref[0:META_PAD, :] = km_ref[...]
        kall_ref[META_PAD:, :] = k_ref[...]
        vall_ref[0:META_PAD, :] = vm_ref[...]
        vall_ref[META_PAD:, :] = v_ref[...]

    lam = par_ref[0]
    slope = par_ref[1 + h]
    q = q_ref[...]
    lane = lax.broadcasted_iota(jnp.int32, (1, 2 * A_HEAD_DIM), 1)
    scale = jnp.asarray(A_HEAD_DIM ** -0.5, BF16)
    zero = jnp.zeros((), BF16)
    q1 = jnp.where(lane < A_HEAD_DIM, q, zero) * scale
    q2 = jnp.where(lane >= A_HEAD_DIM, q, zero) * scale
    k_all = kall_ref[...]
    nt = (((1,), (1,)), ((), ()))
    s1 = lax.dot_general(q1, k_all, nt, preferred_element_type=F32)
    s2 = lax.dot_general(q2, k_all, nt, preferred_element_type=F32)

    col = lax.broadcasted_iota(jnp.int32, (1, n_keys), 1)
    row = lax.broadcasted_iota(jnp.int32, (tq, 1), 0) + (qi * tq + META_PAD)
    dist = jnp.abs(row - col).astype(F32)
    bias = jnp.where(col < META_PAD - N_META, NEG_BIG, -slope * dist)

    def softmax_parts(s):
        s = s + bias
        m = jnp.max(s, axis=-1, keepdims=True)
        p = jnp.exp(s - m)
        return p, jnp.sum(p, axis=-1, keepdims=True)

    p1, l1 = softmax_parts(s1)
    p2, l2 = softmax_parts(s2)
    w = p1 * (1.0 / l1) - p2 * (lam / l2)
    o = jnp.dot(w.astype(BF16), vall_ref[...], preferred_element_type=F32)
    ms = jnp.mean(o * o, axis=-1, keepdims=True)
    o_ref[...] = (o * lax.rsqrt(ms + EPS) * g_ref[...]).astype(o_ref.dtype)


def _diff_attn(proj, meta_proj, params, subln_g, n_seq, seq, tq):
    n_tok = proj.shape[0]
    nq = seq // tq
    hw = 2 * A_HEAD_DIM
    return pl.pallas_call(
        _diff_attn_kernel,
        grid=(n_seq, A_HEADS, nq),
        in_specs=[
            pl.BlockSpec(memory_space=pltpu.SMEM),
            pl.BlockSpec((tq, hw), lambda b, h, i: (b * nq + i, h)),
            pl.BlockSpec((seq, hw), lambda b, h, i: (b, A_HEADS + h)),
            pl.BlockSpec((seq, hw), lambda b, h, i: (b, 2 * A_HEADS + h)),
            pl.BlockSpec((META_PAD, hw), lambda b, h, i: (0, A_HEADS + h)),
            pl.BlockSpec((META_PAD, hw), lambda b, h, i: (0, 2 * A_HEADS + h)),
            pl.BlockSpec((1, hw), lambda b, h, i: (0, 0)),
        ],
        out_specs=pl.BlockSpec((tq, hw), lambda b, h, i: (b * nq + i, h)),
        out_shape=jax.ShapeDtypeStruct((n_tok, A_HEADS * hw), BF16),
        scratch_shapes=[pltpu.VMEM((META_PAD + seq, hw), BF16), pltpu.VMEM((META_PAD + seq, hw), BF16)],
        compiler_params=_cparams(("parallel", "parallel", "arbitrary"), 48),
        name="diff_attn",
    )(params, proj, proj, proj, meta_proj, meta_proj, subln_g)


def _split_cumsum(tri, x):
    hi = x.astype(BF16)
    lo = (x - hi.astype(F32)).astype(BF16)
    return (jnp.dot(tri, hi, preferred_element_type=F32) + jnp.dot(tri, lo, preferred_element_type=F32))


def _hgrn_chunk(q, k, logf, v_bf, state_t, reverse):
    c = q.shape[0]
    r = lax.broadcasted_iota(jnp.int32, (c, c), 0)
    s = lax.broadcasted_iota(jnp.int32, (c, c), 1)
    if reverse:
        visible = s >= r
        ref_row, last_row = c - 1 - c // 2, 0
    else:
        visible = s <= r
        ref_row, last_row = c // 2, c - 1
    tri = jnp.where(visible, 1.0, 0.0).astype(BF16)
    b = _split_cumsum(tri, logf)
    b_ref = b[ref_row:ref_row + 1, :]
    b_last = b[last_row:last_row + 1, :]
    nt = (((1,), (1,)), ((), ()))
    tn = (((0,), (0,)), ((), ()))
    qd = (q * jnp.exp(b - b_ref)).astype(BF16)
    kd = (k * jnp.exp(b_ref - b)).astype(BF16)
    scores = lax.dot_general(qd, kd, nt, preferred_element_type=F32)
    scores = jnp.where(visible, scores, 0.0).astype(BF16)
    o_intra = jnp.dot(scores, v_bf, preferred_element_type=F32)
    qe = (q * jnp.exp(b)).astype(BF16)
    o_inter = lax.dot_general(qe, state_t.astype(BF16), nt, preferred_element_type=F32)
    ks = (k * jnp.exp(b_last - b)).astype(BF16)
    d_state_t = lax.dot_general(v_bf, ks, tn, preferred_element_type=F32)
    new_state_t = state_t * jnp.exp(b_last) + d_state_t
    return o_intra + o_inter, new_state_t


def _forget_gate(logit, lb):
    f = lb + (1.0 - lb) * jax.nn.sigmoid(logit)
    return 1.0 - f, jnp.log(f)


def _hgrn_kernel(q_ref, ff_ref, fb_ref, v_ref, hg_ref, mff_ref, mv_ref, lbf_ref, lbb_ref, g_ref,
                 o_ref, of_ref, ob_ref):
    seq = q_ref.shape[0]
    n_chunks = seq // H_CHUNK
    lbf = lbf_ref[...]
    lbb = lbb_ref[...]

    mrow = lax.broadcasted_iota(jnp.int32, (H_CHUNK, 1), 0)
    is_meta = mrow >= H_CHUNK - N_META
    mk, mlogf = _forget_gate(mff_ref[...].astype(F32), lbf)
    mk = jnp.where(is_meta, mk, 0.0)
    mlogf = jnp.where(is_meta, mlogf, 0.0)
    zeros_q = jnp.zeros((H_CHUNK, H_EXPAND), F32)
    state0 = jnp.zeros((H_DV, H_EXPAND), F32)
    _, state_f0 = _hgrn_chunk(zeros_q, mk, mlogf, mv_ref[...], state0, reverse=False)

    def body(i, carry):
        state_f, state_b = carry
        rf = pl.multiple_of(i * H_CHUNK, H_CHUNK)
        rb = pl.multiple_of((n_chunks - 1 - i) * H_CHUNK, H_CHUNK)

        q_f = jax.nn.silu(q_ref[pl.ds(rf, H_CHUNK), :].astype(F32))
        k_f, logf_f = _forget_gate(ff_ref[pl.ds(rf, H_CHUNK), :].astype(F32), lbf)
        o_f, state_f = _hgrn_chunk(q_f, k_f, logf_f, v_ref[pl.ds(rf, H_CHUNK), :], state_f, reverse=False)
        of_ref[pl.ds(rf, H_CHUNK), :] = o_f

        q_b = jax.nn.silu(q_ref[pl.ds(rb, H_CHUNK), :].astype(F32))
        k_b, logf_b = _forget_gate(fb_ref[pl.ds(rb, H_CHUNK), :].astype(F32), lbb)
        o_b, state_b = _hgrn_chunk(q_b, k_b, logf_b, v_ref[pl.ds(rb, H_CHUNK), :], state_b, reverse=True)
        ob_ref[pl.ds(rb, H_CHUNK), :] = o_b
        return state_f, state_b

    lax.fori_loop(0, n_chunks, body, (state_f0, state0))

    o = of_ref[...] + ob_ref[...]
    ms = jnp.mean(o * o, axis=-1, keepdims=True)
    normed = (o * lax.rsqrt(ms + EPS) * g_ref[...])
    o_ref[...] = (normed * jax.nn.sigmoid(hg_ref[...].astype(F32))).astype(o_ref.dtype)


def _hgrn(proj, meta_proj, lb_fwd, lb_bwd, norm_g, n_seq, seq):
    n_tok = proj.shape[0]
    w = H_DV
    col0 = 3 * A_HEADS
    meta_blk = META_PAD // H_CHUNK - 1

    def sec(k):
        return pl.BlockSpec((seq, w), lambda b, h, k=k: (b, col0 + k * H_HEADS + h))

    def meta_sec(k):
        return pl.BlockSpec((H_CHUNK, w), lambda b, h, k=k: (meta_blk, col0 + k * H_HEADS + h))

    return pl.pallas_call(
        _hgrn_kernel,
        grid=(n_seq, H_HEADS),
        in_specs=[
            sec(0), sec(1), sec(2), sec(3), sec(4),
            meta_sec(1), meta_sec(3),
            pl.BlockSpec((1, w), lambda b, h: (0, h)),
            pl.BlockSpec((1, w), lambda b, h: (0, h)),
            pl.BlockSpec((1, w), lambda b, h: (0, 0)),
        ],
        out_specs=pl.BlockSpec((seq, w), lambda b, h: (b, h)),
        out_shape=jax.ShapeDtypeStruct((n_tok, H_HEADS * w), BF16),
        scratch_shapes=[pltpu.VMEM((seq, w), F32), pltpu.VMEM((seq, w), F32)],
        compiler_params=_cparams(("parallel", "parallel"), 32),
        name="hgrn",
    )(proj, proj, proj, proj, proj, meta_proj, meta_proj, lb_fwd, lb_bwd, norm_g)


def _merge_kernel(ao_ref, ho_ref, ga_ref, gh_ref, x_ref, wa_ref, wh_ref, wo_ref, lng_ref, lnb_ref,
                  l1g_ref, l1b_ref, hf_ref, hb_ref):
    pa = jnp.dot(ao_ref[...], wa_ref[...], preferred_element_type=F32)
    ph = jnp.dot(ho_ref[...], wh_ref[...], preferred_element_type=F32)
    merged = (jax.nn.sigmoid(ga_ref[...].astype(F32)) * pa + jax.nn.sigmoid(gh_ref[...].astype(F32)) * ph)
    t = jnp.dot(merged.astype(BF16), wo_ref[...], preferred_element_type=F32)
    x_in = _layer_norm_f32(x_ref[...], lng_ref[...], lnb_ref[...])
    h = _layer_norm_f32(DN_ALPHA * x_in + t, l1g_ref[...], l1b_ref[...])
    hf_ref[...] = h
    hb_ref[...] = h.astype(BF16)


def _merge(ao, ho, proj, x, wa, wh, wo, ln_g, ln_b, l1g, l1b, tm):
    n_tok, d = x.shape
    gate_blk0 = (proj.shape[1] - 2 * d) // d
    vec = pl.BlockSpec((1, d), lambda i: (0, 0))
    return pl.pallas_call(
        _merge_kernel,
        grid=(n_tok // tm,),
        in_specs=[
            pl.BlockSpec((tm, ao.shape[1]), lambda i: (i, 0)),
            pl.BlockSpec((tm, ho.shape[1]), lambda i: (i, 0)),
            pl.BlockSpec((tm, d), lambda i: (i, gate_blk0)),
            pl.BlockSpec((tm, d), lambda i: (i, gate_blk0 + 1)),
            pl.BlockSpec((tm, d), lambda i: (i, 0)),
            _resident(wa.shape), _resident(wh.shape), _resident(wo.shape),
            vec, vec, vec, vec,
        ],
        out_specs=[pl.BlockSpec((tm, d), lambda i: (i, 0)), pl.BlockSpec((tm, d), lambda i: (i, 0))],
        out_shape=[jax.ShapeDtypeStruct((n_tok, d), F32), jax.ShapeDtypeStruct((n_tok, d), BF16)],
        compiler_params=_cparams(("parallel",), 56),
        name="merge",
    )(ao, ho, proj, proj, x, wa, wh, wo, ln_g, ln_b, l1g, l1b)


def _first_argmax(x, iota, size):
    m = jnp.max(x, axis=0, keepdims=True)
    idx = jnp.min(jnp.where(x == m, iota, size), axis=0, keepdims=True)
    return m, idx


def _router_kernel(h_ref, wrt_ref, bias_ref, idx_ref, gate_ref, rank_ref, cnt_ref, base_ref):
    tm = h_ref.shape[0]

    @pl.when(pl.program_id(0) == 0)
    def _():
        base_ref[...] = jnp.zeros_like(base_ref)

    nt = (((1,), (1,)), ((), ()))
    logits = lax.dot_general(wrt_ref[...], h_ref[...], nt, preferred_element_type=F32)
    scores = jax.nn.sigmoid(logits)
    biased = scores + bias_ref[...]
    neg_inf = jnp.asarray(-jnp.inf, F32)

    iota_g = lax.broadcasted_iota(jnp.int32, (GROUP_SIZE, tm), 0)
    iota_n = lax.broadcasted_iota(jnp.int32, (N_GROUPS, tm), 0)
    grp = jnp.full((N_GROUPS, tm), neg_inf, F32)
    for g in range(N_GROUPS):
        xg = biased[g * GROUP_SIZE:(g + 1) * GROUP_SIZE, :]
        m1, i1 = _first_argmax(xg, iota_g, GROUP_SIZE)
        m2 = jnp.max(jnp.where(iota_g == i1, neg_inf, xg), axis=0, keepdims=True)
        grp = jnp.where(iota_n == g, m1 + m2, grp)

    keep_f = jnp.zeros((N_GROUPS, tm), F32)
    for _ in range(TOPK_GROUPS):
        _, ig = _first_argmax(grp, iota_n, N_GROUPS)
        sel = iota_n == ig
        keep_f = jnp.where(sel, 1.0, keep_f)
        grp = jnp.where(sel, neg_inf, grp)

    masked = jnp.concatenate(
        [jnp.where(keep_f[g:g + 1, :] > 0.5, biased[g * GROUP_SIZE:(g + 1) * GROUP_SIZE, :], neg_inf)
         for g in range(N_GROUPS)], axis=0)

    iota_e = lax.broadcasted_iota(jnp.int32, (N_EXPERTS, tm), 0)
    chosen = jnp.zeros((N_EXPERTS, tm), F32)
    idxs, gates = [], []
    for _ in range(TOP_K):
        _, ie = _first_argmax(masked, iota_e, N_EXPERTS)
        sel = iota_e == ie
        gates.append(jnp.sum(jnp.where(sel, scores, 0.0), axis=0, keepdims=True))
        idxs.append(ie)
        chosen = jnp.where(sel, 1.0, chosen)
        masked = jnp.where(sel, neg_inf, masked)

    denom = gates[0]
    for gk in gates[1:]:
        denom = denom + gk

    r = lax.broadcasted_iota(jnp.int32, (tm, tm), 0)
    c = lax.broadcasted_iota(jnp.int32, (tm, tm), 1)
    before = jnp.where(r < c, 1.0, 0.0).astype(BF16)
    pos = jnp.dot(chosen.astype(BF16), before, preferred_element_type=F32) + base_ref[...]
    for k in range(TOP_K):
        sel = iota_e == idxs[k]
        rank_ref[k:k + 1, :] = jnp.sum(jnp.where(sel, pos, 0.0), axis=0, keepdims=True).astype(jnp.int32)
        idx_ref[k:k + 1, :] = idxs[k]
        gate_ref[k:k + 1, :] = gates[k] / denom * ROUTED_SCALE

    base_ref[...] = base_ref[...] + jnp.sum(chosen, axis=1, keepdims=True)
    cnt_ref[...] = base_ref[...]


def _router(h_bf, w_router_t, bias_col, tm):
    n_tok, d = h_bf.shape
    row_blk = pl.BlockSpec((TOP_K, tm), lambda i: (0, i))
    return pl.pallas_call(
        _router_kernel,
        grid=(n_tok // tm,),
        in_specs=[
            pl.BlockSpec((tm, d), lambda i: (i, 0)),
            _resident(w_router_t.shape),
            pl.BlockSpec((N_EXPERTS, 1), lambda i: (0, 0)),
        ],
        out_specs=[row_blk, row_blk, row_blk, pl.BlockSpec((N_EXPERTS, 1), lambda i: (0, 0))],
        out_shape=[
            jax.ShapeDtypeStruct((TOP_K, n_tok), jnp.int32),
            jax.ShapeDtypeStruct((TOP_K, n_tok), F32),
            jax.ShapeDtypeStruct((TOP_K, n_tok), jnp.int32),
            jax.ShapeDtypeStruct((N_EXPERTS, 1), F32),
        ],
        scratch_shapes=[pltpu.VMEM((N_EXPERTS, 1), F32)],
        compiler_params=_cparams(("arbitrary",), 32),
        name="router",
    )(h_bf, w_router_t, bias_col)


def _experts_kernel(be_ref, nreal_ref, x_ref, gate_ref, wg_ref, wu_ref, wd_ref, y_ref, wgu_bf, wd_bf):
    i = pl.program_id(0)
    ff = wg_ref.shape[1]
    prev = be_ref[jnp.maximum(i - 1, 0)]

    @pl.when(jnp.logical_or(i == 0, be_ref[i] != prev))
    def _():
        wgu_bf[:, 0:ff] = wg_ref[...].astype(BF16)
        wgu_bf[:, ff:] = wu_ref[...].astype(BF16)
        wd_bf[...] = wd_ref[...].astype(BF16)

    @pl.when(i < nreal_ref[0])
    def _():
        gu = jnp.dot(x_ref[...], wgu_bf[...], preferred_element_type=F32)
        hid = (jax.nn.silu(gu[:, 0:ff]) * gu[:, ff:]).astype(BF16)
        y = jnp.dot(hid, wd_bf[...], preferred_element_type=F32)
        y_ref[...] = y * gate_ref[...]


def _experts(x_sorted, gate_sorted, blk_e, n_real, w_gate, w_up, w_down):
    n_rows, d = x_sorted.shape
    ff = w_gate.shape[-1]
    n_blk = n_rows // EXPERT_ROWS

    def row_map(i, be, nreal):
        return (jnp.minimum(i, nreal[0] - 1), 0)

    grid_spec = pltpu.PrefetchScalarGridSpec(
        num_scalar_prefetch=2,
        grid=(n_blk,),
        in_specs=[
            pl.BlockSpec((EXPERT_ROWS, d), row_map),
            pl.BlockSpec((EXPERT_ROWS, 1), row_map),
            pl.BlockSpec((None, None, d, ff), lambda i, be, nreal: (0, be[i], 0, 0)),
            pl.BlockSpec((None, None, d, ff), lambda i, be, nreal: (0, be[i], 0, 0)),
            pl.BlockSpec((None, None, ff, d), lambda i, be, nreal: (0, be[i], 0, 0)),
        ],
        out_specs=pl.BlockSpec((EXPERT_ROWS, d), row_map),
        scratch_shapes=[pltpu.VMEM((d, 2 * ff), BF16), pltpu.VMEM((ff, d), BF16)],
    )
    return pl.pallas_call(
        _experts_kernel,
        grid_spec=grid_spec,
        out_shape=jax.ShapeDtypeStruct((n_rows, d), F32),
        compiler_params=_cparams(("arbitrary",), 56),
        name="experts",
    )(blk_e, n_real, x_sorted, gate_sorted, w_gate, w_up, w_down)


def _final_kernel(hf_ref, hb_ref, routed_ref, wgu_ref, wd_ref, g_ref, b_ref, o_ref):
    ff = wd_ref.shape[0]
    gu = jnp.dot(hb_ref[...], wgu_ref[...], preferred_element_type=F32)
    hid = (jax.nn.silu(gu[:, 0:ff]) * gu[:, ff:]).astype(BF16)
    shared = jnp.dot(hid, wd_ref[...], preferred_element_type=F32)
    o_ref[...] = _layer_norm_f32(DN_ALPHA * hf_ref[...] + shared + routed_ref[...], g_ref[...], b_ref[...])


def _final(h_f32, h_bf, routed, w_gu, w_d, g, b, tm):
    n_tok, d = h_f32.shape
    tile = pl.BlockSpec((tm, d), lambda i: (i, 0))
    vec = pl.BlockSpec((1, d), lambda i: (0, 0))
    return pl.pallas_call(
        _final_kernel,
        grid=(n_tok // tm,),
        in_specs=[tile, tile, tile, _resident(w_gu.shape), _resident(w_d.shape), vec, vec],
        out_specs=tile,
        out_shape=jax.ShapeDtypeStruct((n_tok, d), F32),
        compiler_params=_cparams(("parallel",), 48),
        name="final",
    )(h_f32, h_bf, routed, w_gu, w_d, g, b)


def _lower_bounds(gamma):
    return jnp.cumsum(jax.nn.softmax(gamma.astype(F32), axis=0), axis=0)


def _dispatch_tables(idx, gate, rank, counts, n_tok):
    n_assign = n_tok * TOP_K
    n_blk = (n_assign + N_EXPERTS * (EXPERT_ROWS - 1) + EXPERT_ROWS - 1) // EXPERT_ROWS
    n_rows = n_blk * EXPERT_ROWS
    counts = counts.astype(jnp.int32)
    pcounts = (counts + EXPERT_ROWS - 1) // EXPERT_ROWS * EXPERT_ROWS
    pend = jnp.cumsum(pcounts)
    pstart = pend - pcounts
    dest = pstart[idx] + rank
    tok = jnp.broadcast_to(jnp.arange(n_tok, dtype=jnp.int32)[None, :], dest.shape)
    row_tok = jnp.zeros((n_rows,), jnp.int32).at[dest.reshape(-1)].set(tok.reshape(-1))
    row_gate = jnp.zeros((n_rows,), F32).at[dest.reshape(-1)].set(gate.reshape(-1))
    n_real = (pend[-1] // EXPERT_ROWS).astype(jnp.int32)
    blk = jnp.arange(n_blk, dtype=jnp.int32)
    blk_e = jnp.searchsorted(pend, jnp.minimum(blk, n_real - 1) * EXPERT_ROWS, side='right')
    blk_e = jnp.minimum(blk_e, N_EXPERTS - 1).astype(jnp.int32)
    return dest, row_tok, row_gate, blk_e, n_real.reshape(1)


def kernel(x_prompt, x_sample, meta_tokens, ln_in_g, ln_in_b, hg_gamma_fwd, hg_gamma_bwd, w_in, lambda_q1, lambda_k1, lambda_q2, lambda_k2, attn_subln_g, hg_norm_g, w_branch_attn, w_branch_hgrn, w_out, ln1_g, ln1_b, w_router, router_bias, w_sh_gate, w_sh_up, w_sh_down, w_ex_gate, w_ex_up, w_ex_down, ln2_g, ln2_b):
    bp, seq, d = x_prompt.shape
    bs, seq_s, _ = x_sample.shape
    assert seq == seq_s and w_in.shape[0] == DEPTH
    n_seq = bp + bs
    n_tok = n_seq * seq
    layer = 0

    x = jnp.concatenate([x_prompt.reshape(bp * seq, d), x_sample.reshape(bs * seq, d)], axis=0)
    row = lambda v: v.reshape(1, -1).astype(F32)
    ln_g, ln_b = row(ln_in_g), row(ln_in_b)
    w_in_bf = w_in[layer].astype(BF16)

    proj = _ln_matmul(x, ln_g, ln_b, w_in_bf, tm=min(1024, n_tok), tn=1024)
    meta_proj = _ln_matmul(meta_tokens.astype(F32), ln_g, ln_b, w_in_bf, tm=N_META, tn=1024)
    meta_proj = jnp.pad(meta_proj, ((META_PAD - N_META, 0), (0, 0)))

    lam = (jnp.exp(jnp.sum(lambda_q1[layer].astype(F32) * lambda_k1[layer].astype(F32)))
           - jnp.exp(jnp.sum(lambda_q2[layer].astype(F32) * lambda_k2[layer].astype(F32))) + LAMBDA_INIT)
    slopes = 2.0 ** (-8.0 * jnp.arange(1, A_HEADS + 1, dtype=F32) / A_HEADS)
    attn_params = jnp.concatenate([lam.reshape(1), slopes]).astype(F32)
    subln_g = row(attn_subln_g[layer]) * (1.0 - LAMBDA_INIT)
    ao = _diff_attn(proj, meta_proj, attn_params, subln_g, n_seq, seq, tq=min(256, seq))

    lb_fwd = row(_lower_bounds(hg_gamma_fwd)[layer])
    lb_bwd = row(_lower_bounds(hg_gamma_bwd)[layer])
    ho = _hgrn(proj, meta_proj, lb_fwd, lb_bwd, row(hg_norm_g[layer]), n_seq, seq)

    h_f32, h_bf = _merge(ao, ho, proj, x, w_branch_attn[layer].astype(BF16), w_branch_hgrn[layer].astype(BF16),
                         w_out[layer].astype(BF16), ln_g, ln_b, row(ln1_g[layer]), row(ln1_b[layer]),
                         tm=min(256, n_tok))

    idx, gate, rank, counts = _router(h_bf, w_router[layer].T.astype(BF16),
                                      router_bias[layer].astype(F32).reshape(N_EXPERTS, 1), tm=min(512, n_tok))
    dest, row_tok, row_gate, blk_e, n_real = _dispatch_tables(idx, gate, rank, counts[:, 0], n_tok)

    x_sorted = jnp.take(h_bf, row_tok, axis=0)
    y_sorted = _experts(x_sorted, row_gate.reshape(-1, 1), blk_e, n_real, w_ex_gate, w_ex_up, w_ex_down)
    routed = jnp.sum(jnp.take(y_sorted, dest.T, axis=0), axis=1)

    w_sh_gu = jnp.concatenate([w_sh_gate[layer], w_sh_up[layer]], axis=1).astype(BF16)
    y = _final(h_f32, h_bf, routed, w_sh_gu, w_sh_down[layer].astype(BF16), row(ln2_g[layer]), row(ln2_b[layer]),
               tm=min(512, n_tok))
    y_prompt = y[:bp * seq].reshape(bp, seq, d)
    y_sample = y[bp * seq:].reshape(bs, seq, d)
    return (y_prompt, y_sample)
```

```python
import functools
import math

import jax
import jax.numpy as jnp
from jax import lax
from jax.experimental import pallas as pl
from jax.experimental.pallas import tpu as pltpu

F32 = jnp.float32
BF16 = jnp.bfloat16
U32 = jnp.uint32

N_META = 16
A_HEADS = 8
A_HEAD_DIM = 64
H_HEADS = 8
H_EXPAND = 128
H_DV = 128
H_CHUNK = 64
N_EXPERTS = 256
TOP_K = 8
N_GROUPS = 8
TOPK_GROUPS = 4
GROUP_SIZE = N_EXPERTS // N_GROUPS
ROUTED_SCALE = 2.5
DEPTH = 1
DN_ALPHA = (2 * DEPTH) ** 0.25
EPS = 1e-5
LAMBDA_INIT = 0.8 - 0.6 * math.exp(-0.3 * 0)

META_PAD = 128
NEG_BIG = -1e30
EXPERT_ROWS = 256
HIGH_HALF = 0xFFFF0000


def _cparams(semantics, vmem_mb, **kw):
    return pltpu.CompilerParams(dimension_semantics=semantics, vmem_limit_bytes=vmem_mb * 1024 * 1024, **kw)


def _layer_norm_f32(x, g, b):
    mu = jnp.mean(x, axis=-1, keepdims=True)
    xc = x - mu
    var = jnp.mean(xc * xc, axis=-1, keepdims=True)
    return xc * lax.rsqrt(var + EPS) * g + b


def _resident(shape):
    nd = len(shape)
    return pl.BlockSpec(shape, lambda *_: (0,) * nd, pipeline_mode=pl.Buffered(1))


def _pack_bf16_pair(x):
    k = x.shape[1] // 2
    lo = lax.bitcast_convert_type(x[:, :k].astype(BF16).astype(F32), U32) >> 16
    hi = lax.bitcast_convert_type(x[:, k:].astype(BF16).astype(F32), U32) & jnp.uint32(HIGH_HALF)
    return lo | hi


def _unpack_bf16_pair(p):
    lo = lax.bitcast_convert_type(p << 16, F32).astype(BF16)
    hi = lax.bitcast_convert_type(p & jnp.uint32(HIGH_HALF), F32).astype(BF16)
    return lo, hi


def _dot_packed(p, w_ref):
    k = p.shape[1]
    lo, hi = _unpack_bf16_pair(p)
    return (jnp.dot(lo, w_ref[0:k, :], preferred_element_type=F32)
            + jnp.dot(hi, w_ref[k:, :], preferred_element_type=F32))


def _two_group_specs(tm, d, tp, extra_grid_axes):
    if extra_grid_axes:
        return (pl.BlockSpec((tm, d), lambda i, j: (jnp.minimum(i, tp - 1), 0), pipeline_mode=pl.Buffered(1)),
                pl.BlockSpec((tm, d), lambda i, j: (jnp.maximum(i - tp, 0), 0), pipeline_mode=pl.Buffered(1)))
    return (pl.BlockSpec((tm, d), lambda i: (jnp.minimum(i, tp - 1), 0)),
            pl.BlockSpec((tm, d), lambda i: (jnp.maximum(i - tp, 0), 0)))


def _ln_matmul_kernel(xp_ref, xs_ref, g_ref, b_ref, w_ref, o_ref, xn_ref, *, tp):
    i = pl.program_id(0)
    first_col = pl.program_id(1) == 0

    @pl.when(jnp.logical_and(first_col, i < tp))
    def _():
        xn_ref[...] = _layer_norm_f32(xp_ref[...], g_ref[...], b_ref[...]).astype(BF16)

    @pl.when(jnp.logical_and(first_col, i >= tp))
    def _():
        xn_ref[...] = _layer_norm_f32(xs_ref[...], g_ref[...], b_ref[...]).astype(BF16)

    o_ref[...] = jnp.dot(xn_ref[...], w_ref[...], preferred_element_type=F32).astype(o_ref.dtype)


def _ln_matmul(xp, xs, g, b, w, tm, tn):
    d = xp.shape[1]
    n = w.shape[1]
    tp, ts = xp.shape[0] // tm, xs.shape[0] // tm
    xp_spec, xs_spec = _two_group_specs(tm, d, tp, True)
    return pl.pallas_call(
        functools.partial(_ln_matmul_kernel, tp=tp),
        grid=(tp + ts, n // tn),
        in_specs=[
            xp_spec, xs_spec,
            pl.BlockSpec((1, d), lambda i, j: (0, 0)),
            pl.BlockSpec((1, d), lambda i, j: (0, 0)),
            pl.BlockSpec((d, tn), lambda i, j: (0, j)),
        ],
        out_specs=pl.BlockSpec((tm, tn), lambda i, j: (i, j)),
        out_shape=jax.ShapeDtypeStruct(((tp + ts) * tm, n), BF16),
        scratch_shapes=[pltpu.VMEM((tm, d), BF16)],
        compiler_params=_cparams(("parallel", "arbitrary"), 44),
        name="ln_matmul",
    )(xp, xs, g, b, w)


def _diff_attn_kernel(par_ref, q_ref, k_ref, v_ref, km_ref, vm_ref, g_ref, o_ref, kall_ref, vall_ref):
    h = pl.program_id(1)
    qi = pl.program_id(2)
    tq = q_ref.shape[0]
    n_keys = kall_ref.shape[0]

    @pl.when(qi == 0)
    def _():
        kall_ref[0:META_PAD, :] = km_ref[...]
        kall_ref[META_PAD:, :] = k_ref[...]
        vall_ref[0:META_PAD, :] = vm_ref[...]
        vall_ref[META_PAD:, :] = v_ref[...]

    lam = par_ref[0]
    slope = par_ref[1 + h]
    q = q_ref[...]
    lane = lax.broadcasted_iota(jnp.int32, (1, 2 * A_HEAD_DIM), 1)
    scale = jnp.asarray(A_HEAD_DIM ** -0.5, BF16)
    zero = jnp.zeros((), BF16)
    q1 = jnp.where(lane < A_HEAD_DIM, q, zero) * scale
    q2 = jnp.where(lane >= A_HEAD_DIM, q, zero) * scale
    k_all = kall_ref[...]
    nt = (((1,), (1,)), ((), ()))
    s1 = lax.dot_general(q1, k_all, nt, preferred_element_type=F32)
    s2 = lax.dot_general(q2, k_all, nt, preferred_element_type=F32)

    col = lax.broadcasted_iota(jnp.int32, (1, n_keys), 1)
    row = lax.broadcasted_iota(jnp.int32, (tq, 1), 0) + (qi * tq + META_PAD)
    dist = jnp.abs(row - col).astype(F32)
    bias = jnp.where(col < META_PAD - N_META, NEG_BIG, -slope * dist)

    def softmax_parts(s):
        s = s + bias
        m = jnp.max(s, axis=-1, keepdims=True)
        p = jnp.exp(s - m)
        return p, jnp.sum(p, axis=-1, keepdims=True)

    p1, l1 = softmax_parts(s1)
    p2, l2 = softmax_parts(s2)
    w = p1 * (1.0 / l1) - p2 * (lam / l2)
    o = jnp.dot(w.astype(BF16), vall_ref[...], preferred_element_type=F32)
    ms = jnp.mean(o * o, axis=-1, keepdims=True)
    o_ref[...] = (o * lax.rsqrt(ms + EPS) * g_ref[...]).astype(o_ref.dtype)


def _diff_attn(proj, meta_proj, params, subln_g, n_seq, seq, tq):
    n_tok = proj.shape[0]
    nq = seq // tq
    hw = 2 * A_HEAD_DIM
    return pl.pallas_call(
        _diff_attn_kernel,
        grid=(n_seq, A_HEADS, nq),
        in_specs=[
            pl.BlockSpec(memory_space=pltpu.SMEM),
            pl.BlockSpec((tq, hw), lambda b, h, i: (b * nq + i, h)),
            pl.BlockSpec((seq, hw), lambda b, h, i: (b, A_HEADS + h)),
            pl.BlockSpec((seq, hw), lambda b, h, i: (b, 2 * A_HEADS + h)),
            pl.BlockSpec((META_PAD, hw), lambda b, h, i: (0, A_HEADS + h)),
            pl.BlockSpec((META_PAD, hw), lambda b, h, i: (0, 2 * A_HEADS + h)),
            pl.BlockSpec((1, hw), lambda b, h, i: (0, 0)),
        ],
        out_specs=pl.BlockSpec((tq, hw), lambda b, h, i: (b * nq + i, h)),
        out_shape=jax.ShapeDtypeStruct((n_tok, A_HEADS * hw), BF16),
        scratch_shapes=[pltpu.VMEM((META_PAD + seq, hw), BF16), pltpu.VMEM((META_PAD + seq, hw), BF16)],
        compiler_params=_cparams(("parallel", "parallel", "arbitrary"), 48),
        name="diff_attn",
    )(params, proj, proj, proj, meta_proj, meta_proj, subln_g)


def _split_cumsum(tri, x):
    hi = x.astype(BF16)
    lo = (x - hi.astype(F32)).astype(BF16)
    return (jnp.dot(tri, hi, preferred_element_type=F32) + jnp.dot(tri, lo, preferred_element_type=F32))


def _hgrn_chunk(q, k, logf, v_bf, state_t, reverse):
    c = q.shape[0]
    r = lax.broadcasted_iota(jnp.int32, (c, c), 0)
    s = lax.broadcasted_iota(jnp.int32, (c, c), 1)
    if reverse:
        visible = s >= r
        ref_row, last_row = c - 1 - c // 2, 0
    else:
        visible = s <= r
        ref_row, last_row = c // 2, c - 1
    tri = jnp.where(visible, 1.0, 0.0).astype(BF16)
    b = _split_cumsum(tri, logf)
    b_ref = b[ref_row:ref_row + 1, :]
    b_last = b[last_row:last_row + 1, :]
    nt = (((1,), (1,)), ((), ()))
    tn = (((0,), (0,)), ((), ()))
    qd = (q * jnp.exp(b - b_ref)).astype(BF16)
    kd = (k * jnp.exp(b_ref - b)).astype(BF16)
    scores = lax.dot_general(qd, kd, nt, preferred_element_type=F32)
    scores = jnp.where(visible, scores, 0.0).astype(BF16)
    o_intra = jnp.dot(scores, v_bf, preferred_element_type=F32)
    qe = (q * jnp.exp(b)).astype(BF16)
    o_inter = lax.dot_general(qe, state_t.astype(BF16), nt, preferred_element_type=F32)
    ks = (k * jnp.exp(b_last - b)).astype(BF16)
    d_state_t = lax.dot_general(v_bf, ks, tn, preferred_element_type=F32)
    new_state_t = state_t * jnp.exp(b_last) + d_state_t
    return o_intra + o_inter, new_state_t


def _forget_gate(logit, lb):
    f = lb + (1.0 - lb) * jax.nn.sigmoid(logit)
    return 1.0 - f, jnp.log(f)


def _hgrn_kernel(q_ref, ff_ref, fb_ref, v_ref, hg_ref, mff_ref, mv_ref, lbf_ref, lbb_ref, g_ref,
                 o_ref, of_ref, ob_ref):
    seq = q_ref.shape[0]
    n_chunks = seq // H_CHUNK
    lbf = lbf_ref[...]
    lbb = lbb_ref[...]

    mrow = lax.broadcasted_iota(jnp.int32, (H_CHUNK, 1), 0)
    is_meta = mrow >= H_CHUNK - N_META
    mk, mlogf = _forget_gate(mff_ref[...].astype(F32), lbf)
    mk = jnp.where(is_meta, mk, 0.0)
    mlogf = jnp.where(is_meta, mlogf, 0.0)
    zeros_q = jnp.zeros((H_CHUNK, H_EXPAND), F32)
    state0 = jnp.zeros((H_DV, H_EXPAND), F32)
    _, state_f0 = _hgrn_chunk(zeros_q, mk, mlogf, mv_ref[...], state0, reverse=False)

    def body(i, carry):
        state_f, state_b = carry
        rf = pl.multiple_of(i * H_CHUNK, H_CHUNK)
        rb = pl.multiple_of((n_chunks - 1 - i) * H_CHUNK, H_CHUNK)

        q_f = jax.nn.silu(q_ref[pl.ds(rf, H_CHUNK), :].astype(F32))
        k_f, logf_f = _forget_gate(ff_ref[pl.ds(rf, H_CHUNK), :].astype(F32), lbf)
        o_f, state_f = _hgrn_chunk(q_f, k_f, logf_f, v_ref[pl.ds(rf, H_CHUNK), :], state_f, reverse=False)
        of_ref[pl.ds(rf, H_CHUNK), :] = o_f

        q_b = jax.nn.silu(q_ref[pl.ds(rb, H_CHUNK), :].astype(F32))
        k_b, logf_b = _forget_gate(fb_ref[pl.ds(rb, H_CHUNK), :].astype(F32), lbb)
        o_b, state_b = _hgrn_chunk(q_b, k_b, logf_b, v_ref[pl.ds(rb, H_CHUNK), :], state_b, reverse=True)
        ob_ref[pl.ds(rb, H_CHUNK), :] = o_b
        return state_f, state_b

    lax.fori_loop(0, n_chunks, body, (state_f0, state0))

    o = of_ref[...] + ob_ref[...]
    ms = jnp.mean(o * o, axis=-1, keepdims=True)
    normed = (o * lax.rsqrt(ms + EPS) * g_ref[...])
    o_ref[...] = (normed * jax.nn.sigmoid(hg_ref[...].astype(F32))).astype(o_ref.dtype)


def _hgrn(proj, meta_proj, lb_fwd, lb_bwd, norm_g, n_seq, seq):
    n_tok = proj.shape[0]
    w = H_DV
    col0 = 3 * A_HEADS
    meta_blk = META_PAD // H_CHUNK - 1

    def sec(k):
        return pl.BlockSpec((seq, w), lambda b, h, k=k: (b, col0 + k * H_HEADS + h))

    def meta_sec(k):
        return pl.BlockSpec((H_CHUNK, w), lambda b, h, k=k: (meta_blk, col0 + k * H_HEADS + h))

    return pl.pallas_call(
        _hgrn_kernel,
        grid=(n_seq, H_HEADS),
        in_specs=[
            sec(0), sec(1), sec(2), sec(3), sec(4),
            meta_sec(1), meta_sec(3),
            pl.BlockSpec((1, w), lambda b, h: (0, h)),
            pl.BlockSpec((1, w), lambda b, h: (0, h)),
            pl.BlockSpec((1, w), lambda b, h: (0, 0)),
        ],
        out_specs=pl.BlockSpec((seq, w), lambda b, h: (b, h)),
        out_shape=jax.ShapeDtypeStruct((n_tok, H_HEADS * w), BF16),
        scratch_shapes=[pltpu.VMEM((seq, w), F32), pltpu.VMEM((seq, w), F32)],
        compiler_params=_cparams(("parallel", "parallel"), 32),
        name="hgrn",
    )(proj, proj, proj, proj, proj, meta_proj, meta_proj, lb_fwd, lb_bwd, norm_g)


def _merge_kernel(ao_ref, ho_ref, ga_ref, gh_ref, xp_ref, xs_ref, wa_ref, wh_ref, wo_ref, lng_ref, lnb_ref,
                  l1g_ref, l1b_ref, hf_ref, hp_ref, *, tp):
    pa = jnp.dot(ao_ref[...], wa_ref[...], preferred_element_type=F32)
    ph = jnp.dot(ho_ref[...], wh_ref[...], preferred_element_type=F32)
    merged = (jax.nn.sigmoid(ga_ref[...].astype(F32)) * pa + jax.nn.sigmoid(gh_ref[...].astype(F32)) * ph)
    t = jnp.dot(merged.astype(BF16), wo_ref[...], preferred_element_type=F32)

    def finish(x_ref):
        x_in = _layer_norm_f32(x_ref[...], lng_ref[...], lnb_ref[...])
        h = _layer_norm_f32(DN_ALPHA * x_in + t, l1g_ref[...], l1b_ref[...])
        hf_ref[...] = h
        hp_ref[...] = _pack_bf16_pair(h)

    @pl.when(pl.program_id(0) < tp)
    def _():
        finish(xp_ref)

    @pl.when(pl.program_id(0) >= tp)
    def _():
        finish(xs_ref)


def _merge(ao, ho, proj, xp, xs, wa, wh, wo, ln_g, ln_b, l1g, l1b, tm):
    d = xp.shape[1]
    tp, ts = xp.shape[0] // tm, xs.shape[0] // tm
    n_tok = (tp + ts) * tm
    gate_blk0 = (proj.shape[1] - 2 * d) // d
    vec = pl.BlockSpec((1, d), lambda i: (0, 0))
    xp_spec, xs_spec = _two_group_specs(tm, d, tp, False)
    return pl.pallas_call(
        functools.partial(_merge_kernel, tp=tp),
        grid=(tp + ts,),
        in_specs=[
            pl.BlockSpec((tm, ao.shape[1]), lambda i: (i, 0)),
            pl.BlockSpec((tm, ho.shape[1]), lambda i: (i, 0)),
            pl.BlockSpec((tm, d), lambda i: (i, gate_blk0)),
            pl.BlockSpec((tm, d), lambda i: (i, gate_blk0 + 1)),
            xp_spec, xs_spec,
            _resident(wa.shape), _resident(wh.shape), _resident(wo.shape),
            vec, vec, vec, vec,
        ],
        out_specs=[pl.BlockSpec((tm, d), lambda i: (i, 0)), pl.BlockSpec((tm, d // 2), lambda i: (i, 0))],
        out_shape=[jax.ShapeDtypeStruct((n_tok, d), F32), jax.ShapeDtypeStruct((n_tok, d // 2), U32)],
        compiler_params=_cparams(("parallel",), 56),
        name="merge",
    )(ao, ho, proj, proj, xp, xs, wa, wh, wo, ln_g, ln_b, l1g, l1b)


def _first_argmax(x, iota, size):
    m = jnp.max(x, axis=0, keepdims=True)
    idx = jnp.min(jnp.where(x == m, iota, size), axis=0, keepdims=True)
    return m, idx


def _router_kernel(h_ref, wrt_ref, bias_ref, idx_ref, gate_ref, rank_ref, cnt_ref, base_ref):
    tm = h_ref.shape[0]
    half = h_ref.shape[1]

    @pl.when(pl.program_id(0) == 0)
    def _():
        base_ref[...] = jnp.zeros_like(base_ref)

    nt = (((1,), (1,)), ((), ()))
    h_lo, h_hi = _unpack_bf16_pair(h_ref[...])
    logits = (lax.dot_general(wrt_ref[:, 0:half], h_lo, nt, preferred_element_type=F32)
              + lax.dot_general(wrt_ref[:, half:], h_hi, nt, preferred_element_type=F32))
    scores = jax.nn.sigmoid(logits)
    biased = scores + bias_ref[...]
    neg_inf = jnp.asarray(-jnp.inf, F32)

    iota_g = lax.broadcasted_iota(jnp.int32, (GROUP_SIZE, tm), 0)
    iota_n = lax.broadcasted_iota(jnp.int32, (N_GROUPS, tm), 0)
    grp = jnp.full((N_GROUPS, tm), neg_inf, F32)
    for g in range(N_GROUPS):
        xg = biased[g * GROUP_SIZE:(g + 1) * GROUP_SIZE, :]
        m1, i1 = _first_argmax(xg, iota_g, GROUP_SIZE)
        m2 = jnp.max(jnp.where(iota_g == i1, neg_inf, xg), axis=0, keepdims=True)
        grp = jnp.where(iota_n == g, m1 + m2, grp)

    keep_f = jnp.zeros((N_GROUPS, tm), F32)
    for _ in range(TOPK_GROUPS):
        _, ig = _first_argmax(grp, iota_n, N_GROUPS)
        sel = iota_n == ig
        keep_f = jnp.where(sel, 1.0, keep_f)
        grp = jnp.where(sel, neg_inf, grp)

    masked = jnp.concatenate(
        [jnp.where(keep_f[g:g + 1, :] > 0.5, biased[g * GROUP_SIZE:(g + 1) * GROUP_SIZE, :], neg_inf)
         for g in range(N_GROUPS)], axis=0)

    iota_e = lax.broadcasted_iota(jnp.int32, (N_EXPERTS, tm), 0)
    chosen = jnp.zeros((N_EXPERTS, tm), F32)
    idxs, gates = [], []
    for _ in range(TOP_K):
        _, ie = _first_argmax(masked, iota_e, N_EXPERTS)
        sel = iota_e == ie
        gates.append(jnp.sum(jnp.where(sel, scores, 0.0), axis=0, keepdims=True))
        idxs.append(ie)
        chosen = jnp.where(sel, 1.0, chosen)
        masked = jnp.where(sel, neg_inf, masked)

    denom = gates[0]
    for gk in gates[1:]:
        denom = denom + gk

    r = lax.broadcasted_iota(jnp.int32, (tm, tm), 0)
    c = lax.broadcasted_iota(jnp.int32, (tm, tm), 1)
    before = jnp.where(r < c, 1.0, 0.0).astype(BF16)
    pos = jnp.dot(chosen.astype(BF16), before, preferred_element_type=F32) + base_ref[...]
    for k in range(TOP_K):
        sel = iota_e == idxs[k]
        rank_ref[k:k + 1, :] = jnp.sum(jnp.where(sel, pos, 0.0), axis=0, keepdims=True).astype(jnp.int32)
        idx_ref[k:k + 1, :] = idxs[k]
        gate_ref[k:k + 1, :] = gates[k] / denom * ROUTED_SCALE

    base_ref[...] = base_ref[...] + jnp.sum(chosen, axis=1, keepdims=True)
    cnt_ref[...] = base_ref[...]


def _router(h_pk, w_router_t, bias_col, tm):
    n_tok, half = h_pk.shape
    row_blk = pl.BlockSpec((TOP_K, tm), lambda i: (0, i))
    return pl.pallas_call(
        _router_kernel,
        grid=(n_tok // tm,),
        in_specs=[
            pl.BlockSpec((tm, half), lambda i: (i, 0)),
            _resident(w_router_t.shape),
            pl.BlockSpec((N_EXPERTS, 1), lambda i: (0, 0)),
        ],
        out_specs=[row_blk, row_blk, row_blk, pl.BlockSpec((N_EXPERTS, 1), lambda i: (0, 0))],
        out_shape=[
            jax.ShapeDtypeStruct((TOP_K, n_tok), jnp.int32),
            jax.ShapeDtypeStruct((TOP_K, n_tok), F32),
            jax.ShapeDtypeStruct((TOP_K, n_tok), jnp.int32),
            jax.ShapeDtypeStruct((N_EXPERTS, 1), F32),
        ],
        scratch_shapes=[pltpu.VMEM((N_EXPERTS, 1), F32)],
        compiler_params=_cparams(("arbitrary",), 32),
        name="router",
    )(h_pk, w_router_t, bias_col)


def _dispatch_kernel(dest_ref, h_ref, xs_ref, sem):
    tm = h_ref.shape[0]

    def row_copy(r, k):
        return pltpu.make_async_copy(h_ref.at[pl.ds(r, 1)], xs_ref.at[pl.ds(dest_ref[0, k * tm + r], 1)], sem)

    def start(r, carry):
        for k in range(TOP_K):
            row_copy(r, k).start()
        return carry

    def wait(r, carry):
        for k in range(TOP_K):
            row_copy(r, k).wait()
        return carry

    lax.fori_loop(0, tm, start, 0)
    lax.fori_loop(0, tm, wait, 0)


def _dispatch(h_pk, dest_tiles, tm):
    n_tok, half = h_pk.shape
    return pl.pallas_call(
        _dispatch_kernel,
        grid=(n_tok // tm,),
        in_specs=[
            pl.BlockSpec((None, 1, TOP_K * tm), lambda i: (i, 0, 0), memory_space=pltpu.SMEM),
            pl.BlockSpec((tm, half), lambda i: (i, 0)),
        ],
        out_specs=pl.BlockSpec(memory_space=pl.ANY),
        out_shape=jax.ShapeDtypeStruct((n_tok * TOP_K, half), U32),
        scratch_shapes=[pltpu.SemaphoreType.DMA],
        compiler_params=_cparams(("arbitrary",), 32, disable_bounds_checks=True),
        name="dispatch",
    )(dest_tiles, h_pk)


def _experts_kernel(blk_ref, e_ref, lo_ref, hi_ref, first_ref, x_ref, wg_ref, wu_ref, wd_ref, y_ref,
                    wgu_bf, wd_bf):
    i = pl.program_id(0)
    ff = wg_ref.shape[1]
    prev_e = e_ref[jnp.maximum(i - 1, 0)]
    lo = lo_ref[i]
    hi = hi_ref[i]

    @pl.when(jnp.logical_or(i == 0, e_ref[i] != prev_e))
    def _():
        wgu_bf[:, 0:ff] = wg_ref[...].astype(BF16)
        wgu_bf[:, ff:] = wu_ref[...].astype(BF16)
        wd_bf[...] = wd_ref[...].astype(BF16)

    @pl.when(hi > lo)
    def _():
        gu = _dot_packed(x_ref[...], wgu_bf)
        hid = (jax.nn.silu(gu[:, 0:ff]) * gu[:, ff:]).astype(BF16)
        y = jnp.dot(hid, wd_bf[...], preferred_element_type=F32)
        row = lax.broadcasted_iota(jnp.int32, (y.shape[0], 1), 0)
        y = jnp.where(jnp.logical_and(row >= lo, row < hi), y, 0.0)

        @pl.when(first_ref[i] == 1)
        def _():
            y_ref[...] = y

        @pl.when(first_ref[i] == 0)
        def _():
            y_ref[...] += y


def _experts(x_sorted, items, w_gate, w_up, w_down):
    n_rows, half = x_sorted.shape
    d = 2 * half
    ff = w_gate.shape[-1]
    n_items = items[0].shape[0]
    grid_spec = pltpu.PrefetchScalarGridSpec(
        num_scalar_prefetch=5,
        grid=(n_items,),
        in_specs=[
            pl.BlockSpec((EXPERT_ROWS, half), lambda i, blk, e, lo, hi, first: (blk[i], 0)),
            pl.BlockSpec((None, None, d, ff), lambda i, blk, e, lo, hi, first: (0, e[i], 0, 0)),
            pl.BlockSpec((None, None, d, ff), lambda i, blk, e, lo, hi, first: (0, e[i], 0, 0)),
            pl.BlockSpec((None, None, ff, d), lambda i, blk, e, lo, hi, first: (0, e[i], 0, 0)),
        ],
        out_specs=pl.BlockSpec((EXPERT_ROWS, d), lambda i, blk, e, lo, hi, first: (blk[i], 0)),
        scratch_shapes=[pltpu.VMEM((d, 2 * ff), BF16), pltpu.VMEM((ff, d), BF16)],
    )
    return pl.pallas_call(
        _experts_kernel,
        grid_spec=grid_spec,
        out_shape=jax.ShapeDtypeStruct((n_rows, d), F32),
        compiler_params=_cparams(("arbitrary",), 56),
        name="experts",
    )(*items, x_sorted, w_gate, w_up, w_down)


def _final_kernel(dest_ref, gate_ref, hf_ref, hp_ref, y_ref, wgu_ref, wd_ref, g_ref, b_ref, op_ref, os_ref,
                  ybuf, sem, *, tp):
    tm = hf_ref.shape[0]
    ff = wd_ref.shape[0]

    def row_copy(r, k):
        return pltpu.make_async_copy(y_ref.at[pl.ds(dest_ref[0, k * tm + r], 1)], ybuf.at[k, pl.ds(r, 1)], sem)

    def start(r, carry):
        for k in range(TOP_K):
            row_copy(r, k).start()
        return carry

    def wait(r, carry):
        for k in range(TOP_K):
            row_copy(r, k).wait()
        return carry

    lax.fori_loop(0, tm, start, 0)

    gu = _dot_packed(hp_ref[...], wgu_ref)
    hid = (jax.nn.silu(gu[:, 0:ff]) * gu[:, ff:]).astype(BF16)
    acc = DN_ALPHA * hf_ref[...] + jnp.dot(hid, wd_ref[...], preferred_element_type=F32)

    lax.fori_loop(0, tm, wait, 0)
    gate = gate_ref[...]
    for k in range(TOP_K):
        acc = acc + ybuf[k] * gate[:, k:k + 1]
    out = _layer_norm_f32(acc, g_ref[...], b_ref[...])

    @pl.when(pl.program_id(0) < tp)
    def _():
        op_ref[...] = out

    @pl.when(pl.program_id(0) >= tp)
    def _():
        os_ref[...] = out


def _final(h_f32, h_pk, y_sorted, dest_tiles, gate_t, w_gu, w_d, g, b, n_prompt, tm):
    n_tok, d = h_f32.shape
    tp = n_prompt // tm
    ts = n_tok // tm - tp
    vec = pl.BlockSpec((1, d), lambda i: (0, 0))
    op_spec, os_spec = _two_group_specs(tm, d, tp, False)
    return pl.pallas_call(
        functools.partial(_final_kernel, tp=tp),
        grid=(tp + ts,),
        in_specs=[
            pl.BlockSpec((None, 1, TOP_K * tm), lambda i: (i, 0, 0), memory_space=pltpu.SMEM),
            pl.BlockSpec((tm, TOP_K), lambda i: (i, 0)),
            pl.BlockSpec((tm, d), lambda i: (i, 0)),
            pl.BlockSpec((tm, d // 2), lambda i: (i, 0)),
            pl.BlockSpec(memory_space=pl.ANY),
            _resident(w_gu.shape), _resident(w_d.shape), vec, vec,
        ],
        out_specs=[op_spec, os_spec],
        out_shape=[jax.ShapeDtypeStruct((tp * tm, d), F32), jax.ShapeDtypeStruct((ts * tm, d), F32)],
        scratch_shapes=[pltpu.VMEM((TOP_K, tm, d), F32), pltpu.SemaphoreType.DMA],
        compiler_params=_cparams(("arbitrary",), 48, disable_bounds_checks=True),
        name="final",
    )(dest_tiles, gate_t, h_f32, h_pk, y_sorted, w_gu, w_d, g, b)


def _lower_bounds(gamma):
    return jnp.cumsum(jax.nn.softmax(gamma.astype(F32), axis=0), axis=0)


def _tile_major(dest, tm):
    k, n = dest.shape
    return jnp.transpose(dest.reshape(k, n // tm, tm), (1, 0, 2)).reshape(n // tm, 1, k * tm)


def _work_items(counts, n_assign):
    rows = EXPERT_ROWS
    n_items = n_assign // rows + N_EXPERTS - 1
    end = jnp.cumsum(counts)
    start = end - counts
    first_blk = start // rows
    last_blk = jnp.maximum(end - 1, 0) // rows
    per_e = jnp.where(counts > 0, last_blk - first_blk + 1, 0)
    it_end = jnp.cumsum(per_e)
    it_start = it_end - per_e
    total = it_end[-1]
    i = jnp.arange(n_items, dtype=jnp.int32)
    ic = jnp.minimum(i, total - 1)
    e = jnp.sum((it_end[None, :] <= ic[:, None]).astype(jnp.int32), axis=1)
    blk = first_blk[e] + (ic - it_start[e])
    lo = jnp.maximum(start[e], blk * rows) - blk * rows
    hi = jnp.minimum(end[e], (blk + 1) * rows) - blk * rows
    valid = i < total
    hi = jnp.where(valid, hi, lo)
    prev_blk = jnp.concatenate([jnp.full((1,), -1, jnp.int32), blk[:-1]])
    first = jnp.where(jnp.logical_and(valid, blk != prev_blk), 1, 0)
    as_i32 = lambda v: v.astype(jnp.int32)
    return start, (as_i32(blk), as_i32(e), as_i32(lo), as_i32(hi), as_i32(first))


def kernel(x_prompt, x_sample, meta_tokens, ln_in_g, ln_in_b, hg_gamma_fwd, hg_gamma_bwd, w_in, lambda_q1, lambda_k1, lambda_q2, lambda_k2, attn_subln_g, hg_norm_g, w_branch_attn, w_branch_hgrn, w_out, ln1_g, ln1_b, w_router, router_bias, w_sh_gate, w_sh_up, w_sh_down, w_ex_gate, w_ex_up, w_ex_down, ln2_g, ln2_b):
    bp, seq, d = x_prompt.shape
    bs, seq_s, _ = x_sample.shape
    assert seq == seq_s and w_in.shape[0] == DEPTH
    n_seq = bp + bs
    n_tok = n_seq * seq
    layer = 0

    xp = x_prompt.reshape(bp * seq, d)
    xs = x_sample.reshape(bs * seq, d)
    row = lambda v: v.reshape(1, -1).astype(F32)
    ln_g, ln_b = row(ln_in_g), row(ln_in_b)
    w_in_bf = w_in[layer].astype(BF16)

    proj = _ln_matmul(xp, xs, ln_g, ln_b, w_in_bf, tm=min(1024, bs * seq), tn=1024)
    meta = meta_tokens.astype(F32)
    meta_proj = _ln_matmul(meta, meta, ln_g, ln_b, w_in_bf, tm=N_META, tn=1024)[:N_META]
    meta_proj = jnp.pad(meta_proj, ((META_PAD - N_META, 0), (0, 0)))

    lam = (jnp.exp(jnp.sum(lambda_q1[layer].astype(F32) * lambda_k1[layer].astype(F32)))
           - jnp.exp(jnp.sum(lambda_q2[layer].astype(F32) * lambda_k2[layer].astype(F32))) + LAMBDA_INIT)
    slopes = 2.0 ** (-8.0 * jnp.arange(1, A_HEADS + 1, dtype=F32) / A_HEADS)
    attn_params = jnp.concatenate([lam.reshape(1), slopes]).astype(F32)
    subln_g = row(attn_subln_g[layer]) * (1.0 - LAMBDA_INIT)
    ao = _diff_attn(proj, meta_proj, attn_params, subln_g, n_seq, seq, tq=min(256, seq))

    lb_fwd = row(_lower_bounds(hg_gamma_fwd)[layer])
    lb_bwd = row(_lower_bounds(hg_gamma_bwd)[layer])
    ho = _hgrn(proj, meta_proj, lb_fwd, lb_bwd, row(hg_norm_g[layer]), n_seq, seq)

    h_f32, h_pk = _merge(ao, ho, proj, xp, xs, w_branch_attn[layer].astype(BF16),
                         w_branch_hgrn[layer].astype(BF16), w_out[layer].astype(BF16), ln_g, ln_b,
                         row(ln1_g[layer]), row(ln1_b[layer]), tm=min(256, bs * seq))

    idx, gate, rank, counts = _router(h_pk, w_router[layer].T.astype(BF16),
                                      router_bias[layer].astype(F32).reshape(N_EXPERTS, 1), tm=min(512, n_tok))
    start, items = _work_items(counts[:, 0].astype(jnp.int32), n_tok * TOP_K)
    dest = start[idx] + rank

    tm_d = min(256, n_tok)
    x_sorted = _dispatch(h_pk, _tile_major(dest, tm_d), tm_d)
    y_sorted = _experts(x_sorted, items, w_ex_gate, w_ex_up, w_ex_down)

    w_sh_gu = jnp.concatenate([w_sh_gate[layer], w_sh_up[layer]], axis=1).astype(BF16)
    tm_f = min(128, bs * seq)
    y_prompt, y_sample = _final(h_f32, h_pk, y_sorted, _tile_major(dest, tm_f), gate.T, w_sh_gu,
                                w_sh_down[layer].astype(BF16), row(ln2_g[layer]), row(ln2_b[layer]),
                                bp * seq, tm_f)
    return (y_prompt.reshape(bp, seq, d), y_sample.reshape(bs, seq, d))
```

```python
import functools
import math

import jax
import jax.numpy as jnp
from jax import lax
from jax.experimental import pallas as pl
from jax.experimental.pallas import tpu as pltpu

F32 = jnp.float32
BF16 = jnp.bfloat16
U32 = jnp.uint32

N_META = 16
A_HEADS = 8
A_HEAD_DIM = 64
H_HEADS = 8
H_EXPAND = 128
H_DV = 128
H_CHUNK = 64
N_EXPERTS = 256
TOP_K = 8
N_GROUPS = 8
TOPK_GROUPS = 4
GROUP_SIZE = N_EXPERTS // N_GROUPS
ROUTED_SCALE = 2.5
DEPTH = 1
DN_ALPHA = (2 * DEPTH) ** 0.25
EPS = 1e-5
LAMBDA_INIT = 0.8 - 0.6 * math.exp(-0.3 * 0)

LANES = 128
HGRN_HEADS_PER_STEP = 4
META_PAD = 128
NEG_BIG = -1e30
EXPERT_ROWS = 256
HIGH_HALF = 0xFFFF0000


def _cparams(semantics, vmem_mb, **kw):
    return pltpu.CompilerParams(dimension_semantics=semantics, vmem_limit_bytes=vmem_mb * 1024 * 1024, **kw)


def _layer_norm_f32(x, g, b):
    mu = jnp.mean(x, axis=-1, keepdims=True)
    xc = x - mu
    var = jnp.mean(xc * xc, axis=-1, keepdims=True)
    return xc * lax.rsqrt(var + EPS) * g + b


def _resident(shape):
    nd = len(shape)
    return pl.BlockSpec(shape, lambda *_: (0,) * nd, pipeline_mode=pl.Buffered(1))


def _pack_bf16_pair(x):
    k = x.shape[1] // 2
    lo = lax.bitcast_convert_type(x[:, :k].astype(BF16).astype(F32), U32) >> 16
    hi = lax.bitcast_convert_type(x[:, k:].astype(BF16).astype(F32), U32) & jnp.uint32(HIGH_HALF)
    return lo | hi


def _unpack_bf16_pair(p):
    lo = lax.bitcast_convert_type(p << 16, F32).astype(BF16)
    hi = lax.bitcast_convert_type(p & jnp.uint32(HIGH_HALF), F32).astype(BF16)
    return lo, hi


def _dot_packed(p, w_ref):
    k = p.shape[1]
    lo, hi = _unpack_bf16_pair(p)
    return (jnp.dot(lo, w_ref[0:k, :], preferred_element_type=F32)
            + jnp.dot(hi, w_ref[k:, :], preferred_element_type=F32))


def _two_group_specs(tm, d, tp, extra_grid_axes):
    if extra_grid_axes:
        return (pl.BlockSpec((tm, d), lambda i, j: (jnp.minimum(i, tp - 1), 0), pipeline_mode=pl.Buffered(1)),
                pl.BlockSpec((tm, d), lambda i, j: (jnp.maximum(i - tp, 0), 0), pipeline_mode=pl.Buffered(1)))
    return (pl.BlockSpec((tm, d), lambda i: (jnp.minimum(i, tp - 1), 0)),
            pl.BlockSpec((tm, d), lambda i: (jnp.maximum(i - tp, 0), 0)))


def _ln_matmul_kernel(xp_ref, xs_ref, g_ref, b_ref, w_ref, o_ref, xn_ref, *, tp):
    i = pl.program_id(0)
    first_col = pl.program_id(1) == 0

    @pl.when(jnp.logical_and(first_col, i < tp))
    def _():
        xn_ref[...] = _layer_norm_f32(xp_ref[...], g_ref[...], b_ref[...]).astype(BF16)

    @pl.when(jnp.logical_and(first_col, i >= tp))
    def _():
        xn_ref[...] = _layer_norm_f32(xs_ref[...], g_ref[...], b_ref[...]).astype(BF16)

    o_ref[...] = jnp.dot(xn_ref[...], w_ref[...], preferred_element_type=F32).astype(o_ref.dtype)


def _ln_matmul(xp, xs, g, b, w, tm, tn):
    d = xp.shape[1]
    n = w.shape[1]
    tp, ts = xp.shape[0] // tm, xs.shape[0] // tm
    xp_spec, xs_spec = _two_group_specs(tm, d, tp, True)
    return pl.pallas_call(
        functools.partial(_ln_matmul_kernel, tp=tp),
        grid=(tp + ts, n // tn),
        in_specs=[
            xp_spec, xs_spec,
            pl.BlockSpec((1, d), lambda i, j: (0, 0)),
            pl.BlockSpec((1, d), lambda i, j: (0, 0)),
            pl.BlockSpec((d, tn), lambda i, j: (0, j)),
        ],
        out_specs=pl.BlockSpec((tm, tn), lambda i, j: (i, j)),
        out_shape=jax.ShapeDtypeStruct(((tp + ts) * tm, n), BF16),
        scratch_shapes=[pltpu.VMEM((tm, d), BF16)],
        compiler_params=_cparams(("parallel", "arbitrary"), 44),
        name="ln_matmul",
    )(xp, xs, g, b, w)


def _diff_attn_kernel(par_ref, q_ref, k_ref, v_ref, km_ref, vm_ref, g_ref, o_ref, kall_ref, vall_ref, bias_ref):
    h = pl.program_id(0)
    b = pl.program_id(1)
    qi = pl.program_id(2)
    tq = q_ref.shape[0]
    n_keys = kall_ref.shape[0]
    shift = bias_ref.shape[1] - n_keys

    @pl.when(jnp.logical_and(b == 0, qi == 0))
    def _():
        i = lax.broadcasted_iota(jnp.int32, (tq, 1), 0)
        j = lax.broadcasted_iota(jnp.int32, (1, bias_ref.shape[1]), 1)
        bias_ref[...] = -par_ref[1 + h] * jnp.abs(i + (META_PAD + shift) - j).astype(F32)

    @pl.when(qi == 0)
    def _():
        kall_ref[0:META_PAD, :] = km_ref[...]
        kall_ref[META_PAD:, :] = k_ref[...]
        vall_ref[0:META_PAD, :] = vm_ref[...]
        vall_ref[META_PAD:, :] = v_ref[...]

    lam = par_ref[0]
    q = q_ref[...]
    lane = lax.broadcasted_iota(jnp.int32, (1, 2 * A_HEAD_DIM), 1)
    scale = jnp.asarray(A_HEAD_DIM ** -0.5, BF16)
    zero = jnp.zeros((), BF16)
    q1 = jnp.where(lane < A_HEAD_DIM, q, zero) * scale
    q2 = jnp.where(lane >= A_HEAD_DIM, q, zero) * scale
    k_all = kall_ref[...]
    nt = (((1,), (1,)), ((), ()))
    s1 = lax.dot_general(q1, k_all, nt, preferred_element_type=F32)
    s2 = lax.dot_general(q2, k_all, nt, preferred_element_type=F32)

    off = pl.multiple_of(shift - qi * tq, LANES)
    lane_m = lax.broadcasted_iota(jnp.int32, (1, META_PAD), 1)
    pad_mask = jnp.where(lane_m < META_PAD - N_META, NEG_BIG, 0.0)
    bias_meta = bias_ref[:, pl.ds(off, META_PAD)] + pad_mask
    bias_seq = bias_ref[:, pl.ds(off + META_PAD, n_keys - META_PAD)]

    def softmax_parts(s):
        s_meta = s[:, 0:META_PAD] + bias_meta
        s_seq = s[:, META_PAD:] + bias_seq
        m = jnp.maximum(jnp.max(s_meta, axis=-1, keepdims=True), jnp.max(s_seq, axis=-1, keepdims=True))
        p_meta = jnp.exp(s_meta - m)
        p_seq = jnp.exp(s_seq - m)
        l = jnp.sum(p_meta, axis=-1, keepdims=True) + jnp.sum(p_seq, axis=-1, keepdims=True)
        return p_meta.astype(BF16), p_seq.astype(BF16), l

    p1_meta, p1_seq, l1 = softmax_parts(s1)
    p2_meta, p2_seq, l2 = softmax_parts(s2)
    a1 = (1.0 / l1).astype(BF16)
    a2 = (lam / l2).astype(BF16)
    w_meta = p1_meta * a1 - p2_meta * a2
    w_seq = p1_seq * a1 - p2_seq * a2
    o = (jnp.dot(w_meta, vall_ref[0:META_PAD, :], preferred_element_type=F32)
         + jnp.dot(w_seq, vall_ref[META_PAD:, :], preferred_element_type=F32))
    ms = jnp.mean(o * o, axis=-1, keepdims=True)
    o_ref[...] = (o * lax.rsqrt(ms + EPS) * g_ref[...]).astype(o_ref.dtype)


def _diff_attn(proj, meta_proj, params, subln_g, n_seq, seq, tq):
    n_tok = proj.shape[0]
    nq = seq // tq
    hw = 2 * A_HEAD_DIM
    return pl.pallas_call(
        _diff_attn_kernel,
        grid=(A_HEADS, n_seq, nq),
        in_specs=[
            pl.BlockSpec(memory_space=pltpu.SMEM),
            pl.BlockSpec((tq, hw), lambda h, b, i: (b * nq + i, h)),
            pl.BlockSpec((seq, hw), lambda h, b, i: (b, A_HEADS + h)),
            pl.BlockSpec((seq, hw), lambda h, b, i: (b, 2 * A_HEADS + h)),
            pl.BlockSpec((META_PAD, hw), lambda h, b, i: (0, A_HEADS + h)),
            pl.BlockSpec((META_PAD, hw), lambda h, b, i: (0, 2 * A_HEADS + h)),
            pl.BlockSpec((1, hw), lambda h, b, i: (0, 0)),
        ],
        out_specs=pl.BlockSpec((tq, hw), lambda h, b, i: (b * nq + i, h)),
        out_shape=jax.ShapeDtypeStruct((n_tok, A_HEADS * hw), BF16),
        scratch_shapes=[pltpu.VMEM((META_PAD + seq, hw), BF16), pltpu.VMEM((META_PAD + seq, hw), BF16),
                        pltpu.VMEM((tq, META_PAD + 2 * seq - tq), F32)],
        compiler_params=_cparams(("arbitrary", "arbitrary", "arbitrary"), 48),
        name="diff_attn",
    )(params, proj, proj, proj, meta_proj, meta_proj, subln_g)


def _split_cumsum(tri, x):
    hi = x.astype(BF16)
    lo = (x - hi.astype(F32)).astype(BF16)
    return (jnp.dot(tri, hi, preferred_element_type=F32) + jnp.dot(tri, lo, preferred_element_type=F32))


def _hgrn_chunk(q, k, logf, v_bf, state_t, reverse):
    c = q.shape[0]
    r = lax.broadcasted_iota(jnp.int32, (c, c), 0)
    s = lax.broadcasted_iota(jnp.int32, (c, c), 1)
    if reverse:
        visible = s >= r
        ref_row, last_row = c - 1 - c // 2, 0
    else:
        visible = s <= r
        ref_row, last_row = c // 2, c - 1
    tri = jnp.where(visible, 1.0, 0.0).astype(BF16)
    b = _split_cumsum(tri, logf)
    b_ref = b[ref_row:ref_row + 1, :]
    b_last = b[last_row:last_row + 1, :]
    nt = (((1,), (1,)), ((), ()))
    tn = (((0,), (0,)), ((), ()))
    qd = (q * jnp.exp(b - b_ref)).astype(BF16)
    kd = (k * jnp.exp(b_ref - b)).astype(BF16)
    scores = lax.dot_general(qd, kd, nt, preferred_element_type=F32)
    scores = jnp.where(visible, scores, 0.0).astype(BF16)
    o_intra = jnp.dot(scores, v_bf, preferred_element_type=F32)
    qe = (q * jnp.exp(b)).astype(BF16)
    o_inter = lax.dot_general(qe, state_t.astype(BF16), nt, preferred_element_type=F32)
    ks = (k * jnp.exp(b_last - b)).astype(BF16)
    d_state_t = lax.dot_general(v_bf, ks, tn, preferred_element_type=F32)
    new_state_t = state_t * jnp.exp(b_last) + d_state_t
    return o_intra + o_inter, new_state_t


def _forget_gate(logit, lb):
    f = lb + (1.0 - lb) * jax.nn.sigmoid(logit)
    return 1.0 - f, jnp.log(f)


def _hgrn_kernel(q_ref, ff_ref, fb_ref, v_ref, hg_ref, mff_ref, mv_ref, lbf_ref, lbb_ref, g_ref,
                 o_ref, of_ref, ob_ref):
    seq = q_ref.shape[0]
    n_heads = q_ref.shape[1] // H_DV
    n_chunks = seq // H_CHUNK
    head = lambda j: pl.ds(j * H_DV, H_DV)

    mrow = lax.broadcasted_iota(jnp.int32, (H_CHUNK, 1), 0)
    is_meta = mrow >= H_CHUNK - N_META
    zeros_q = jnp.zeros((H_CHUNK, H_EXPAND), F32)
    state0 = jnp.zeros((H_DV, H_EXPAND), F32)
    init_f = []
    for j in range(n_heads):
        mk, mlogf = _forget_gate(mff_ref[:, head(j)].astype(F32), lbf_ref[:, head(j)])
        mk = jnp.where(is_meta, mk, 0.0)
        mlogf = jnp.where(is_meta, mlogf, 0.0)
        init_f.append(_hgrn_chunk(zeros_q, mk, mlogf, mv_ref[:, head(j)], state0, reverse=False)[1])

    def body(i, carry):
        states_f, states_b = carry
        rows_f = pl.ds(pl.multiple_of(i * H_CHUNK, H_CHUNK), H_CHUNK)
        rows_b = pl.ds(pl.multiple_of((n_chunks - 1 - i) * H_CHUNK, H_CHUNK), H_CHUNK)
        new_f, new_b = [], []
        for j in range(n_heads):
            q_f = jax.nn.silu(q_ref[rows_f, head(j)].astype(F32))
            k_f, logf_f = _forget_gate(ff_ref[rows_f, head(j)].astype(F32), lbf_ref[:, head(j)])
            o_f, s_f = _hgrn_chunk(q_f, k_f, logf_f, v_ref[rows_f, head(j)], states_f[j], reverse=False)
            of_ref[rows_f, head(j)] = o_f
            new_f.append(s_f)

            q_b = jax.nn.silu(q_ref[rows_b, head(j)].astype(F32))
            k_b, logf_b = _forget_gate(fb_ref[rows_b, head(j)].astype(F32), lbb_ref[:, head(j)])
            o_b, s_b = _hgrn_chunk(q_b, k_b, logf_b, v_ref[rows_b, head(j)], states_b[j], reverse=True)
            ob_ref[rows_b, head(j)] = o_b
            new_b.append(s_b)
        return tuple(new_f), tuple(new_b)

    lax.fori_loop(0, n_chunks, body, (tuple(init_f), (state0,) * n_heads))

    for j in range(n_heads):
        o = of_ref[:, head(j)] + ob_ref[:, head(j)]
        ms = jnp.mean(o * o, axis=-1, keepdims=True)
        normed = o * lax.rsqrt(ms + EPS) * g_ref[...]
        o_ref[:, head(j)] = (normed * jax.nn.sigmoid(hg_ref[:, head(j)].astype(F32))).astype(o_ref.dtype)


def _hgrn(proj, meta_proj, lb_fwd, lb_bwd, norm_g, n_seq, seq):
    n_tok = proj.shape[0]
    hps = HGRN_HEADS_PER_STEP
    w = hps * H_DV
    col0 = 3 * A_HEADS * (2 * A_HEAD_DIM) // w
    per_sec = H_HEADS // hps
    meta_blk = META_PAD // H_CHUNK - 1

    def sec(k):
        return pl.BlockSpec((seq, w), lambda b, h, k=k: (b, col0 + k * per_sec + h))

    def meta_sec(k):
        return pl.BlockSpec((H_CHUNK, w), lambda b, h, k=k: (meta_blk, col0 + k * per_sec + h))

    return pl.pallas_call(
        _hgrn_kernel,
        grid=(n_seq, per_sec),
        in_specs=[
            sec(0), sec(1), sec(2), sec(3), sec(4),
            meta_sec(1), meta_sec(3),
            pl.BlockSpec((1, w), lambda b, h: (0, h)),
            pl.BlockSpec((1, w), lambda b, h: (0, h)),
            pl.BlockSpec((1, H_DV), lambda b, h: (0, 0)),
        ],
        out_specs=pl.BlockSpec((seq, w), lambda b, h: (b, h)),
        out_shape=jax.ShapeDtypeStruct((n_tok, H_HEADS * H_DV), BF16),
        scratch_shapes=[pltpu.VMEM((seq, w), F32), pltpu.VMEM((seq, w), F32)],
        compiler_params=_cparams(("parallel", "parallel"), 44),
        name="hgrn",
    )(proj, proj, proj, proj, proj, meta_proj, meta_proj, lb_fwd, lb_bwd, norm_g)


def _merge_kernel(ao_ref, ho_ref, ga_ref, gh_ref, xp_ref, xs_ref, wa_ref, wh_ref, wo_ref, lng_ref, lnb_ref,
                  l1g_ref, l1b_ref, hf_ref, hp_ref, *, tp):
    pa = jnp.dot(ao_ref[...], wa_ref[...], preferred_element_type=F32)
    ph = jnp.dot(ho_ref[...], wh_ref[...], preferred_element_type=F32)
    merged = (jax.nn.sigmoid(ga_ref[...].astype(F32)) * pa + jax.nn.sigmoid(gh_ref[...].astype(F32)) * ph)
    t = jnp.dot(merged.astype(BF16), wo_ref[...], preferred_element_type=F32)

    def finish(x_ref):
        x_in = _layer_norm_f32(x_ref[...], lng_ref[...], lnb_ref[...])
        h = _layer_norm_f32(DN_ALPHA * x_in + t, l1g_ref[...], l1b_ref[...])
        hf_ref[...] = h
        hp_ref[...] = _pack_bf16_pair(h)

    @pl.when(pl.program_id(0) < tp)
    def _():
        finish(xp_ref)

    @pl.when(pl.program_id(0) >= tp)
    def _():
        finish(xs_ref)


def _merge(ao, ho, proj, xp, xs, wa, wh, wo, ln_g, ln_b, l1g, l1b, tm):
    d = xp.shape[1]
    tp, ts = xp.shape[0] // tm, xs.shape[0] // tm
    n_tok = (tp + ts) * tm
    gate_blk0 = (proj.shape[1] - 2 * d) // d
    vec = pl.BlockSpec((1, d), lambda i: (0, 0))
    xp_spec, xs_spec = _two_group_specs(tm, d, tp, False)
    return pl.pallas_call(
        functools.partial(_merge_kernel, tp=tp),
        grid=(tp + ts,),
        in_specs=[
            pl.BlockSpec((tm, ao.shape[1]), lambda i: (i, 0)),
            pl.BlockSpec((tm, ho.shape[1]), lambda i: (i, 0)),
            pl.BlockSpec((tm, d), lambda i: (i, gate_blk0)),
            pl.BlockSpec((tm, d), lambda i: (i, gate_blk0 + 1)),
            xp_spec, xs_spec,
            _resident(wa.shape), _resident(wh.shape), _resident(wo.shape),
            vec, vec, vec, vec,
        ],
        out_specs=[pl.BlockSpec((tm, d), lambda i: (i, 0)), pl.BlockSpec((tm, d // 2), lambda i: (i, 0))],
        out_shape=[jax.ShapeDtypeStruct((n_tok, d), F32), jax.ShapeDtypeStruct((n_tok, d // 2), U32)],
        compiler_params=_cparams(("parallel",), 56),
        name="merge",
    )(ao, ho, proj, proj, xp, xs, wa, wh, wo, ln_g, ln_b, l1g, l1b)


def _first_argmax(x, iota, size):
    m = jnp.max(x, axis=0, keepdims=True)
    idx = jnp.min(jnp.where(x == m, iota, size), axis=0, keepdims=True)
    return m, idx


def _router_kernel(h_ref, wrt_ref, bias_ref, idx_ref, gate_ref, rank_ref, cnt_ref, base_ref):
    tm = h_ref.shape[0]
    half = h_ref.shape[1]

    @pl.when(pl.program_id(0) == 0)
    def _():
        base_ref[...] = jnp.zeros_like(base_ref)

    nt = (((1,), (1,)), ((), ()))
    h_lo, h_hi = _unpack_bf16_pair(h_ref[...])
    logits = (lax.dot_general(wrt_ref[:, 0:half], h_lo, nt, preferred_element_type=F32)
              + lax.dot_general(wrt_ref[:, half:], h_hi, nt, preferred_element_type=F32))
    scores = jax.nn.sigmoid(logits)
    biased = scores + bias_ref[...]
    neg_inf = jnp.asarray(-jnp.inf, F32)

    iota_g = lax.broadcasted_iota(jnp.int32, (GROUP_SIZE, tm), 0)
    iota_n = lax.broadcasted_iota(jnp.int32, (N_GROUPS, tm), 0)
    grp = jnp.full((N_GROUPS, tm), neg_inf, F32)
    for g in range(N_GROUPS):
        xg = biased[g * GROUP_SIZE:(g + 1) * GROUP_SIZE, :]
        m1, i1 = _first_argmax(xg, iota_g, GROUP_SIZE)
        m2 = jnp.max(jnp.where(iota_g == i1, neg_inf, xg), axis=0, keepdims=True)
        grp = jnp.where(iota_n == g, m1 + m2, grp)

    keep_f = jnp.zeros((N_GROUPS, tm), F32)
    for _ in range(TOPK_GROUPS):
        _, ig = _first_argmax(grp, iota_n, N_GROUPS)
        sel = iota_n == ig
        keep_f = jnp.where(sel, 1.0, keep_f)
        grp = jnp.where(sel, neg_inf, grp)

    masked = jnp.concatenate(
        [jnp.where(keep_f[g:g + 1, :] > 0.5, biased[g * GROUP_SIZE:(g + 1) * GROUP_SIZE, :], neg_inf)
         for g in range(N_GROUPS)], axis=0)

    iota_e = lax.broadcasted_iota(jnp.int32, (N_EXPERTS, tm), 0)
    chosen = jnp.zeros((N_EXPERTS, tm), F32)
    idxs, gates = [], []
    for _ in range(TOP_K):
        _, ie = _first_argmax(masked, iota_e, N_EXPERTS)
        sel = iota_e == ie
        gates.append(jnp.sum(jnp.where(sel, scores, 0.0), axis=0, keepdims=True))
        idxs.append(ie)
        chosen = jnp.where(sel, 1.0, chosen)
        masked = jnp.where(sel, neg_inf, masked)

    denom = gates[0]
    for gk in gates[1:]:
        denom = denom + gk

    r = lax.broadcasted_iota(jnp.int32, (tm, tm), 0)
    c = lax.broadcasted_iota(jnp.int32, (tm, tm), 1)
    before = jnp.where(r < c, 1.0, 0.0).astype(BF16)
    pos = jnp.dot(chosen.astype(BF16), before, preferred_element_type=F32) + base_ref[...]
    for k in range(TOP_K):
        sel = iota_e == idxs[k]
        rank_ref[k:k + 1, :] = jnp.sum(jnp.where(sel, pos, 0.0), axis=0, keepdims=True).astype(jnp.int32)
        idx_ref[k:k + 1, :] = idxs[k]
        gate_ref[k:k + 1, :] = gates[k] / denom * ROUTED_SCALE

    base_ref[...] = base_ref[...] + jnp.sum(chosen, axis=1, keepdims=True)
    cnt_ref[...] = base_ref[...]


def _router(h_pk, w_router_t, bias_col, tm):
    n_tok, half = h_pk.shape
    row_blk = pl.BlockSpec((TOP_K, tm), lambda i: (0, i))
    return pl.pallas_call(
        _router_kernel,
        grid=(n_tok // tm,),
        in_specs=[
            pl.BlockSpec((tm, half), lambda i: (i, 0)),
            _resident(w_router_t.shape),
            pl.BlockSpec((N_EXPERTS, 1), lambda i: (0, 0)),
        ],
        out_specs=[row_blk, row_blk, row_blk, pl.BlockSpec((N_EXPERTS, 1), lambda i: (0, 0))],
        out_shape=[
            jax.ShapeDtypeStruct((TOP_K, n_tok), jnp.int32),
            jax.ShapeDtypeStruct((TOP_K, n_tok), F32),
            jax.ShapeDtypeStruct((TOP_K, n_tok), jnp.int32),
            jax.ShapeDtypeStruct((N_EXPERTS, 1), F32),
        ],
        scratch_shapes=[pltpu.VMEM((N_EXPERTS, 1), F32)],
        compiler_params=_cparams(("arbitrary",), 32),
        name="router",
    )(h_pk, w_router_t, bias_col)


def _dest_kernel(idx_ref, rank_ref, start_ref, dest_ref):
    tm = idx_ref.shape[1]
    iota_e = lax.broadcasted_iota(jnp.int32, (N_EXPERTS, tm), 0)
    start = start_ref[...]
    for k in range(TOP_K):
        sel = iota_e == idx_ref[k:k + 1, :]
        base = jnp.sum(jnp.where(sel, start, 0.0), axis=0, keepdims=True)
        dest_ref[k:k + 1, :] = base.astype(jnp.int32) + rank_ref[k:k + 1, :]


def _dest(idx, rank, start_col, tm):
    n_tok = idx.shape[1]
    row_blk = pl.BlockSpec((TOP_K, tm), lambda i: (0, i))
    return pl.pallas_call(
        _dest_kernel,
        grid=(n_tok // tm,),
        in_specs=[row_blk, row_blk, pl.BlockSpec((N_EXPERTS, 1), lambda i: (0, 0))],
        out_specs=row_blk,
        out_shape=jax.ShapeDtypeStruct((TOP_K, n_tok), jnp.int32),
        compiler_params=_cparams(("parallel",), 32),
        name="dest",
    )(idx, rank, start_col)


def _dispatch_kernel(dest_ref, h_ref, xs_ref, sem):
    tm = h_ref.shape[0]

    def row_copy(r, k):
        return pltpu.make_async_copy(h_ref.at[pl.ds(r, 1)], xs_ref.at[pl.ds(dest_ref[0, k * tm + r], 1)], sem)

    def start(r, carry):
        for k in range(TOP_K):
            row_copy(r, k).start()
        return carry

    def wait(r, carry):
        for k in range(TOP_K):
            row_copy(r, k).wait()
        return carry

    lax.fori_loop(0, tm, start, 0)
    lax.fori_loop(0, tm, wait, 0)


def _dispatch(h_pk, dest_tiles, tm):
    n_tok, half = h_pk.shape
    return pl.pallas_call(
        _dispatch_kernel,
        grid=(n_tok // tm,),
        in_specs=[
            pl.BlockSpec((None, 1, TOP_K * tm), lambda i: (i, 0, 0), memory_space=pltpu.SMEM),
            pl.BlockSpec((tm, half), lambda i: (i, 0)),
        ],
        out_specs=pl.BlockSpec(memory_space=pl.ANY),
        out_shape=jax.ShapeDtypeStruct((n_tok * TOP_K, half), U32),
        scratch_shapes=[pltpu.SemaphoreType.DMA],
        compiler_params=_cparams(("arbitrary",), 32, disable_bounds_checks=True),
        name="dispatch",
    )(dest_tiles, h_pk)


def _experts_kernel(blk_ref, e_ref, lo_ref, hi_ref, first_ref, newe_ref, slot_ref, nexte_ref,
                    x_ref, wg_hbm, wu_hbm, wd_hbm, y_ref, wg_buf, wu_buf, wd_buf, wgu_bf, wd_bf, sems):
    i = pl.program_id(0)
    ff = wg_buf.shape[2]
    lo = lo_ref[i]
    hi = hi_ref[i]

    def weight_copies(e, slot):
        return (pltpu.make_async_copy(wg_hbm.at[0, e], wg_buf.at[slot], sems.at[slot, 0]),
                pltpu.make_async_copy(wu_hbm.at[0, e], wu_buf.at[slot], sems.at[slot, 1]),
                pltpu.make_async_copy(wd_hbm.at[0, e], wd_buf.at[slot], sems.at[slot, 2]))

    @pl.when(newe_ref[i] == 1)
    def _():
        slot = slot_ref[i]
        e = e_ref[i]

        @pl.when(i == 0)
        def _():
            for c in weight_copies(e, slot):
                c.start()

        for c in weight_copies(e, slot):
            c.wait()
        nxt = nexte_ref[i]

        @pl.when(nxt >= 0)
        def _():
            for c in weight_copies(nxt, 1 - slot):
                c.start()

        wgu_bf[:, 0:ff] = wg_buf[slot].astype(BF16)
        wgu_bf[:, ff:] = wu_buf[slot].astype(BF16)
        wd_bf[...] = wd_buf[slot].astype(BF16)

    @pl.when(hi > lo)
    def _():
        gu = _dot_packed(x_ref[...], wgu_bf)
        hid = (jax.nn.silu(gu[:, 0:ff]) * gu[:, ff:]).astype(BF16)
        y = jnp.dot(hid, wd_bf[...], preferred_element_type=F32)
        row = lax.broadcasted_iota(jnp.int32, (y.shape[0], 1), 0)
        y = jnp.where(jnp.logical_and(row >= lo, row < hi), y, 0.0)

        @pl.when(first_ref[i] == 1)
        def _():
            y_ref[...] = y

        @pl.when(first_ref[i] == 0)
        def _():
            y_ref[...] += y


def _experts(x_sorted, items, w_gate, w_up, w_down):
    n_rows, half = x_sorted.shape
    d = 2 * half
    ff = w_gate.shape[-1]
    n_items = items[0].shape[0]
    hbm = pl.BlockSpec(memory_space=pl.ANY)
    grid_spec = pltpu.PrefetchScalarGridSpec(
        num_scalar_prefetch=len(items),
        grid=(n_items,),
        in_specs=[pl.BlockSpec((EXPERT_ROWS, half), lambda i, blk, *_: (blk[i], 0)), hbm, hbm, hbm],
        out_specs=pl.BlockSpec((EXPERT_ROWS, d), lambda i, blk, *_: (blk[i], 0)),
        scratch_shapes=[pltpu.VMEM((2, d, ff), F32), pltpu.VMEM((2, d, ff), F32), pltpu.VMEM((2, ff, d), F32),
                        pltpu.VMEM((d, 2 * ff), BF16), pltpu.VMEM((ff, d), BF16),
                        pltpu.SemaphoreType.DMA((2, 3))],
    )
    return pl.pallas_call(
        _experts_kernel,
        grid_spec=grid_spec,
        out_shape=jax.ShapeDtypeStruct((n_rows, d), F32),
        compiler_params=_cparams(("arbitrary",), 52),
        name="experts",
    )(*items, x_sorted, w_gate, w_up, w_down)


def _final_kernel(dest_ref, gate_ref, hf_ref, hp_ref, y_ref, wgu_ref, wd_ref, g_ref, b_ref, op_ref, os_ref,
                  ybuf, sem, *, tp):
    tm = hf_ref.shape[0]
    ff = wd_ref.shape[0]

    def row_copy(r, k):
        return pltpu.make_async_copy(y_ref.at[pl.ds(dest_ref[0, k * tm + r], 1)], ybuf.at[k, pl.ds(r, 1)], sem)

    def start(r, carry):
        for k in range(TOP_K):
            row_copy(r, k).start()
        return carry

    def wait(r, carry):
        for k in range(TOP_K):
            row_copy(r, k).wait()
        return carry

    lax.fori_loop(0, tm, start, 0)

    gu = _dot_packed(hp_ref[...], wgu_ref)
    hid = (jax.nn.silu(gu[:, 0:ff]) * gu[:, ff:]).astype(BF16)
    acc = DN_ALPHA * hf_ref[...] + jnp.dot(hid, wd_ref[...], preferred_element_type=F32)

    lax.fori_loop(0, tm, wait, 0)
    gate = gate_ref[...]
    for k in range(TOP_K):
        acc = acc + ybuf[k] * gate[:, k:k + 1]
    out = _layer_norm_f32(acc, g_ref[...], b_ref[...])

    @pl.when(pl.program_id(0) < tp)
    def _():
        op_ref[...] = out

    @pl.when(pl.program_id(0) >= tp)
    def _():
        os_ref[...] = out


def _final(h_f32, h_pk, y_sorted, dest_tiles, gate_t, w_gu, w_d, g, b, n_prompt, tm):
    n_tok, d = h_f32.shape
    tp = n_prompt // tm
    ts = n_tok // tm - tp
    vec = pl.BlockSpec((1, d), lambda i: (0, 0))
    op_spec, os_spec = _two_group_specs(tm, d, tp, False)
    return pl.pallas_call(
        functools.partial(_final_kernel, tp=tp),
        grid=(tp + ts,),
        in_specs=[
            pl.BlockSpec((None, 1, TOP_K * tm), lambda i: (i, 0, 0), memory_space=pltpu.SMEM),
            pl.BlockSpec((tm, TOP_K), lambda i: (i, 0)),
            pl.BlockSpec((tm, d), lambda i: (i, 0)),
            pl.BlockSpec((tm, d // 2), lambda i: (i, 0)),
            pl.BlockSpec(memory_space=pl.ANY),
            _resident(w_gu.shape), _resident(w_d.shape), vec, vec,
        ],
        out_specs=[op_spec, os_spec],
        out_shape=[jax.ShapeDtypeStruct((tp * tm, d), F32), jax.ShapeDtypeStruct((ts * tm, d), F32)],
        scratch_shapes=[pltpu.VMEM((TOP_K, tm, d), F32), pltpu.SemaphoreType.DMA],
        compiler_params=_cparams(("arbitrary",), 48, disable_bounds_checks=True),
        name="final",
    )(dest_tiles, gate_t, h_f32, h_pk, y_sorted, w_gu, w_d, g, b)


def _lower_bounds(gamma):
    return jnp.cumsum(jax.nn.softmax(gamma.astype(F32), axis=0), axis=0)


def _tile_major(dest, tm):
    k, n = dest.shape
    return jnp.transpose(dest.reshape(k, n // tm, tm), (1, 0, 2)).reshape(n // tm, 1, k * tm)


def _work_items(counts, n_assign):
    rows = EXPERT_ROWS
    n_items = n_assign // rows + N_EXPERTS - 1
    end = jnp.cumsum(counts)
    start = end - counts
    first_blk = start // rows
    last_blk = jnp.maximum(end - 1, 0) // rows
    per_e = jnp.where(counts > 0, last_blk - first_blk + 1, 0)
    it_end = jnp.cumsum(per_e)
    it_start = it_end - per_e
    total = it_end[-1]
    i = jnp.arange(n_items, dtype=jnp.int32)
    ic = jnp.minimum(i, total - 1)
    e = jnp.sum((it_end[None, :] <= ic[:, None]).astype(jnp.int32), axis=1)
    blk = first_blk[e] + (ic - it_start[e])
    lo = jnp.maximum(start[e], blk * rows) - blk * rows
    hi = jnp.minimum(end[e], (blk + 1) * rows) - blk * rows
    valid = i < total
    hi = jnp.where(valid, hi, lo)
    prev_blk = jnp.concatenate([jnp.full((1,), -1, jnp.int32), blk[:-1]])
    first = jnp.where(jnp.logical_and(valid, blk != prev_blk), 1, 0)
    prev_e = jnp.concatenate([jnp.full((1,), -1, jnp.int32), e[:-1].astype(jnp.int32)])
    new_e = jnp.where(jnp.logical_and(valid, e != prev_e), 1, 0)
    nonempty = counts > 0
    slot = (jnp.cumsum(nonempty.astype(jnp.int32)) - 1) % 2
    ids = jnp.arange(N_EXPERTS, dtype=jnp.int32)
    later = jnp.concatenate([jnp.where(nonempty, ids, N_EXPERTS)[1:], jnp.full((1,), N_EXPERTS, jnp.int32)])
    nxt = jnp.flip(lax.cummin(jnp.flip(later)))
    nxt = jnp.where(nxt < N_EXPERTS, nxt, -1)
    as_i32 = lambda v: v.astype(jnp.int32)
    return start, (as_i32(blk), as_i32(e), as_i32(lo), as_i32(hi), as_i32(first), as_i32(new_e),
                   as_i32(slot[e]), as_i32(nxt[e]))


def kernel(x_prompt, x_sample, meta_tokens, ln_in_g, ln_in_b, hg_gamma_fwd, hg_gamma_bwd, w_in, lambda_q1, lambda_k1, lambda_q2, lambda_k2, attn_subln_g, hg_norm_g, w_branch_attn, w_branch_hgrn, w_out, ln1_g, ln1_b, w_router, router_bias, w_sh_gate, w_sh_up, w_sh_down, w_ex_gate, w_ex_up, w_ex_down, ln2_g, ln2_b):
    bp, seq, d = x_prompt.shape
    bs, seq_s, _ = x_sample.shape
    assert seq == seq_s and w_in.shape[0] == DEPTH
    n_seq = bp + bs
    n_tok = n_seq * seq
    layer = 0

    xp = x_prompt.reshape(bp * seq, d)
    xs = x_sample.reshape(bs * seq, d)
    row = lambda v: v.reshape(1, -1).astype(F32)
    ln_g, ln_b = row(ln_in_g), row(ln_in_b)
    w_in_bf = w_in[layer].astype(BF16)

    proj = _ln_matmul(xp, xs, ln_g, ln_b, w_in_bf, tm=min(1024, bs * seq), tn=1024)
    meta = meta_tokens.astype(F32)
    meta_proj = _ln_matmul(meta, meta, ln_g, ln_b, w_in_bf, tm=N_META, tn=1024)[:N_META]
    meta_proj = jnp.pad(meta_proj, ((META_PAD - N_META, 0), (0, 0)))

    lam = (jnp.exp(jnp.sum(lambda_q1[layer].astype(F32) * lambda_k1[layer].astype(F32)))
           - jnp.exp(jnp.sum(lambda_q2[layer].astype(F32) * lambda_k2[layer].astype(F32))) + LAMBDA_INIT)
    slopes = 2.0 ** (-8.0 * jnp.arange(1, A_HEADS + 1, dtype=F32) / A_HEADS)
    attn_params = jnp.concatenate([lam.reshape(1), slopes]).astype(F32)
    subln_g = row(attn_subln_g[layer]) * (1.0 - LAMBDA_INIT)
    ao = _diff_attn(proj, meta_proj, attn_params, subln_g, n_seq, seq, tq=min(256, seq))

    lb_fwd = row(_lower_bounds(hg_gamma_fwd)[layer])
    lb_bwd = row(_lower_bounds(hg_gamma_bwd)[layer])
    ho = _hgrn(proj, meta_proj, lb_fwd, lb_bwd, row(hg_norm_g[layer]), n_seq, seq)

    h_f32, h_pk = _merge(ao, ho, proj, xp, xs, w_branch_attn[layer].astype(BF16),
                         w_branch_hgrn[layer].astype(BF16), w_out[layer].astype(BF16), ln_g, ln_b,
                         row(ln1_g[layer]), row(ln1_b[layer]), tm=min(256, bs * seq))

    idx, gate, rank, counts = _router(h_pk, w_router[layer].T.astype(BF16),
                                      router_bias[layer].astype(F32).reshape(N_EXPERTS, 1), tm=min(512, n_tok))
    start, items = _work_items(counts[:, 0].astype(jnp.int32), n_tok * TOP_K)
    dest = _dest(idx, rank, start.astype(F32).reshape(N_EXPERTS, 1), tm=min(2048, n_tok))

    tm_d = min(256, n_tok)
    x_sorted = _dispatch(h_pk, _tile_major(dest, tm_d), tm_d)
    y_sorted = _experts(x_sorted, items, w_ex_gate, w_ex_up, w_ex_down)

    w_sh_gu = jnp.concatenate([w_sh_gate[layer], w_sh_up[layer]], axis=1).astype(BF16)
    tm_f = min(128, bs * seq)
    y_prompt, y_sample = _final(h_f32, h_pk, y_sorted, _tile_major(dest, tm_f), gate.T, w_sh_gu,
                                w_sh_down[layer].astype(BF16), row(ln2_g[layer]), row(ln2_b[layer]),
                                bp * seq, tm_f)
    return (y_prompt.reshape(bp, seq, d), y_sample.reshape(bs, seq, d))
```

```python
import functools
import math

import jax
import jax.numpy as jnp
from jax import lax
from jax.experimental import pallas as pl
from jax.experimental.pallas import tpu as pltpu

F32 = jnp.float32
BF16 = jnp.bfloat16
U32 = jnp.uint32

N_META = 16
A_HEADS = 8
A_HEAD_DIM = 64
H_HEADS = 8
H_EXPAND = 128
H_DV = 128
H_CHUNK = 64
N_EXPERTS = 256
TOP_K = 8
N_GROUPS = 8
TOPK_GROUPS = 4
GROUP_SIZE = N_EXPERTS // N_GROUPS
ROUTED_SCALE = 2.5
DEPTH = 1
DN_ALPHA = (2 * DEPTH) ** 0.25
EPS = 1e-5
LAMBDA_INIT = 0.8 - 0.6 * math.exp(-0.3 * 0)

LANES = 128
HGRN_HEADS_PER_STEP = 4
META_PAD = 128
NEG_BIG = -1e30
EXPERT_ROWS = 256
WEIGHT_DMA_CHUNKS = 4
HIGH_HALF = 0xFFFF0000


def _cparams(semantics, vmem_mb, **kw):
    return pltpu.CompilerParams(dimension_semantics=semantics, vmem_limit_bytes=vmem_mb * 1024 * 1024, **kw)


def _layer_norm_f32(x, g, b):
    mu = jnp.mean(x, axis=-1, keepdims=True)
    xc = x - mu
    var = jnp.mean(xc * xc, axis=-1, keepdims=True)
    return xc * lax.rsqrt(var + EPS) * g + b


def _resident(shape):
    nd = len(shape)
    return pl.BlockSpec(shape, lambda *_: (0,) * nd, pipeline_mode=pl.Buffered(1))


def _pack_bf16_pair(x):
    k = x.shape[1] // 2
    lo = lax.bitcast_convert_type(x[:, :k].astype(BF16).astype(F32), U32) >> 16
    hi = lax.bitcast_convert_type(x[:, k:].astype(BF16).astype(F32), U32) & jnp.uint32(HIGH_HALF)
    return lo | hi


def _unpack_bf16_pair(p):
    lo = lax.bitcast_convert_type(p << 16, F32).astype(BF16)
    hi = lax.bitcast_convert_type(p & jnp.uint32(HIGH_HALF), F32).astype(BF16)
    return lo, hi


def _dot_packed(p, w_ref):
    k = p.shape[1]
    lo, hi = _unpack_bf16_pair(p)
    return (jnp.dot(lo, w_ref[0:k, :], preferred_element_type=F32)
            + jnp.dot(hi, w_ref[k:, :], preferred_element_type=F32))


def _two_group_specs(tm, d, tp, extra_grid_axes):
    if extra_grid_axes:
        return (pl.BlockSpec((tm, d), lambda i, j: (jnp.minimum(i, tp - 1), 0), pipeline_mode=pl.Buffered(1)),
                pl.BlockSpec((tm, d), lambda i, j: (jnp.maximum(i - tp, 0), 0), pipeline_mode=pl.Buffered(1)))
    return (pl.BlockSpec((tm, d), lambda i: (jnp.minimum(i, tp - 1), 0)),
            pl.BlockSpec((tm, d), lambda i: (jnp.maximum(i - tp, 0), 0)))


def _ln_matmul_kernel(xp_ref, xs_ref, g_ref, b_ref, w_ref, o_ref, xn_ref, *, tp):
    i = pl.program_id(0)
    first_col = pl.program_id(1) == 0

    @pl.when(jnp.logical_and(first_col, i < tp))
    def _():
        xn_ref[...] = _layer_norm_f32(xp_ref[...], g_ref[...], b_ref[...]).astype(BF16)

    @pl.when(jnp.logical_and(first_col, i >= tp))
    def _():
        xn_ref[...] = _layer_norm_f32(xs_ref[...], g_ref[...], b_ref[...]).astype(BF16)

    o_ref[...] = jnp.dot(xn_ref[...], w_ref[...], preferred_element_type=F32).astype(o_ref.dtype)


def _ln_matmul(xp, xs, g, b, w, tm, tn):
    d = xp.shape[1]
    n = w.shape[1]
    tp, ts = xp.shape[0] // tm, xs.shape[0] // tm
    xp_spec, xs_spec = _two_group_specs(tm, d, tp, True)
    return pl.pallas_call(
        functools.partial(_ln_matmul_kernel, tp=tp),
        grid=(tp + ts, n // tn),
        in_specs=[
            xp_spec, xs_spec,
            pl.BlockSpec((1, d), lambda i, j: (0, 0)),
            pl.BlockSpec((1, d), lambda i, j: (0, 0)),
            pl.BlockSpec((d, tn), lambda i, j: (0, j)),
        ],
        out_specs=pl.BlockSpec((tm, tn), lambda i, j: (i, j)),
        out_shape=jax.ShapeDtypeStruct(((tp + ts) * tm, n), BF16),
        scratch_shapes=[pltpu.VMEM((tm, d), BF16)],
        compiler_params=_cparams(("parallel", "arbitrary"), 44),
        name="ln_matmul",
    )(xp, xs, g, b, w)


def _diff_attn_kernel(par_ref, q_ref, k_ref, v_ref, km_ref, vm_ref, g_ref, o_ref, kall_ref, vall_ref, bias_ref):
    h = pl.program_id(0)
    b = pl.program_id(1)
    qi = pl.program_id(2)
    tq = q_ref.shape[0]
    n_keys = kall_ref.shape[0]
    shift = bias_ref.shape[1] - n_keys

    @pl.when(jnp.logical_and(b == 0, qi == 0))
    def _():
        i = lax.broadcasted_iota(jnp.int32, (tq, 1), 0)
        j = lax.broadcasted_iota(jnp.int32, (1, bias_ref.shape[1]), 1)
        bias_ref[...] = -par_ref[1 + h] * jnp.abs(i + (META_PAD + shift) - j).astype(F32)

    @pl.when(qi == 0)
    def _():
        kall_ref[0:META_PAD, :] = km_ref[...]
        kall_ref[META_PAD:, :] = k_ref[...]
        vall_ref[0:META_PAD, :] = vm_ref[...]
        vall_ref[META_PAD:, :] = v_ref[...]

    lam = par_ref[0]
    q = q_ref[...]
    lane = lax.broadcasted_iota(jnp.int32, (1, 2 * A_HEAD_DIM), 1)
    scale = jnp.asarray(A_HEAD_DIM ** -0.5, BF16)
    zero = jnp.zeros((), BF16)
    q1 = jnp.where(lane < A_HEAD_DIM, q, zero) * scale
    q2 = jnp.where(lane >= A_HEAD_DIM, q, zero) * scale
    k_all = kall_ref[...]
    nt = (((1,), (1,)), ((), ()))
    s1 = lax.dot_general(q1, k_all, nt, preferred_element_type=F32)
    s2 = lax.dot_general(q2, k_all, nt, preferred_element_type=F32)

    off = pl.multiple_of(shift - qi * tq, LANES)
    lane_m = lax.broadcasted_iota(jnp.int32, (1, META_PAD), 1)
    pad_mask = jnp.where(lane_m < META_PAD - N_META, NEG_BIG, 0.0)
    bias_meta = bias_ref[:, pl.ds(off, META_PAD)] + pad_mask
    bias_seq = bias_ref[:, pl.ds(off + META_PAD, n_keys - META_PAD)]

    def softmax_parts(s):
        s_meta = s[:, 0:META_PAD] + bias_meta
        s_seq = s[:, META_PAD:] + bias_seq
        m = jnp.maximum(jnp.max(s_meta, axis=-1, keepdims=True), jnp.max(s_seq, axis=-1, keepdims=True))
        p_meta = jnp.exp(s_meta - m)
        p_seq = jnp.exp(s_seq - m)
        l = jnp.sum(p_meta, axis=-1, keepdims=True) + jnp.sum(p_seq, axis=-1, keepdims=True)
        return p_meta.astype(BF16), p_seq.astype(BF16), l

    p1_meta, p1_seq, l1 = softmax_parts(s1)
    p2_meta, p2_seq, l2 = softmax_parts(s2)
    a1 = (1.0 / l1).astype(BF16)
    a2 = (lam / l2).astype(BF16)
    w_meta = p1_meta * a1 - p2_meta * a2
    w_seq = p1_seq * a1 - p2_seq * a2
    o = (jnp.dot(w_meta, vall_ref[0:META_PAD, :], preferred_element_type=F32)
         + jnp.dot(w_seq, vall_ref[META_PAD:, :], preferred_element_type=F32))
    ms = jnp.mean(o * o, axis=-1, keepdims=True)
    o_ref[...] = (o * lax.rsqrt(ms + EPS) * g_ref[...]).astype(o_ref.dtype)


def _diff_attn(proj, meta_proj, params, subln_g, n_seq, seq, tq):
    n_tok = proj.shape[0]
    nq = seq // tq
    hw = 2 * A_HEAD_DIM
    return pl.pallas_call(
        _diff_attn_kernel,
        grid=(A_HEADS, n_seq, nq),
        in_specs=[
            pl.BlockSpec(memory_space=pltpu.SMEM),
            pl.BlockSpec((tq, hw), lambda h, b, i: (b * nq + i, h)),
            pl.BlockSpec((seq, hw), lambda h, b, i: (b, A_HEADS + h)),
            pl.BlockSpec((seq, hw), lambda h, b, i: (b, 2 * A_HEADS + h)),
            pl.BlockSpec((META_PAD, hw), lambda h, b, i: (0, A_HEADS + h)),
            pl.BlockSpec((META_PAD, hw), lambda h, b, i: (0, 2 * A_HEADS + h)),
            pl.BlockSpec((1, hw), lambda h, b, i: (0, 0)),
        ],
        out_specs=pl.BlockSpec((tq, hw), lambda h, b, i: (b * nq + i, h)),
        out_shape=jax.ShapeDtypeStruct((n_tok, A_HEADS * hw), BF16),
        scratch_shapes=[pltpu.VMEM((META_PAD + seq, hw), BF16), pltpu.VMEM((META_PAD + seq, hw), BF16),
                        pltpu.VMEM((tq, META_PAD + 2 * seq - tq), F32)],
        compiler_params=_cparams(("arbitrary", "arbitrary", "arbitrary"), 48),
        name="diff_attn",
    )(params, proj, proj, proj, meta_proj, meta_proj, subln_g)


def _split_cumsum(tri, x):
    hi = x.astype(BF16)
    lo = (x - hi.astype(F32)).astype(BF16)
    return (jnp.dot(tri, hi, preferred_element_type=F32) + jnp.dot(tri, lo, preferred_element_type=F32))


def _hgrn_chunk(q, k, logf, v_bf, state_t, reverse):
    c = q.shape[0]
    r = lax.broadcasted_iota(jnp.int32, (c, c), 0)
    s = lax.broadcasted_iota(jnp.int32, (c, c), 1)
    if reverse:
        visible = s >= r
        ref_row, last_row = c - 1 - c // 2, 0
    else:
        visible = s <= r
        ref_row, last_row = c // 2, c - 1
    tri = jnp.where(visible, 1.0, 0.0).astype(BF16)
    b = _split_cumsum(tri, logf)
    b_ref = b[ref_row:ref_row + 1, :]
    b_last = b[last_row:last_row + 1, :]
    nt = (((1,), (1,)), ((), ()))
    tn = (((0,), (0,)), ((), ()))
    qd = (q * jnp.exp(b - b_ref)).astype(BF16)
    kd = (k * jnp.exp(b_ref - b)).astype(BF16)
    scores = lax.dot_general(qd, kd, nt, preferred_element_type=F32)
    scores = jnp.where(visible, scores, 0.0).astype(BF16)
    o_intra = jnp.dot(scores, v_bf, preferred_element_type=F32)
    qe = (q * jnp.exp(b)).astype(BF16)
    o_inter = lax.dot_general(qe, state_t.astype(BF16), nt, preferred_element_type=F32)
    ks = (k * jnp.exp(b_last - b)).astype(BF16)
    d_state_t = lax.dot_general(v_bf, ks, tn, preferred_element_type=F32)
    new_state_t = state_t * jnp.exp(b_last) + d_state_t
    return o_intra + o_inter, new_state_t


def _forget_gate(logit, lb):
    f = lb + (1.0 - lb) * jax.nn.sigmoid(logit)
    return 1.0 - f, jnp.log(f)


def _hgrn_kernel(q_ref, ff_ref, fb_ref, v_ref, hg_ref, mff_ref, mv_ref, lbf_ref, lbb_ref, g_ref,
                 o_ref, of_ref, ob_ref):
    seq = q_ref.shape[0]
    n_heads = q_ref.shape[1] // H_DV
    n_chunks = seq // H_CHUNK
    head = lambda j: pl.ds(j * H_DV, H_DV)

    mrow = lax.broadcasted_iota(jnp.int32, (H_CHUNK, 1), 0)
    is_meta = mrow >= H_CHUNK - N_META
    zeros_q = jnp.zeros((H_CHUNK, H_EXPAND), F32)
    state0 = jnp.zeros((H_DV, H_EXPAND), F32)
    init_f = []
    for j in range(n_heads):
        mk, mlogf = _forget_gate(mff_ref[:, head(j)].astype(F32), lbf_ref[:, head(j)])
        mk = jnp.where(is_meta, mk, 0.0)
        mlogf = jnp.where(is_meta, mlogf, 0.0)
        init_f.append(_hgrn_chunk(zeros_q, mk, mlogf, mv_ref[:, head(j)], state0, reverse=False)[1])

    def body(i, carry):
        states_f, states_b = carry
        rows_f = pl.ds(pl.multiple_of(i * H_CHUNK, H_CHUNK), H_CHUNK)
        rows_b = pl.ds(pl.multiple_of((n_chunks - 1 - i) * H_CHUNK, H_CHUNK), H_CHUNK)
        new_f, new_b = [], []
        for j in range(n_heads):
            q_f = jax.nn.silu(q_ref[rows_f, head(j)].astype(F32))
            k_f, logf_f = _forget_gate(ff_ref[rows_f, head(j)].astype(F32), lbf_ref[:, head(j)])
            o_f, s_f = _hgrn_chunk(q_f, k_f, logf_f, v_ref[rows_f, head(j)], states_f[j], reverse=False)
            of_ref[rows_f, head(j)] = o_f
            new_f.append(s_f)

            q_b = jax.nn.silu(q_ref[rows_b, head(j)].astype(F32))
            k_b, logf_b = _forget_gate(fb_ref[rows_b, head(j)].astype(F32), lbb_ref[:, head(j)])
            o_b, s_b = _hgrn_chunk(q_b, k_b, logf_b, v_ref[rows_b, head(j)], states_b[j], reverse=True)
            ob_ref[rows_b, head(j)] = o_b
            new_b.append(s_b)
        return tuple(new_f), tuple(new_b)

    lax.fori_loop(0, n_chunks, body, (tuple(init_f), (state0,) * n_heads))

    for j in range(n_heads):
        o = of_ref[:, head(j)] + ob_ref[:, head(j)]
        ms = jnp.mean(o * o, axis=-1, keepdims=True)
        normed = o * lax.rsqrt(ms + EPS) * g_ref[...]
        o_ref[:, head(j)] = (normed * jax.nn.sigmoid(hg_ref[:, head(j)].astype(F32))).astype(o_ref.dtype)


def _hgrn(proj, meta_proj, lb_fwd, lb_bwd, norm_g, n_seq, seq):
    n_tok = proj.shape[0]
    hps = HGRN_HEADS_PER_STEP
    w = hps * H_DV
    col0 = 3 * A_HEADS * (2 * A_HEAD_DIM) // w
    per_sec = H_HEADS // hps
    meta_blk = META_PAD // H_CHUNK - 1

    def sec(k):
        return pl.BlockSpec((seq, w), lambda b, h, k=k: (b, col0 + k * per_sec + h))

    def meta_sec(k):
        return pl.BlockSpec((H_CHUNK, w), lambda b, h, k=k: (meta_blk, col0 + k * per_sec + h))

    return pl.pallas_call(
        _hgrn_kernel,
        grid=(n_seq, per_sec),
        in_specs=[
            sec(0), sec(1), sec(2), sec(3), sec(4),
            meta_sec(1), meta_sec(3),
            pl.BlockSpec((1, w), lambda b, h: (0, h)),
            pl.BlockSpec((1, w), lambda b, h: (0, h)),
            pl.BlockSpec((1, H_DV), lambda b, h: (0, 0)),
        ],
        out_specs=pl.BlockSpec((seq, w), lambda b, h: (b, h)),
        out_shape=jax.ShapeDtypeStruct((n_tok, H_HEADS * H_DV), BF16),
        scratch_shapes=[pltpu.VMEM((seq, w), F32), pltpu.VMEM((seq, w), F32)],
        compiler_params=_cparams(("parallel", "parallel"), 44),
        name="hgrn",
    )(proj, proj, proj, proj, proj, meta_proj, meta_proj, lb_fwd, lb_bwd, norm_g)


def _merge_kernel(ao_ref, ho_ref, ga_ref, gh_ref, xp_ref, xs_ref, wa_ref, wh_ref, wo_ref, lng_ref, lnb_ref,
                  l1g_ref, l1b_ref, hf_ref, hp_ref, *, tp):
    pa = jnp.dot(ao_ref[...], wa_ref[...], preferred_element_type=F32)
    ph = jnp.dot(ho_ref[...], wh_ref[...], preferred_element_type=F32)
    merged = (jax.nn.sigmoid(ga_ref[...].astype(F32)) * pa + jax.nn.sigmoid(gh_ref[...].astype(F32)) * ph)
    t = jnp.dot(merged.astype(BF16), wo_ref[...], preferred_element_type=F32)

    def finish(x_ref):
        x_in = _layer_norm_f32(x_ref[...], lng_ref[...], lnb_ref[...])
        h = _layer_norm_f32(DN_ALPHA * x_in + t, l1g_ref[...], l1b_ref[...])
        hf_ref[...] = h
        hp_ref[...] = _pack_bf16_pair(h)

    @pl.when(pl.program_id(0) < tp)
    def _():
        finish(xp_ref)

    @pl.when(pl.program_id(0) >= tp)
    def _():
        finish(xs_ref)


def _merge(ao, ho, proj, xp, xs, wa, wh, wo, ln_g, ln_b, l1g, l1b, tm):
    d = xp.shape[1]
    tp, ts = xp.shape[0] // tm, xs.shape[0] // tm
    n_tok = (tp + ts) * tm
    gate_blk0 = (proj.shape[1] - 2 * d) // d
    vec = pl.BlockSpec((1, d), lambda i: (0, 0))
    xp_spec, xs_spec = _two_group_specs(tm, d, tp, False)
    return pl.pallas_call(
        functools.partial(_merge_kernel, tp=tp),
        grid=(tp + ts,),
        in_specs=[
            pl.BlockSpec((tm, ao.shape[1]), lambda i: (i, 0)),
            pl.BlockSpec((tm, ho.shape[1]), lambda i: (i, 0)),
            pl.BlockSpec((tm, d), lambda i: (i, gate_blk0)),
            pl.BlockSpec((tm, d), lambda i: (i, gate_blk0 + 1)),
            xp_spec, xs_spec,
            _resident(wa.shape), _resident(wh.shape), _resident(wo.shape),
            vec, vec, vec, vec,
        ],
        out_specs=[pl.BlockSpec((tm, d), lambda i: (i, 0)), pl.BlockSpec((tm, d // 2), lambda i: (i, 0))],
        out_shape=[jax.ShapeDtypeStruct((n_tok, d), F32), jax.ShapeDtypeStruct((n_tok, d // 2), U32)],
        compiler_params=_cparams(("parallel",), 56),
        name="merge",
    )(ao, ho, proj, proj, xp, xs, wa, wh, wo, ln_g, ln_b, l1g, l1b)


def _first_argmax(x, iota, size):
    m = jnp.max(x, axis=0, keepdims=True)
    idx = jnp.min(jnp.where(x == m, iota, size), axis=0, keepdims=True)
    return m, idx


def _router_kernel(h_ref, wrt_ref, bias_ref, idx_ref, gate_ref, rank_ref, cnt_ref, base_ref):
    tm = h_ref.shape[0]
    half = h_ref.shape[1]

    @pl.when(pl.program_id(0) == 0)
    def _():
        base_ref[...] = jnp.zeros_like(base_ref)

    nt = (((1,), (1,)), ((), ()))
    h_lo, h_hi = _unpack_bf16_pair(h_ref[...])
    logits = (lax.dot_general(wrt_ref[:, 0:half], h_lo, nt, preferred_element_type=F32)
              + lax.dot_general(wrt_ref[:, half:], h_hi, nt, preferred_element_type=F32))
    scores = jax.nn.sigmoid(logits)
    biased = scores + bias_ref[...]
    neg_inf = jnp.asarray(-jnp.inf, F32)

    iota_g = lax.broadcasted_iota(jnp.int32, (GROUP_SIZE, tm), 0)
    iota_n = lax.broadcasted_iota(jnp.int32, (N_GROUPS, tm), 0)
    grp = jnp.full((N_GROUPS, tm), neg_inf, F32)
    for g in range(N_GROUPS):
        xg = biased[g * GROUP_SIZE:(g + 1) * GROUP_SIZE, :]
        m1, i1 = _first_argmax(xg, iota_g, GROUP_SIZE)
        m2 = jnp.max(jnp.where(iota_g == i1, neg_inf, xg), axis=0, keepdims=True)
        grp = jnp.where(iota_n == g, m1 + m2, grp)

    keep_f = jnp.zeros((N_GROUPS, tm), F32)
    for _ in range(TOPK_GROUPS):
        _, ig = _first_argmax(grp, iota_n, N_GROUPS)
        sel = iota_n == ig
        keep_f = jnp.where(sel, 1.0, keep_f)
        grp = jnp.where(sel, neg_inf, grp)

    masked = jnp.concatenate(
        [jnp.where(keep_f[g:g + 1, :] > 0.5, biased[g * GROUP_SIZE:(g + 1) * GROUP_SIZE, :], neg_inf)
         for g in range(N_GROUPS)], axis=0)

    iota_e = lax.broadcasted_iota(jnp.int32, (N_EXPERTS, tm), 0)
    chosen = jnp.zeros((N_EXPERTS, tm), F32)
    idxs, gates = [], []
    for _ in range(TOP_K):
        _, ie = _first_argmax(masked, iota_e, N_EXPERTS)
        sel = iota_e == ie
        gates.append(jnp.sum(jnp.where(sel, scores, 0.0), axis=0, keepdims=True))
        idxs.append(ie)
        chosen = jnp.where(sel, 1.0, chosen)
        masked = jnp.where(sel, neg_inf, masked)

    denom = gates[0]
    for gk in gates[1:]:
        denom = denom + gk

    r = lax.broadcasted_iota(jnp.int32, (tm, tm), 0)
    c = lax.broadcasted_iota(jnp.int32, (tm, tm), 1)
    before = jnp.where(r < c, 1.0, 0.0).astype(BF16)
    pos = jnp.dot(chosen.astype(BF16), before, preferred_element_type=F32) + base_ref[...]
    for k in range(TOP_K):
        sel = iota_e == idxs[k]
        rank_ref[k:k + 1, :] = jnp.sum(jnp.where(sel, pos, 0.0), axis=0, keepdims=True).astype(jnp.int32)
        idx_ref[k:k + 1, :] = idxs[k]
        gate_ref[k:k + 1, :] = gates[k] / denom * ROUTED_SCALE

    base_ref[...] = base_ref[...] + jnp.sum(chosen, axis=1, keepdims=True)
    cnt_ref[...] = base_ref[...]


def _router(h_pk, w_router_t, bias_col, tm):
    n_tok, half = h_pk.shape
    row_blk = pl.BlockSpec((TOP_K, tm), lambda i: (0, i))
    return pl.pallas_call(
        _router_kernel,
        grid=(n_tok // tm,),
        in_specs=[
            pl.BlockSpec((tm, half), lambda i: (i, 0)),
            _resident(w_router_t.shape),
            pl.BlockSpec((N_EXPERTS, 1), lambda i: (0, 0)),
        ],
        out_specs=[row_blk, row_blk, row_blk, pl.BlockSpec((N_EXPERTS, 1), lambda i: (0, 0))],
        out_shape=[
            jax.ShapeDtypeStruct((TOP_K, n_tok), jnp.int32),
            jax.ShapeDtypeStruct((TOP_K, n_tok), F32),
            jax.ShapeDtypeStruct((TOP_K, n_tok), jnp.int32),
            jax.ShapeDtypeStruct((N_EXPERTS, 1), F32),
        ],
        scratch_shapes=[pltpu.VMEM((N_EXPERTS, 1), F32)],
        compiler_params=_cparams(("arbitrary",), 32),
        name="router",
    )(h_pk, w_router_t, bias_col)


def _dest_kernel(idx_ref, rank_ref, start_ref, dest_ref):
    tm = idx_ref.shape[1]
    iota_e = lax.broadcasted_iota(jnp.int32, (N_EXPERTS, tm), 0)
    start = start_ref[...]
    for k in range(TOP_K):
        sel = iota_e == idx_ref[k:k + 1, :]
        base = jnp.sum(jnp.where(sel, start, 0.0), axis=0, keepdims=True)
        dest_ref[k:k + 1, :] = base.astype(jnp.int32) + rank_ref[k:k + 1, :]


def _dest(idx, rank, start_col, tm):
    n_tok = idx.shape[1]
    row_blk = pl.BlockSpec((TOP_K, tm), lambda i: (0, i))
    return pl.pallas_call(
        _dest_kernel,
        grid=(n_tok // tm,),
        in_specs=[row_blk, row_blk, pl.BlockSpec((N_EXPERTS, 1), lambda i: (0, 0))],
        out_specs=row_blk,
        out_shape=jax.ShapeDtypeStruct((TOP_K, n_tok), jnp.int32),
        compiler_params=_cparams(("parallel",), 32),
        name="dest",
    )(idx, rank, start_col)


def _dispatch_kernel(dest_ref, h_ref, xs_ref, sem):
    tm = h_ref.shape[0]

    def row_copy(r, k):
        return pltpu.make_async_copy(h_ref.at[pl.ds(r, 1)], xs_ref.at[pl.ds(dest_ref[0, k * tm + r], 1)], sem)

    def start(r, carry):
        for k in range(TOP_K):
            row_copy(r, k).start()
        return carry

    def wait(r, carry):
        for k in range(TOP_K):
            row_copy(r, k).wait()
        return carry

    lax.fori_loop(0, tm, start, 0)
    lax.fori_loop(0, tm, wait, 0)


def _dispatch(h_pk, dest_tiles, tm):
    n_tok, half = h_pk.shape
    return pl.pallas_call(
        _dispatch_kernel,
        grid=(n_tok // tm,),
        in_specs=[
            pl.BlockSpec((None, 1, TOP_K * tm), lambda i: (i, 0, 0), memory_space=pltpu.SMEM),
            pl.BlockSpec((tm, half), lambda i: (i, 0)),
        ],
        out_specs=pl.BlockSpec(memory_space=pl.ANY),
        out_shape=jax.ShapeDtypeStruct((n_tok * TOP_K, half), U32),
        scratch_shapes=[pltpu.SemaphoreType.DMA],
        compiler_params=_cparams(("arbitrary",), 32, disable_bounds_checks=True),
        name="dispatch",
    )(dest_tiles, h_pk)


def _experts_kernel(blk_ref, e_ref, lo_ref, hi_ref, first_ref, last_ref, newe_ref, slot_ref, nexte_ref,
                    x_ref, wg_hbm, wu_hbm, wd_hbm, y_ref, wg_buf, wu_buf, wd_buf, wgu_bf, wd_bf, acc_ref, sems):
    i = pl.program_id(0)
    ff = wg_buf.shape[2]
    lo = lo_ref[i]
    hi = hi_ref[i]

    def weight_copies(e, slot):
        copies = []
        for m, (src, dst) in enumerate(((wg_hbm, wg_buf), (wu_hbm, wu_buf), (wd_hbm, wd_buf))):
            rows = dst.shape[1] // WEIGHT_DMA_CHUNKS
            for c in range(WEIGHT_DMA_CHUNKS):
                part = pl.ds(c * rows, rows)
                copies.append(pltpu.make_async_copy(src.at[0, e, part], dst.at[slot, part],
                                                    sems.at[slot, m * WEIGHT_DMA_CHUNKS + c]))
        return copies

    @pl.when(newe_ref[i] == 1)
    def _():
        slot = slot_ref[i]
        e = e_ref[i]

        @pl.when(i == 0)
        def _():
            for c in weight_copies(e, slot):
                c.start()

        for c in weight_copies(e, slot):
            c.wait()
        nxt = nexte_ref[i]

        @pl.when(nxt >= 0)
        def _():
            for c in weight_copies(nxt, 1 - slot):
                c.start()

        wgu_bf[:, 0:ff] = wg_buf[slot].astype(BF16)
        wgu_bf[:, ff:] = wu_buf[slot].astype(BF16)
        wd_bf[...] = wd_buf[slot].astype(BF16)

    @pl.when(hi > lo)
    def _():
        gu = _dot_packed(x_ref[...], wgu_bf)
        hid = (jax.nn.silu(gu[:, 0:ff]) * gu[:, ff:]).astype(BF16)
        y = jnp.dot(hid, wd_bf[...], preferred_element_type=F32)
        row = lax.broadcasted_iota(jnp.int32, (y.shape[0], 1), 0)
        y = jnp.where(jnp.logical_and(row >= lo, row < hi), y, 0.0)

        @pl.when(first_ref[i] == 1)
        def _():
            acc_ref[...] = y

        @pl.when(first_ref[i] == 0)
        def _():
            acc_ref[...] += y

        @pl.when(last_ref[i] == 1)
        def _():
            y_ref[...] = _pack_bf16_pair(acc_ref[...])


def _experts(x_sorted, items, w_gate, w_up, w_down):
    n_rows, half = x_sorted.shape
    d = 2 * half
    ff = w_gate.shape[-1]
    n_items = items[0].shape[0]
    hbm = pl.BlockSpec(memory_space=pl.ANY)
    grid_spec = pltpu.PrefetchScalarGridSpec(
        num_scalar_prefetch=len(items),
        grid=(n_items,),
        in_specs=[pl.BlockSpec((EXPERT_ROWS, half), lambda i, blk, *_: (blk[i], 0)), hbm, hbm, hbm],
        out_specs=pl.BlockSpec((EXPERT_ROWS, half), lambda i, blk, *_: (blk[i], 0)),
        scratch_shapes=[pltpu.VMEM((2, d, ff), F32), pltpu.VMEM((2, d, ff), F32), pltpu.VMEM((2, ff, d), F32),
                        pltpu.VMEM((d, 2 * ff), BF16), pltpu.VMEM((ff, d), BF16),
                        pltpu.VMEM((EXPERT_ROWS, d), F32),
                        pltpu.SemaphoreType.DMA((2, 3 * WEIGHT_DMA_CHUNKS))],
    )
    return pl.pallas_call(
        _experts_kernel,
        grid_spec=grid_spec,
        out_shape=jax.ShapeDtypeStruct((n_rows, half), U32),
        compiler_params=_cparams(("arbitrary",), 52),
        name="experts",
    )(*items, x_sorted, w_gate, w_up, w_down)


def _final_kernel(dest_ref, gate_ref, hf_ref, hp_ref, y_ref, wgu_ref, wd_ref, g_ref, b_ref, op_ref, os_ref,
                  ybuf, sem, *, tp):
    tm = hf_ref.shape[0]
    ff = wd_ref.shape[0]

    def row_copy(r, k):
        return pltpu.make_async_copy(y_ref.at[pl.ds(dest_ref[0, k * tm + r], 1)], ybuf.at[k, pl.ds(r, 1)], sem)

    def start(r, carry):
        for k in range(TOP_K):
            row_copy(r, k).start()
        return carry

    def wait(r, carry):
        for k in range(TOP_K):
            row_copy(r, k).wait()
        return carry

    lax.fori_loop(0, tm, start, 0)

    gu = _dot_packed(hp_ref[...], wgu_ref)
    hid = (jax.nn.silu(gu[:, 0:ff]) * gu[:, ff:]).astype(BF16)
    acc = DN_ALPHA * hf_ref[...] + jnp.dot(hid, wd_ref[...], preferred_element_type=F32)

    lax.fori_loop(0, tm, wait, 0)
    gate = gate_ref[...]
    half = ybuf.shape[2]
    acc_lo, acc_hi = acc[:, :half], acc[:, half:]
    for k in range(TOP_K):
        packed = ybuf[k]
        g_k = gate[:, k:k + 1]
        acc_lo = acc_lo + lax.bitcast_convert_type(packed << 16, F32) * g_k
        acc_hi = acc_hi + lax.bitcast_convert_type(packed & jnp.uint32(HIGH_HALF), F32) * g_k
    out = _layer_norm_f32(jnp.concatenate([acc_lo, acc_hi], axis=1), g_ref[...], b_ref[...])

    @pl.when(pl.program_id(0) < tp)
    def _():
        op_ref[...] = out

    @pl.when(pl.program_id(0) >= tp)
    def _():
        os_ref[...] = out


def _final(h_f32, h_pk, y_sorted, dest_tiles, gate_t, w_gu, w_d, g, b, n_prompt, tm):
    n_tok, d = h_f32.shape
    tp = n_prompt // tm
    ts = n_tok // tm - tp
    vec = pl.BlockSpec((1, d), lambda i: (0, 0))
    op_spec, os_spec = _two_group_specs(tm, d, tp, False)
    return pl.pallas_call(
        functools.partial(_final_kernel, tp=tp),
        grid=(tp + ts,),
        in_specs=[
            pl.BlockSpec((None, 1, TOP_K * tm), lambda i: (i, 0, 0), memory_space=pltpu.SMEM),
            pl.BlockSpec((tm, TOP_K), lambda i: (i, 0)),
            pl.BlockSpec((tm, d), lambda i: (i, 0)),
            pl.BlockSpec((tm, d // 2), lambda i: (i, 0)),
            pl.BlockSpec(memory_space=pl.ANY),
            _resident(w_gu.shape), _resident(w_d.shape), vec, vec,
        ],
        out_specs=[op_spec, os_spec],
        out_shape=[jax.ShapeDtypeStruct((tp * tm, d), F32), jax.ShapeDtypeStruct((ts * tm, d), F32)],
        scratch_shapes=[pltpu.VMEM((TOP_K, tm, d // 2), U32), pltpu.SemaphoreType.DMA],
        compiler_params=_cparams(("arbitrary",), 48, disable_bounds_checks=True),
        name="final",
    )(dest_tiles, gate_t, h_f32, h_pk, y_sorted, w_gu, w_d, g, b)


def _lower_bounds(gamma):
    return jnp.cumsum(jax.nn.softmax(gamma.astype(F32), axis=0), axis=0)


def _tile_major(dest, tm):
    k, n = dest.shape
    return jnp.transpose(dest.reshape(k, n // tm, tm), (1, 0, 2)).reshape(n // tm, 1, k * tm)


def _work_items(counts, n_assign):
    rows = EXPERT_ROWS
    n_items = n_assign // rows + N_EXPERTS - 1
    end = jnp.cumsum(counts)
    start = end - counts
    first_blk = start // rows
    last_blk = jnp.maximum(end - 1, 0) // rows
    per_e = jnp.where(counts > 0, last_blk - first_blk + 1, 0)
    it_end = jnp.cumsum(per_e)
    it_start = it_end - per_e
    total = it_end[-1]
    i = jnp.arange(n_items, dtype=jnp.int32)
    ic = jnp.minimum(i, total - 1)
    e = jnp.sum((it_end[None, :] <= ic[:, None]).astype(jnp.int32), axis=1)
    blk = first_blk[e] + (ic - it_start[e])
    lo = jnp.maximum(start[e], blk * rows) - blk * rows
    hi = jnp.minimum(end[e], (blk + 1) * rows) - blk * rows
    valid = i < total
    hi = jnp.where(valid, hi, lo)
    prev_blk = jnp.concatenate([jnp.full((1,), -1, jnp.int32), blk[:-1]])
    first = jnp.where(jnp.logical_and(valid, blk != prev_blk), 1, 0)
    next_blk = jnp.concatenate([blk[1:], jnp.full((1,), -1, jnp.int32)])
    last = jnp.where(jnp.logical_and(valid, jnp.logical_or(blk != next_blk, i == total - 1)), 1, 0)
    prev_e = jnp.concatenate([jnp.full((1,), -1, jnp.int32), e[:-1].astype(jnp.int32)])
    new_e = jnp.where(jnp.logical_and(valid, e != prev_e), 1, 0)
    nonempty = counts > 0
    slot = (jnp.cumsum(nonempty.astype(jnp.int32)) - 1) % 2
    ids = jnp.arange(N_EXPERTS, dtype=jnp.int32)
    later = jnp.concatenate([jnp.where(nonempty, ids, N_EXPERTS)[1:], jnp.full((1,), N_EXPERTS, jnp.int32)])
    nxt = jnp.flip(lax.cummin(jnp.flip(later)))
    nxt = jnp.where(nxt < N_EXPERTS, nxt, -1)
    as_i32 = lambda v: v.astype(jnp.int32)
    return start, (as_i32(blk), as_i32(e), as_i32(lo), as_i32(hi), as_i32(first), as_i32(last), as_i32(new_e),
                   as_i32(slot[e]), as_i32(nxt[e]))


def kernel(x_prompt, x_sample, meta_tokens, ln_in_g, ln_in_b, hg_gamma_fwd, hg_gamma_bwd, w_in, lambda_q1, lambda_k1, lambda_q2, lambda_k2, attn_subln_g, hg_norm_g, w_branch_attn, w_branch_hgrn, w_out, ln1_g, ln1_b, w_router, router_bias, w_sh_gate, w_sh_up, w_sh_down, w_ex_gate, w_ex_up, w_ex_down, ln2_g, ln2_b):
    bp, seq, d = x_prompt.shape
    bs, seq_s, _ = x_sample.shape
    assert seq == seq_s and w_in.shape[0] == DEPTH
    n_seq = bp + bs
    n_tok = n_seq * seq
    layer = 0

    xp = x_prompt.reshape(bp * seq, d)
    xs = x_sample.reshape(bs * seq, d)
    row = lambda v: v.reshape(1, -1).astype(F32)
    ln_g, ln_b = row(ln_in_g), row(ln_in_b)
    w_in_bf = w_in[layer].astype(BF16)

    proj = _ln_matmul(xp, xs, ln_g, ln_b, w_in_bf, tm=min(1024, bs * seq), tn=1024)
    meta = meta_tokens.astype(F32)
    meta_proj = _ln_matmul(meta, meta, ln_g, ln_b, w_in_bf, tm=N_META, tn=1024)[:N_META]
    meta_proj = jnp.pad(meta_proj, ((META_PAD - N_META, 0), (0, 0)))

    lam = (jnp.exp(jnp.sum(lambda_q1[layer].astype(F32) * lambda_k1[layer].astype(F32)))
           - jnp.exp(jnp.sum(lambda_q2[layer].astype(F32) * lambda_k2[layer].astype(F32))) + LAMBDA_INIT)
    slopes = 2.0 ** (-8.0 * jnp.arange(1, A_HEADS + 1, dtype=F32) / A_HEADS)
    attn_params = jnp.concatenate([lam.reshape(1), slopes]).astype(F32)
    subln_g = row(attn_subln_g[layer]) * (1.0 - LAMBDA_INIT)
    ao = _diff_attn(proj, meta_proj, attn_params, subln_g, n_seq, seq, tq=min(256, seq))

    lb_fwd = row(_lower_bounds(hg_gamma_fwd)[layer])
    lb_bwd = row(_lower_bounds(hg_gamma_bwd)[layer])
    ho = _hgrn(proj, meta_proj, lb_fwd, lb_bwd, row(hg_norm_g[layer]), n_seq, seq)

    h_f32, h_pk = _merge(ao, ho, proj, xp, xs, w_branch_attn[layer].astype(BF16),
                         w_branch_hgrn[layer].astype(BF16), w_out[layer].astype(BF16), ln_g, ln_b,
                         row(ln1_g[layer]), row(ln1_b[layer]), tm=min(256, bs * seq))

    idx, gate, rank, counts = _router(h_pk, w_router[layer].T.astype(BF16),
                                      router_bias[layer].astype(F32).reshape(N_EXPERTS, 1), tm=min(512, n_tok))
    start, items = _work_items(counts[:, 0].astype(jnp.int32), n_tok * TOP_K)
    dest = _dest(idx, rank, start.astype(F32).reshape(N_EXPERTS, 1), tm=min(2048, n_tok))

    tm_d = min(256, n_tok)
    x_sorted = _dispatch(h_pk, _tile_major(dest, tm_d), tm_d)
    y_sorted = _experts(x_sorted, items, w_ex_gate, w_ex_up, w_ex_down)

    w_sh_gu = jnp.concatenate([w_sh_gate[layer], w_sh_up[layer]], axis=1).astype(BF16)
    tm_f = min(256, bs * seq)
    y_prompt, y_sample = _final(h_f32, h_pk, y_sorted, _tile_major(dest, tm_f), gate.T, w_sh_gu,
                                w_sh_down[layer].astype(BF16), row(ln2_g[layer]), row(ln2_b[layer]),
                                bp * seq, tm_f)
    return (y_prompt.reshape(bp, seq, d), y_sample.reshape(bs, seq, d))
```

```python
import functools
import math

import jax
import jax.numpy as jnp
from jax import lax
from jax.experimental import pallas as pl
from jax.experimental.pallas import tpu as pltpu

F32 = jnp.float32
BF16 = jnp.bfloat16
U32 = jnp.uint32

N_META = 16
A_HEADS = 8
A_HEAD_DIM = 64
H_HEADS = 8
H_EXPAND = 128
H_DV = 128
H_CHUNK = 64
N_EXPERTS = 256
TOP_K = 8
N_GROUPS = 8
TOPK_GROUPS = 4
GROUP_SIZE = N_EXPERTS // N_GROUPS
ROUTED_SCALE = 2.5
DEPTH = 1
DN_ALPHA = (2 * DEPTH) ** 0.25
EPS = 1e-5
LAMBDA_INIT = 0.8 - 0.6 * math.exp(-0.3 * 0)

LANES = 128
HGRN_HEADS_PER_STEP = 4
HGRN_GROUP = 256
DEC_ROWS = 8
HGRN_UNROLL = 8
META_PAD = 128
NEG_BIG = -1e30
EXPERT_ROWS = 256
WEIGHT_DMA_CHUNKS = 4
HIGH_HALF = 0xFFFF0000


def _cparams(semantics, vmem_mb, **kw):
    return pltpu.CompilerParams(dimension_semantics=semantics, vmem_limit_bytes=vmem_mb * 1024 * 1024, **kw)


def _layer_norm_f32(x, g, b):
    mu = jnp.mean(x, axis=-1, keepdims=True)
    xc = x - mu
    var = jnp.mean(xc * xc, axis=-1, keepdims=True)
    return xc * lax.rsqrt(var + EPS) * g + b


def _resident(shape):
    nd = len(shape)
    return pl.BlockSpec(shape, lambda *_: (0,) * nd, pipeline_mode=pl.Buffered(1))


def _pack_bf16_pair(x):
    k = x.shape[1] // 2
    lo = lax.bitcast_convert_type(x[:, :k].astype(BF16).astype(F32), U32) >> 16
    hi = lax.bitcast_convert_type(x[:, k:].astype(BF16).astype(F32), U32) & jnp.uint32(HIGH_HALF)
    return lo | hi


def _unpack_bf16_pair(p):
    lo = lax.bitcast_convert_type(p << 16, F32).astype(BF16)
    hi = lax.bitcast_convert_type(p & jnp.uint32(HIGH_HALF), F32).astype(BF16)
    return lo, hi


def _dot_packed(p, w_ref):
    k = p.shape[1]
    lo, hi = _unpack_bf16_pair(p)
    return (jnp.dot(lo, w_ref[0:k, :], preferred_element_type=F32)
            + jnp.dot(hi, w_ref[k:, :], preferred_element_type=F32))


def _two_group_specs(tm, d, tp, extra_grid_axes):
    if extra_grid_axes:
        return (pl.BlockSpec((tm, d), lambda i, j: (jnp.minimum(i, tp - 1), 0), pipeline_mode=pl.Buffered(1)),
                pl.BlockSpec((tm, d), lambda i, j: (jnp.maximum(i - tp, 0), 0), pipeline_mode=pl.Buffered(1)))
    return (pl.BlockSpec((tm, d), lambda i: (jnp.minimum(i, tp - 1), 0)),
            pl.BlockSpec((tm, d), lambda i: (jnp.maximum(i - tp, 0), 0)))


def _ln_matmul_kernel(xp_ref, xs_ref, g_ref, b_ref, w_ref, o_ref, xn_ref, *, tp):
    i = pl.program_id(0)
    first_col = pl.program_id(1) == 0

    @pl.when(jnp.logical_and(first_col, i < tp))
    def _():
        xn_ref[...] = _layer_norm_f32(xp_ref[...], g_ref[...], b_ref[...]).astype(BF16)

    @pl.when(jnp.logical_and(first_col, i >= tp))
    def _():
        xn_ref[...] = _layer_norm_f32(xs_ref[...], g_ref[...], b_ref[...]).astype(BF16)

    o_ref[...] = jnp.dot(xn_ref[...], w_ref[...], preferred_element_type=F32).astype(o_ref.dtype)


def _ln_matmul(xp, xs, g, b, w, tm, tn):
    d = xp.shape[1]
    n = w.shape[1]
    tp, ts = xp.shape[0] // tm, xs.shape[0] // tm
    xp_spec, xs_spec = _two_group_specs(tm, d, tp, True)
    return pl.pallas_call(
        functools.partial(_ln_matmul_kernel, tp=tp),
        grid=(tp + ts, n // tn),
        in_specs=[
            xp_spec, xs_spec,
            pl.BlockSpec((1, d), lambda i, j: (0, 0)),
            pl.BlockSpec((1, d), lambda i, j: (0, 0)),
            pl.BlockSpec((d, tn), lambda i, j: (0, j)),
        ],
        out_specs=pl.BlockSpec((tm, tn), lambda i, j: (i, j)),
        out_shape=jax.ShapeDtypeStruct(((tp + ts) * tm, n), BF16),
        scratch_shapes=[pltpu.VMEM((tm, d), BF16)],
        compiler_params=_cparams(("parallel", "arbitrary"), 44),
        name="ln_matmul",
    )(xp, xs, g, b, w)


def _diff_attn_kernel(par_ref, q_ref, k_ref, v_ref, km_ref, vm_ref, g_ref, o_ref, kall_ref, vall_ref, bias_ref):
    h = pl.program_id(0)
    b = pl.program_id(1)
    qi = pl.program_id(2)
    tq = q_ref.shape[0]
    n_keys = kall_ref.shape[0]
    shift = bias_ref.shape[1] - n_keys

    @pl.when(jnp.logical_and(b == 0, qi == 0))
    def _():
        i = lax.broadcasted_iota(jnp.int32, (tq, 1), 0)
        j = lax.broadcasted_iota(jnp.int32, (1, bias_ref.shape[1]), 1)
        bias_ref[...] = -par_ref[1 + h] * jnp.abs(i + (META_PAD + shift) - j).astype(F32)

    @pl.when(qi == 0)
    def _():
        kall_ref[0:META_PAD, :] = km_ref[...]
        kall_ref[META_PAD:, :] = k_ref[...]
        vall_ref[0:META_PAD, :] = vm_ref[...]
        vall_ref[META_PAD:, :] = v_ref[...]

    lam = par_ref[0]
    q = q_ref[...]
    lane = lax.broadcasted_iota(jnp.int32, (1, 2 * A_HEAD_DIM), 1)
    scale = jnp.asarray(A_HEAD_DIM ** -0.5, BF16)
    zero = jnp.zeros((), BF16)
    q1 = jnp.where(lane < A_HEAD_DIM, q, zero) * scale
    q2 = jnp.where(lane >= A_HEAD_DIM, q, zero) * scale
    k_all = kall_ref[...]
    nt = (((1,), (1,)), ((), ()))
    s1 = lax.dot_general(q1, k_all, nt, preferred_element_type=F32)
    s2 = lax.dot_general(q2, k_all, nt, preferred_element_type=F32)

    off = pl.multiple_of(shift - qi * tq, LANES)
    lane_m = lax.broadcasted_iota(jnp.int32, (1, META_PAD), 1)
    pad_mask = jnp.where(lane_m < META_PAD - N_META, NEG_BIG, 0.0)
    bias_meta = bias_ref[:, pl.ds(off, META_PAD)] + pad_mask
    bias_seq = bias_ref[:, pl.ds(off + META_PAD, n_keys - META_PAD)]

    def softmax_parts(s):
        s_meta = s[:, 0:META_PAD] + bias_meta
        s_seq = s[:, META_PAD:] + bias_seq
        m = jnp.maximum(jnp.max(s_meta, axis=-1, keepdims=True), jnp.max(s_seq, axis=-1, keepdims=True))
        p_meta = jnp.exp(s_meta - m)
        p_seq = jnp.exp(s_seq - m)
        l = jnp.sum(p_meta, axis=-1, keepdims=True) + jnp.sum(p_seq, axis=-1, keepdims=True)
        return p_meta.astype(BF16), p_seq.astype(BF16), l

    p1_meta, p1_seq, l1 = softmax_parts(s1)
    p2_meta, p2_seq, l2 = softmax_parts(s2)
    a1 = (1.0 / l1).astype(BF16)
    a2 = (lam / l2).astype(BF16)
    w_meta = p1_meta * a1 - p2_meta * a2
    w_seq = p1_seq * a1 - p2_seq * a2
    o = (jnp.dot(w_meta, vall_ref[0:META_PAD, :], preferred_element_type=F32)
         + jnp.dot(w_seq, vall_ref[META_PAD:, :], preferred_element_type=F32))
    ms = jnp.mean(o * o, axis=-1, keepdims=True)
    o_ref[...] = (o * lax.rsqrt(ms + EPS) * g_ref[...]).astype(o_ref.dtype)


def _diff_attn(proj, meta_proj, params, subln_g, n_seq, seq, tq):
    n_tok = proj.shape[0]
    nq = seq // tq
    hw = 2 * A_HEAD_DIM
    return pl.pallas_call(
        _diff_attn_kernel,
        grid=(A_HEADS, n_seq, nq),
        in_specs=[
            pl.BlockSpec(memory_space=pltpu.SMEM),
            pl.BlockSpec((tq, hw), lambda h, b, i: (b * nq + i, h)),
            pl.BlockSpec((seq, hw), lambda h, b, i: (b, A_HEADS + h)),
            pl.BlockSpec((seq, hw), lambda h, b, i: (b, 2 * A_HEADS + h)),
            pl.BlockSpec((META_PAD, hw), lambda h, b, i: (0, A_HEADS + h)),
            pl.BlockSpec((META_PAD, hw), lambda h, b, i: (0, 2 * A_HEADS + h)),
            pl.BlockSpec((1, hw), lambda h, b, i: (0, 0)),
        ],
        out_specs=pl.BlockSpec((tq, hw), lambda h, b, i: (b * nq + i, h)),
        out_shape=jax.ShapeDtypeStruct((n_tok, A_HEADS * hw), BF16),
        scratch_shapes=[pltpu.VMEM((META_PAD + seq, hw), BF16), pltpu.VMEM((META_PAD + seq, hw), BF16),
                        pltpu.VMEM((tq, META_PAD + 2 * seq - tq), F32)],
        compiler_params=_cparams(("arbitrary", "arbitrary", "arbitrary"), 48),
        name="diff_attn",
    )(params, proj, proj, proj, meta_proj, meta_proj, subln_g)


def _split_cumsum(tri, x):
    hi = x.astype(BF16)
    lo = (x - hi.astype(F32)).astype(BF16)
    return (jnp.dot(tri, hi, preferred_element_type=F32) + jnp.dot(tri, lo, preferred_element_type=F32))


def _hgrn_chunk(q, k, logf, v_bf, state_t, reverse):
    c = q.shape[0]
    r = lax.broadcasted_iota(jnp.int32, (c, c), 0)
    s = lax.broadcasted_iota(jnp.int32, (c, c), 1)
    if reverse:
        visible = s >= r
        ref_row, last_row = c - 1 - c // 2, 0
    else:
        visible = s <= r
        ref_row, last_row = c // 2, c - 1
    tri = jnp.where(visible, 1.0, 0.0).astype(BF16)
    b = _split_cumsum(tri, logf)
    b_ref = b[ref_row:ref_row + 1, :]
    b_last = b[last_row:last_row + 1, :]
    nt = (((1,), (1,)), ((), ()))
    tn = (((0,), (0,)), ((), ()))
    qd = (q * jnp.exp(b - b_ref)).astype(BF16)
    kd = (k * jnp.exp(b_ref - b)).astype(BF16)
    scores = lax.dot_general(qd, kd, nt, preferred_element_type=F32)
    scores = jnp.where(visible, scores, 0.0).astype(BF16)
    o_intra = jnp.dot(scores, v_bf, preferred_element_type=F32)
    qe = (q * jnp.exp(b)).astype(BF16)
    o_inter = lax.dot_general(qe, state_t.astype(BF16), nt, preferred_element_type=F32)
    ks = (k * jnp.exp(b_last - b)).astype(BF16)
    d_state_t = lax.dot_general(v_bf, ks, tn, preferred_element_type=F32)
    new_state_t = state_t * jnp.exp(b_last) + d_state_t
    return o_intra + o_inter, new_state_t


def _forget_gate(logit, lb):
    f = lb + (1.0 - lb) * jax.nn.sigmoid(logit)
    return 1.0 - f, jnp.log(f)


def _chunk_sum_operators(reverse):
    g = HGRN_GROUP
    r = lax.broadcasted_iota(jnp.int32, (g, g), 0)
    s = lax.broadcasted_iota(jnp.int32, (g, g), 1)
    shift = H_CHUNK.bit_length() - 1
    same = (r >> shift) == (s >> shift)
    s_loc = s & (H_CHUNK - 1)
    if reverse:
        cum, ref = s >= r, s_loc >= H_CHUNK - 1 - H_CHUNK // 2
    else:
        cum, ref = s <= r, s_loc <= H_CHUNK // 2
    ops = [jnp.logical_and(same, cum), jnp.logical_and(same, ref), same]
    return jnp.concatenate([jnp.where(o, 1.0, 0.0) for o in ops], axis=0).astype(BF16)


def _hgrn_kernel(q_ref, ff_ref, fb_ref, v_ref, hg_ref, mff_ref, mv_ref, lbf_ref, lbb_ref, g_ref, o_ref,
                 qdf_ref, kdf_ref, qdb_ref, kdb_ref, qe_ref, ks_ref, dec_ref, ds_ref, sp_ref, acc_ref):
    seq = q_ref.shape[0]
    n_heads = q_ref.shape[1] // H_DV
    n_chunks = seq // H_CHUNK
    dk = H_EXPAND
    nt = (((1,), (1,)), ((), ()))
    tn = (((0,), (0,)), ((), ()))
    ops_f = _chunk_sum_operators(False)
    ops_b = _chunk_sum_operators(True)
    r = lax.broadcasted_iota(jnp.int32, (2 * H_CHUNK, 2 * H_CHUNK), 0)
    s = lax.broadcasted_iota(jnp.int32, (2 * H_CHUNK, 2 * H_CHUNK), 1)
    same_chunk = (r >= H_CHUNK) == (s >= H_CHUNK)
    vis_f = jnp.logical_and(same_chunk, s <= r)
    vis_b = jnp.logical_and(same_chunk, s >= r)
    mrow = lax.broadcasted_iota(jnp.int32, (H_CHUNK, 1), 0)
    is_meta = mrow >= H_CHUNK - N_META
    zeros_q = jnp.zeros((H_CHUNK, dk), F32)
    state0 = jnp.zeros((H_DV, dk), F32)
    chunks_per_group = HGRN_GROUP // H_CHUNK

    def chunk_rows(x):
        return jnp.concatenate([x[c * H_CHUNK:c * H_CHUNK + 1] for c in range(chunks_per_group)], axis=0)

    def per_chunk(y):
        return jnp.concatenate([jnp.broadcast_to(y[c:c + 1], (H_CHUNK, y.shape[1]))
                                for c in range(chunks_per_group)], axis=0)

    def dec_row(c):
        return pl.ds(pl.multiple_of(c * DEC_ROWS, DEC_ROWS), DEC_ROWS)

    for j in range(n_heads):
        head = pl.ds(j * H_DV, H_DV)
        lbf = lbf_ref[:, head]
        lbb = lbb_ref[:, head]

        def prepare(g, carry):
            rows = pl.ds(pl.multiple_of(g * HGRN_GROUP, HGRN_GROUP), HGRN_GROUP)
            q = jax.nn.silu(q_ref[rows, head].astype(F32))
            for logit_ref, lb, ops, qd_ref, kd_ref, col in ((ff_ref, lbf, ops_f, qdf_ref, kdf_ref, 0),
                                                           (fb_ref, lbb, ops_b, qdb_ref, kdb_ref, dk)):
                k, logf = _forget_gate(logit_ref[rows, head].astype(F32), lb)
                hi = logf.astype(BF16)
                lo = (logf - hi.astype(F32)).astype(BF16)
                sums = jnp.dot(ops, jnp.concatenate([hi, lo], axis=1), preferred_element_type=F32)
                sums = sums[:, 0:dk] + sums[:, dk:]
                b = sums[0:HGRN_GROUP]
                b_mid = sums[HGRN_GROUP:2 * HGRN_GROUP]
                b_all = sums[2 * HGRN_GROUP:]
                e_mid = per_chunk(jnp.exp(chunk_rows(b_mid)))
                e_all = jnp.exp(chunk_rows(b_all))
                e_rest = per_chunk(jnp.exp(chunk_rows(b_all) - chunk_rows(b_mid)))
                qd = q * jnp.exp(b - b_mid)
                kd = k * jnp.exp(b_mid - b)
                qd_ref[rows, :] = qd.astype(BF16)
                kd_ref[rows, :] = kd.astype(BF16)
                qe_ref[rows, col:col + dk] = (qd * e_mid).astype(BF16)
                ks_ref[rows, col:col + dk] = (kd * e_rest).astype(BF16)
                group_rows = chunks_per_group * DEC_ROWS
                dec_ref[pl.ds(pl.multiple_of(g * group_rows, group_rows), group_rows), col:col + dk] = (
                    jnp.concatenate([jnp.broadcast_to(e_all[c:c + 1], (DEC_ROWS, dk))
                                     for c in range(chunks_per_group)], axis=0))
            return carry

        lax.fori_loop(0, seq // HGRN_GROUP, prepare, 0, unroll=2)

        def intra(p, carry):
            rows = pl.ds(pl.multiple_of(p * (2 * H_CHUNK), 2 * H_CHUNK), 2 * H_CHUNK)
            s_f = lax.dot_general(qdf_ref[rows, :], kdf_ref[rows, :], nt, preferred_element_type=F32)
            s_b = lax.dot_general(qdb_ref[rows, :], kdb_ref[rows, :], nt, preferred_element_type=F32)
            w = (jnp.where(vis_f, s_f, 0.0) + jnp.where(vis_b, s_b, 0.0)).astype(BF16)
            acc_ref[rows, head] = jnp.dot(w, v_ref[rows, head], preferred_element_type=F32)
            for half in range(2):
                c = 2 * p + half
                rows_c = pl.ds(pl.multiple_of(c * H_CHUNK, H_CHUNK), H_CHUNK)
                ds_ref[pl.ds(pl.multiple_of(c * H_DV, H_DV), H_DV), :] = lax.dot_general(
                    v_ref[rows_c, head], ks_ref[rows_c, :], tn, preferred_element_type=F32)
            return carry

        lax.fori_loop(0, n_chunks // 2, intra, 0, unroll=HGRN_UNROLL // 2)

        mk, mlogf = _forget_gate(mff_ref[:, head].astype(F32), lbf)
        mk = jnp.where(is_meta, mk, 0.0)
        mlogf = jnp.where(is_meta, mlogf, 0.0)
        state_f0 = _hgrn_chunk(zeros_q, mk, mlogf, mv_ref[:, head], state0, reverse=False)[1]

        def scan(i, carry):
            st_f, st_b = carry
            cf = i
            cb = n_chunks - 1 - i
            blk_f = pl.ds(pl.multiple_of(cf * H_DV, H_DV), H_DV)
            blk_b = pl.ds(pl.multiple_of(cb * H_DV, H_DV), H_DV)
            sp_ref[blk_f, 0:dk] = st_f.astype(BF16)
            sp_ref[blk_b, dk:] = st_b.astype(BF16)
            st_f = st_f * dec_ref[dec_row(cf), 0:dk][0:1] + ds_ref[blk_f, 0:dk]
            st_b = st_b * dec_ref[dec_row(cb), dk:][0:1] + ds_ref[blk_b, dk:]
            return st_f, st_b

        lax.fori_loop(0, n_chunks, scan, (state_f0, state0))

        def inter(c, carry):
            rows = pl.ds(pl.multiple_of(c * H_CHUNK, H_CHUNK), H_CHUNK)
            state = sp_ref[pl.ds(pl.multiple_of(c * H_DV, H_DV), H_DV), :]
            acc_ref[rows, head] += lax.dot_general(qe_ref[rows, :], state, nt, preferred_element_type=F32)
            return carry

        lax.fori_loop(0, n_chunks, inter, 0, unroll=HGRN_UNROLL)

    for j in range(n_heads):
        head = pl.ds(j * H_DV, H_DV)
        o = acc_ref[:, head]
        ms = jnp.mean(o * o, axis=-1, keepdims=True)
        normed = o * lax.rsqrt(ms + EPS) * g_ref[...]
        o_ref[:, head] = (normed * jax.nn.sigmoid(hg_ref[:, head].astype(F32))).astype(o_ref.dtype)


def _hgrn(proj, meta_proj, lb_fwd, lb_bwd, norm_g, n_seq, seq):
    n_tok = proj.shape[0]
    hps = HGRN_HEADS_PER_STEP
    w = hps * H_DV
    col0 = 3 * A_HEADS * (2 * A_HEAD_DIM) // w
    per_sec = H_HEADS // hps
    meta_blk = META_PAD // H_CHUNK - 1

    def sec(k):
        return pl.BlockSpec((seq, w), lambda b, h, k=k: (b, col0 + k * per_sec + h))

    def meta_sec(k):
        return pl.BlockSpec((H_CHUNK, w), lambda b, h, k=k: (meta_blk, col0 + k * per_sec + h))

    return pl.pallas_call(
        _hgrn_kernel,
        grid=(n_seq, per_sec),
        in_specs=[
            sec(0), sec(1), sec(2), sec(3), sec(4),
            meta_sec(1), meta_sec(3),
            pl.BlockSpec((1, w), lambda b, h: (0, h)),
            pl.BlockSpec((1, w), lambda b, h: (0, h)),
            pl.BlockSpec((1, H_DV), lambda b, h: (0, 0)),
        ],
        out_specs=pl.BlockSpec((seq, w), lambda b, h: (b, h)),
        out_shape=jax.ShapeDtypeStruct((n_tok, H_HEADS * H_DV), BF16),
        scratch_shapes=[
            pltpu.VMEM((seq, H_EXPAND), BF16), pltpu.VMEM((seq, H_EXPAND), BF16),
            pltpu.VMEM((seq, H_EXPAND), BF16), pltpu.VMEM((seq, H_EXPAND), BF16),
            pltpu.VMEM((seq, 2 * H_EXPAND), BF16),
            pltpu.VMEM((seq, 2 * H_EXPAND), BF16),
            pltpu.VMEM((seq // H_CHUNK * DEC_ROWS, 2 * H_EXPAND), F32),
            pltpu.VMEM((seq // H_CHUNK * H_DV, 2 * H_EXPAND), F32),
            pltpu.VMEM((seq // H_CHUNK * H_DV, 2 * H_EXPAND), BF16),
            pltpu.VMEM((seq, w), F32),
        ],
        compiler_params=_cparams(("parallel", "parallel"), 48),
        name="hgrn",
    )(proj, proj, proj, proj, proj, meta_proj, meta_proj, lb_fwd, lb_bwd, norm_g)


def _merge_kernel(ao_ref, ho_ref, ga_ref, gh_ref, xp_ref, xs_ref, wa_ref, wh_ref, wo_ref, lng_ref, lnb_ref,
                  l1g_ref, l1b_ref, hf_ref, hp_ref, *, tp):
    pa = jnp.dot(ao_ref[...], wa_ref[...], preferred_element_type=F32)
    ph = jnp.dot(ho_ref[...], wh_ref[...], preferred_element_type=F32)
    merged = (jax.nn.sigmoid(ga_ref[...].astype(F32)) * pa + jax.nn.sigmoid(gh_ref[...].astype(F32)) * ph)
    t = jnp.dot(merged.astype(BF16), wo_ref[...], preferred_element_type=F32)

    def finish(x_ref):
        x_in = _layer_norm_f32(x_ref[...], lng_ref[...], lnb_ref[...])
        h = _layer_norm_f32(DN_ALPHA * x_in + t, l1g_ref[...], l1b_ref[...])
        hf_ref[...] = h
        hp_ref[...] = _pack_bf16_pair(h)

    @pl.when(pl.program_id(0) < tp)
    def _():
        finish(xp_ref)

    @pl.when(pl.program_id(0) >= tp)
    def _():
        finish(xs_ref)


def _merge(ao, ho, proj, xp, xs, wa, wh, wo, ln_g, ln_b, l1g, l1b, tm):
    d = xp.shape[1]
    tp, ts = xp.shape[0] // tm, xs.shape[0] // tm
    n_tok = (tp + ts) * tm
    gate_blk0 = (proj.shape[1] - 2 * d) // d
    vec = pl.BlockSpec((1, d), lambda i: (0, 0))
    xp_spec, xs_spec = _two_group_specs(tm, d, tp, False)
    return pl.pallas_call(
        functools.partial(_merge_kernel, tp=tp),
        grid=(tp + ts,),
        in_specs=[
            pl.BlockSpec((tm, ao.shape[1]), lambda i: (i, 0)),
            pl.BlockSpec((tm, ho.shape[1]), lambda i: (i, 0)),
            pl.BlockSpec((tm, d), lambda i: (i, gate_blk0)),
            pl.BlockSpec((tm, d), lambda i: (i, gate_blk0 + 1)),
            xp_spec, xs_spec,
            _resident(wa.shape), _resident(wh.shape), _resident(wo.shape),
            vec, vec, vec, vec,
        ],
        out_specs=[pl.BlockSpec((tm, d), lambda i: (i, 0)), pl.BlockSpec((tm, d // 2), lambda i: (i, 0))],
        out_shape=[jax.ShapeDtypeStruct((n_tok, d), F32), jax.ShapeDtypeStruct((n_tok, d // 2), U32)],
        compiler_params=_cparams(("parallel",), 56),
        name="merge",
    )(ao, ho, proj, proj, xp, xs, wa, wh, wo, ln_g, ln_b, l1g, l1b)


def _first_argmax(x, iota, size):
    m = jnp.max(x, axis=0, keepdims=True)
    idx = jnp.min(jnp.where(x == m, iota, size), axis=0, keepdims=True)
    return m, idx


def _router_kernel(h_ref, wrt_ref, bias_ref, idx_ref, gate_ref, rank_ref, cnt_ref, base_ref):
    tm = h_ref.shape[0]
    half = h_ref.shape[1]

    @pl.when(pl.program_id(0) == 0)
    def _():
        base_ref[...] = jnp.zeros_like(base_ref)

    nt = (((1,), (1,)), ((), ()))
    h_lo, h_hi = _unpack_bf16_pair(h_ref[...])
    logits = (lax.dot_general(wrt_ref[:, 0:half], h_lo, nt, preferred_element_type=F32)
              + lax.dot_general(wrt_ref[:, half:], h_hi, nt, preferred_element_type=F32))
    scores = jax.nn.sigmoid(logits)
    biased = scores + bias_ref[...]
    neg_inf = jnp.asarray(-jnp.inf, F32)

    iota_g = lax.broadcasted_iota(jnp.int32, (GROUP_SIZE, tm), 0)
    iota_n = lax.broadcasted_iota(jnp.int32, (N_GROUPS, tm), 0)
    grp = jnp.full((N_GROUPS, tm), neg_inf, F32)
    for g in range(N_GROUPS):
        xg = biased[g * GROUP_SIZE:(g + 1) * GROUP_SIZE, :]
        m1, i1 = _first_argmax(xg, iota_g, GROUP_SIZE)
        m2 = jnp.max(jnp.where(iota_g == i1, neg_inf, xg), axis=0, keepdims=True)
        grp = jnp.where(iota_n == g, m1 + m2, grp)

    keep_f = jnp.zeros((N_GROUPS, tm), F32)
    for _ in range(TOPK_GROUPS):
        _, ig = _first_argmax(grp, iota_n, N_GROUPS)
        sel = iota_n == ig
        keep_f = jnp.where(sel, 1.0, keep_f)
        grp = jnp.where(sel, neg_inf, grp)

    masked = jnp.concatenate(
        [jnp.where(keep_f[g:g + 1, :] > 0.5, biased[g * GROUP_SIZE:(g + 1) * GROUP_SIZE, :], neg_inf)
         for g in range(N_GROUPS)], axis=0)

    iota_e = lax.broadcasted_iota(jnp.int32, (N_EXPERTS, tm), 0)
    chosen = jnp.zeros((N_EXPERTS, tm), F32)
    idxs, gates = [], []
    for _ in range(TOP_K):
        _, ie = _first_argmax(masked, iota_e, N_EXPERTS)
        sel = iota_e == ie
        gates.append(jnp.sum(jnp.where(sel, scores, 0.0), axis=0, keepdims=True))
        idxs.append(ie)
        chosen = jnp.where(sel, 1.0, chosen)
        masked = jnp.where(sel, neg_inf, masked)

    denom = gates[0]
    for gk in gates[1:]:
        denom = denom + gk

    r = lax.broadcasted_iota(jnp.int32, (tm, tm), 0)
    c = lax.broadcasted_iota(jnp.int32, (tm, tm), 1)
    before = jnp.where(r < c, 1.0, 0.0).astype(BF16)
    pos = jnp.dot(chosen.astype(BF16), before, preferred_element_type=F32) + base_ref[...]
    for k in range(TOP_K):
        sel = iota_e == idxs[k]
        rank_ref[k:k + 1, :] = jnp.sum(jnp.where(sel, pos, 0.0), axis=0, keepdims=True).astype(jnp.int32)
        idx_ref[k:k + 1, :] = idxs[k]
        gate_ref[k:k + 1, :] = gates[k] / denom * ROUTED_SCALE

    base_ref[...] = base_ref[...] + jnp.sum(chosen, axis=1, keepdims=True)
    cnt_ref[...] = base_ref[...]


def _router(h_pk, w_router_t, bias_col, tm):
    n_tok, half = h_pk.shape
    row_blk = pl.BlockSpec((TOP_K, tm), lambda i: (0, i))
    return pl.pallas_call(
        _router_kernel,
        grid=(n_tok // tm,),
        in_specs=[
            pl.BlockSpec((tm, half), lambda i: (i, 0)),
            _resident(w_router_t.shape),
            pl.BlockSpec((N_EXPERTS, 1), lambda i: (0, 0)),
        ],
        out_specs=[row_blk, row_blk, row_blk, pl.BlockSpec((N_EXPERTS, 1), lambda i: (0, 0))],
        out_shape=[
            jax.ShapeDtypeStruct((TOP_K, n_tok), jnp.int32),
            jax.ShapeDtypeStruct((TOP_K, n_tok), F32),
            jax.ShapeDtypeStruct((TOP_K, n_tok), jnp.int32),
            jax.ShapeDtypeStruct((N_EXPERTS, 1), F32),
        ],
        scratch_shapes=[pltpu.VMEM((N_EXPERTS, 1), F32)],
        compiler_params=_cparams(("arbitrary",), 32),
        name="router",
    )(h_pk, w_router_t, bias_col)


def _dest_kernel(idx_ref, rank_ref, start_ref, dest_ref):
    tm = idx_ref.shape[1]
    iota_e = lax.broadcasted_iota(jnp.int32, (N_EXPERTS, tm), 0)
    start = start_ref[...]
    for k in range(TOP_K):
        sel = iota_e == idx_ref[k:k + 1, :]
        base = jnp.sum(jnp.where(sel, start, 0.0), axis=0, keepdims=True)
        dest_ref[k:k + 1, :] = base.astype(jnp.int32) + rank_ref[k:k + 1, :]


def _dest(idx, rank, start_col, tm):
    n_tok = idx.shape[1]
    row_blk = pl.BlockSpec((TOP_K, tm), lambda i: (0, i))
    return pl.pallas_call(
        _dest_kernel,
        grid=(n_tok // tm,),
        in_specs=[row_blk, row_blk, pl.BlockSpec((N_EXPERTS, 1), lambda i: (0, 0))],
        out_specs=row_blk,
        out_shape=jax.ShapeDtypeStruct((TOP_K, n_tok), jnp.int32),
        compiler_params=_cparams(("parallel",), 32),
        name="dest",
    )(idx, rank, start_col)


def _dispatch_kernel(dest_ref, h_ref, xs_ref, sem):
    tm = h_ref.shape[0]

    def row_copy(r, k):
        return pltpu.make_async_copy(h_ref.at[pl.ds(r, 1)], xs_ref.at[pl.ds(dest_ref[0, k * tm + r], 1)], sem)

    def start(r, carry):
        for k in range(TOP_K):
            row_copy(r, k).start()
        return carry

    def wait(r, carry):
        for k in range(TOP_K):
            row_copy(r, k).wait()
        return carry

    lax.fori_loop(0, tm, start, 0)
    lax.fori_loop(0, tm, wait, 0)


def _dispatch(h_pk, dest_tiles, tm):
    n_tok, half = h_pk.shape
    return pl.pallas_call(
        _dispatch_kernel,
        grid=(n_tok // tm,),
        in_specs=[
            pl.BlockSpec((None, 1, TOP_K * tm), lambda i: (i, 0, 0), memory_space=pltpu.SMEM),
            pl.BlockSpec((tm, half), lambda i: (i, 0)),
        ],
        out_specs=pl.BlockSpec(memory_space=pl.ANY),
        out_shape=jax.ShapeDtypeStruct((n_tok * TOP_K, half), U32),
        scratch_shapes=[pltpu.SemaphoreType.DMA],
        compiler_params=_cparams(("arbitrary",), 32, disable_bounds_checks=True),
        name="dispatch",
    )(dest_tiles, h_pk)


def _experts_kernel(blk_ref, e_ref, lo_ref, hi_ref, first_ref, last_ref, newe_ref, slot_ref, nexte_ref,
                    x_ref, wg_hbm, wu_hbm, wd_hbm, y_ref, wg_buf, wu_buf, wd_buf, wgu_bf, wd_bf, acc_ref, sems):
    i = pl.program_id(0)
    ff = wg_buf.shape[2]
    lo = lo_ref[i]
    hi = hi_ref[i]

    def weight_copies(e, slot):
        copies = []
        for m, (src, dst) in enumerate(((wg_hbm, wg_buf), (wu_hbm, wu_buf), (wd_hbm, wd_buf))):
            rows = dst.shape[1] // WEIGHT_DMA_CHUNKS
            for c in range(WEIGHT_DMA_CHUNKS):
                part = pl.ds(c * rows, rows)
                copies.append(pltpu.make_async_copy(src.at[0, e, part], dst.at[slot, part],
                                                    sems.at[slot, m * WEIGHT_DMA_CHUNKS + c]))
        return copies

    @pl.when(newe_ref[i] == 1)
    def _():
        slot = slot_ref[i]
        e = e_ref[i]

        @pl.when(i == 0)
        def _():
            for c in weight_copies(e, slot):
                c.start()

        for c in weight_copies(e, slot):
            c.wait()
        nxt = nexte_ref[i]

        @pl.when(nxt >= 0)
        def _():
            for c in weight_copies(nxt, 1 - slot):
                c.start()

        wgu_bf[:, 0:ff] = wg_buf[slot].astype(BF16)
        wgu_bf[:, ff:] = wu_buf[slot].astype(BF16)
        wd_bf[...] = wd_buf[slot].astype(BF16)

    @pl.when(hi > lo)
    def _():
        gu = _dot_packed(x_ref[...], wgu_bf)
        hid = (jax.nn.silu(gu[:, 0:ff]) * gu[:, ff:]).astype(BF16)
        y = jnp.dot(hid, wd_bf[...], preferred_element_type=F32)
        row = lax.broadcasted_iota(jnp.int32, (y.shape[0], 1), 0)
        y = jnp.where(jnp.logical_and(row >= lo, row < hi), y, 0.0)

        @pl.when(first_ref[i] == 1)
        def _():
            acc_ref[...] = y

        @pl.when(first_ref[i] == 0)
        def _():
            acc_ref[...] += y

        @pl.when(last_ref[i] == 1)
        def _():
            y_ref[...] = _pack_bf16_pair(acc_ref[...])


def _experts(x_sorted, items, w_gate, w_up, w_down):
    n_rows, half = x_sorted.shape
    d = 2 * half
    ff = w_gate.shape[-1]
    n_items = items[0].shape[0]
    hbm = pl.BlockSpec(memory_space=pl.ANY)
    grid_spec = pltpu.PrefetchScalarGridSpec(
        num_scalar_prefetch=len(items),
        grid=(n_items,),
        in_specs=[pl.BlockSpec((EXPERT_ROWS, half), lambda i, blk, *_: (blk[i], 0)), hbm, hbm, hbm],
        out_specs=pl.BlockSpec((EXPERT_ROWS, half), lambda i, blk, *_: (blk[i], 0)),
        scratch_shapes=[pltpu.VMEM((2, d, ff), F32), pltpu.VMEM((2, d, ff), F32), pltpu.VMEM((2, ff, d), F32),
                        pltpu.VMEM((d, 2 * ff), BF16), pltpu.VMEM((ff, d), BF16),
                        pltpu.VMEM((EXPERT_ROWS, d), F32),
                        pltpu.SemaphoreType.DMA((2, 3 * WEIGHT_DMA_CHUNKS))],
    )
    return pl.pallas_call(
        _experts_kernel,
        grid_spec=grid_spec,
        out_shape=jax.ShapeDtypeStruct((n_rows, half), U32),
        compiler_params=_cparams(("arbitrary",), 52),
        name="experts",
    )(*items, x_sorted, w_gate, w_up, w_down)


def _final_kernel(dest_ref, gate_ref, hf_ref, hp_ref, y_ref, wgu_ref, wd_ref, g_ref, b_ref, op_ref, os_ref,
                  ybuf, sem, *, tp):
    tm = hf_ref.shape[0]
    ff = wd_ref.shape[0]

    def row_copy(r, k):
        return pltpu.make_async_copy(y_ref.at[pl.ds(dest_ref[0, k * tm + r], 1)], ybuf.at[k, pl.ds(r, 1)], sem)

    def start(r, carry):
        for k in range(TOP_K):
            row_copy(r, k).start()
        return carry

    def wait(r, carry):
        for k in range(TOP_K):
            row_copy(r, k).wait()
        return carry

    lax.fori_loop(0, tm, start, 0)

    gu = _dot_packed(hp_ref[...], wgu_ref)
    hid = (jax.nn.silu(gu[:, 0:ff]) * gu[:, ff:]).astype(BF16)
    acc = DN_ALPHA * hf_ref[...] + jnp.dot(hid, wd_ref[...], preferred_element_type=F32)

    lax.fori_loop(0, tm, wait, 0)
    gate = gate_ref[...]
    half = ybuf.shape[2]
    acc_lo, acc_hi = acc[:, :half], acc[:, half:]
    for k in range(TOP_K):
        packed = ybuf[k]
        g_k = gate[:, k:k + 1]
        acc_lo = acc_lo + lax.bitcast_convert_type(packed << 16, F32) * g_k
        acc_hi = acc_hi + lax.bitcast_convert_type(packed & jnp.uint32(HIGH_HALF), F32) * g_k
    out = _layer_norm_f32(jnp.concatenate([acc_lo, acc_hi], axis=1), g_ref[...], b_ref[...])

    @pl.when(pl.program_id(0) < tp)
    def _():
        op_ref[...] = out

    @pl.when(pl.program_id(0) >= tp)
    def _():
        os_ref[...] = out


def _final(h_f32, h_pk, y_sorted, dest_tiles, gate_t, w_gu, w_d, g, b, n_prompt, tm):
    n_tok, d = h_f32.shape
    tp = n_prompt // tm
    ts = n_tok // tm - tp
    vec = pl.BlockSpec((1, d), lambda i: (0, 0))
    op_spec, os_spec = _two_group_specs(tm, d, tp, False)
    return pl.pallas_call(
        functools.partial(_final_kernel, tp=tp),
        grid=(tp + ts,),
        in_specs=[
            pl.BlockSpec((None, 1, TOP_K * tm), lambda i: (i, 0, 0), memory_space=pltpu.SMEM),
            pl.BlockSpec((tm, TOP_K), lambda i: (i, 0)),
            pl.BlockSpec((tm, d), lambda i: (i, 0)),
            pl.BlockSpec((tm, d // 2), lambda i: (i, 0)),
            pl.BlockSpec(memory_space=pl.ANY),
            _resident(w_gu.shape), _resident(w_d.shape), vec, vec,
        ],
        out_specs=[op_spec, os_spec],
        out_shape=[jax.ShapeDtypeStruct((tp * tm, d), F32), jax.ShapeDtypeStruct((ts * tm, d), F32)],
        scratch_shapes=[pltpu.VMEM((TOP_K, tm, d // 2), U32), pltpu.SemaphoreType.DMA],
        compiler_params=_cparams(("arbitrary",), 48, disable_bounds_checks=True),
        name="final",
    )(dest_tiles, gate_t, h_f32, h_pk, y_sorted, w_gu, w_d, g, b)


def _lower_bounds(gamma):
    return jnp.cumsum(jax.nn.softmax(gamma.astype(F32), axis=0), axis=0)


def _tile_major(dest, tm):
    k, n = dest.shape
    return jnp.transpose(dest.reshape(k, n // tm, tm), (1, 0, 2)).reshape(n // tm, 1, k * tm)


def _work_items(counts, n_assign):
    rows = EXPERT_ROWS
    n_items = n_assign // rows + N_EXPERTS - 1
    end = jnp.cumsum(counts)
    start = end - counts
    first_blk = start // rows
    last_blk = jnp.maximum(end - 1, 0) // rows
    per_e = jnp.where(counts > 0, last_blk - first_blk + 1, 0)
    it_end = jnp.cumsum(per_e)
    it_start = it_end - per_e
    total = it_end[-1]
    i = jnp.arange(n_items, dtype=jnp.int32)
    ic = jnp.minimum(i, total - 1)
    e = jnp.sum((it_end[None, :] <= ic[:, None]).astype(jnp.int32), axis=1)
    blk = first_blk[e] + (ic - it_start[e])
    lo = jnp.maximum(start[e], blk * rows) - blk * rows
    hi = jnp.minimum(end[e], (blk + 1) * rows) - blk * rows
    valid = i < total
    hi = jnp.where(valid, hi, lo)
    prev_blk = jnp.concatenate([jnp.full((1,), -1, jnp.int32), blk[:-1]])
    first = jnp.where(jnp.logical_and(valid, blk != prev_blk), 1, 0)
    next_blk = jnp.concatenate([blk[1:], jnp.full((1,), -1, jnp.int32)])
    last = jnp.where(jnp.logical_and(valid, jnp.logical_or(blk != next_blk, i == total - 1)), 1, 0)
    prev_e = jnp.concatenate([jnp.full((1,), -1, jnp.int32), e[:-1].astype(jnp.int32)])
    new_e = jnp.where(jnp.logical_and(valid, e != prev_e), 1, 0)
    nonempty = counts > 0
    slot = (jnp.cumsum(nonempty.astype(jnp.int32)) - 1) % 2
    ids = jnp.arange(N_EXPERTS, dtype=jnp.int32)
    later = jnp.concatenate([jnp.where(nonempty, ids, N_EXPERTS)[1:], jnp.full((1,), N_EXPERTS, jnp.int32)])
    nxt = jnp.flip(lax.cummin(jnp.flip(later)))
    nxt = jnp.where(nxt < N_EXPERTS, nxt, -1)
    as_i32 = lambda v: v.astype(jnp.int32)
    return start, (as_i32(blk), as_i32(e), as_i32(lo), as_i32(hi), as_i32(first), as_i32(last), as_i32(new_e),
                   as_i32(slot[e]), as_i32(nxt[e]))


def kernel(x_prompt, x_sample, meta_tokens, ln_in_g, ln_in_b, hg_gamma_fwd, hg_gamma_bwd, w_in, lambda_q1, lambda_k1, lambda_q2, lambda_k2, attn_subln_g, hg_norm_g, w_branch_attn, w_branch_hgrn, w_out, ln1_g, ln1_b, w_router, router_bias, w_sh_gate, w_sh_up, w_sh_down, w_ex_gate, w_ex_up, w_ex_down, ln2_g, ln2_b):
    bp, seq, d = x_prompt.shape
    bs, seq_s, _ = x_sample.shape
    assert seq == seq_s and w_in.shape[0] == DEPTH
    n_seq = bp + bs
    n_tok = n_seq * seq
    layer = 0

    xp = x_prompt.reshape(bp * seq, d)
    xs = x_sample.reshape(bs * seq, d)
    row = lambda v: v.reshape(1, -1).astype(F32)
    ln_g, ln_b = row(ln_in_g), row(ln_in_b)
    w_in_bf = w_in[layer].astype(BF16)

    proj = _ln_matmul(xp, xs, ln_g, ln_b, w_in_bf, tm=min(1024, bs * seq), tn=1024)
    meta = meta_tokens.astype(F32)
    meta_proj = _ln_matmul(meta, meta, ln_g, ln_b, w_in_bf, tm=N_META, tn=1024)[:N_META]
    meta_proj = jnp.pad(meta_proj, ((META_PAD - N_META, 0), (0, 0)))

    lam = (jnp.exp(jnp.sum(lambda_q1[layer].astype(F32) * lambda_k1[layer].astype(F32)))
           - jnp.exp(jnp.sum(lambda_q2[layer].astype(F32) * lambda_k2[layer].astype(F32))) + LAMBDA_INIT)
    slopes = 2.0 ** (-8.0 * jnp.arange(1, A_HEADS + 1, dtype=F32) / A_HEADS)
    attn_params = jnp.concatenate([lam.reshape(1), slopes]).astype(F32)
    subln_g = row(attn_subln_g[layer]) * (1.0 - LAMBDA_INIT)
    ao = _diff_attn(proj, meta_proj, attn_params, subln_g, n_seq, seq, tq=min(256, seq))

    lb_fwd = row(_lower_bounds(hg_gamma_fwd)[layer])
    lb_bwd = row(_lower_bounds(hg_gamma_bwd)[layer])
    ho = _hgrn(proj, meta_proj, lb_fwd, lb_bwd, row(hg_norm_g[layer]), n_seq, seq)

    h_f32, h_pk = _merge(ao, ho, proj, xp, xs, w_branch_attn[layer].astype(BF16),
                         w_branch_hgrn[layer].astype(BF16), w_out[layer].astype(BF16), ln_g, ln_b,
                         row(ln1_g[layer]), row(ln1_b[layer]), tm=min(256, bs * seq))

    idx, gate, rank, counts = _router(h_pk, w_router[layer].T.astype(BF16),
                                      router_bias[layer].astype(F32).reshape(N_EXPERTS, 1), tm=min(512, n_tok))
    start, items = _work_items(counts[:, 0].astype(jnp.int32), n_tok * TOP_K)
    dest = _dest(idx, rank, start.astype(F32).reshape(N_EXPERTS, 1), tm=min(2048, n_tok))

    tm_d = min(256, n_tok)
    x_sorted = _dispatch(h_pk, _tile_major(dest, tm_d), tm_d)
    y_sorted = _experts(x_sorted, items, w_ex_gate, w_ex_up, w_ex_down)

    w_sh_gu = jnp.concatenate([w_sh_gate[layer], w_sh_up[layer]], axis=1).astype(BF16)
    tm_f = min(256, bs * seq)
    y_prompt, y_sample = _final(h_f32, h_pk, y_sorted, _tile_major(dest, tm_f), gate.T, w_sh_gu,
                                w_sh_down[layer].astype(BF16), row(ln2_g[layer]), row(ln2_b[layer]),
                                bp * seq, tm_f)
    return (y_prompt.reshape(bp, seq, d), y_sample.reshape(bs, seq, d))
```

```python
import functools
import math

import jax
import jax.numpy as jnp
from jax import lax
from jax.experimental import pallas as pl
from jax.experimental.pallas import tpu as pltpu

F32 = jnp.float32
BF16 = jnp.bfloat16
U32 = jnp.uint32

N_META = 16
A_HEADS = 8
A_HEAD_DIM = 64
H_HEADS = 8
H_EXPAND = 128
H_DV = 128
H_CHUNK = 64
N_EXPERTS = 256
TOP_K = 8
N_GROUPS = 8
TOPK_GROUPS = 4
GROUP_SIZE = N_EXPERTS // N_GROUPS
ROUTED_SCALE = 2.5
DEPTH = 1
DN_ALPHA = (2 * DEPTH) ** 0.25
EPS = 1e-5
LAMBDA_INIT = 0.8 - 0.6 * math.exp(-0.3 * 0)

LANES = 128
HGRN_HEADS_PER_STEP = 4
HGRN_GROUP = 256
DEC_ROWS = 8
HGRN_UNROLL = 8
META_PAD = 128
NEG_BIG = -1e30
EXPERT_ROWS = 256
WEIGHT_DMA_CHUNKS = 4
ROW_ALIGN = 8
HIGH_HALF = 0xFFFF0000


def _cparams(semantics, vmem_mb, **kw):
    return pltpu.CompilerParams(dimension_semantics=semantics, vmem_limit_bytes=vmem_mb * 1024 * 1024, **kw)


def _layer_norm_f32(x, g, b):
    mu = jnp.mean(x, axis=-1, keepdims=True)
    xc = x - mu
    var = jnp.mean(xc * xc, axis=-1, keepdims=True)
    return xc * lax.rsqrt(var + EPS) * g + b


def _resident(shape):
    nd = len(shape)
    return pl.BlockSpec(shape, lambda *_: (0,) * nd, pipeline_mode=pl.Buffered(1))


def _pack_bf16_pair(x):
    k = x.shape[1] // 2
    lo = lax.bitcast_convert_type(x[:, :k].astype(BF16).astype(F32), U32) >> 16
    hi = lax.bitcast_convert_type(x[:, k:].astype(BF16).astype(F32), U32) & jnp.uint32(HIGH_HALF)
    return lo | hi


def _unpack_bf16_pair(p):
    lo = lax.bitcast_convert_type(p << 16, F32).astype(BF16)
    hi = lax.bitcast_convert_type(p & jnp.uint32(HIGH_HALF), F32).astype(BF16)
    return lo, hi


def _dot_packed(p, w_ref):
    k = p.shape[1]
    lo, hi = _unpack_bf16_pair(p)
    return (jnp.dot(lo, w_ref[0:k, :], preferred_element_type=F32)
            + jnp.dot(hi, w_ref[k:, :], preferred_element_type=F32))


def _two_group_specs(tm, d, tp, extra_grid_axes):
    if extra_grid_axes:
        return (pl.BlockSpec((tm, d), lambda i, j: (jnp.minimum(i, tp - 1), 0), pipeline_mode=pl.Buffered(1)),
                pl.BlockSpec((tm, d), lambda i, j: (jnp.maximum(i - tp, 0), 0), pipeline_mode=pl.Buffered(1)))
    return (pl.BlockSpec((tm, d), lambda i: (jnp.minimum(i, tp - 1), 0)),
            pl.BlockSpec((tm, d), lambda i: (jnp.maximum(i - tp, 0), 0)))


def _ln_matmul_kernel(xp_ref, xs_ref, g_ref, b_ref, w_ref, o_ref, xn_ref, *, tp):
    i = pl.program_id(0)
    first_col = pl.program_id(1) == 0

    @pl.when(jnp.logical_and(first_col, i < tp))
    def _():
        xn_ref[...] = _layer_norm_f32(xp_ref[...], g_ref[...], b_ref[...]).astype(BF16)

    @pl.when(jnp.logical_and(first_col, i >= tp))
    def _():
        xn_ref[...] = _layer_norm_f32(xs_ref[...], g_ref[...], b_ref[...]).astype(BF16)

    o_ref[...] = jnp.dot(xn_ref[...], w_ref[...], preferred_element_type=F32).astype(o_ref.dtype)


def _ln_matmul(xp, xs, g, b, w, tm, tn):
    d = xp.shape[1]
    n = w.shape[1]
    tp, ts = xp.shape[0] // tm, xs.shape[0] // tm
    xp_spec, xs_spec = _two_group_specs(tm, d, tp, True)
    return pl.pallas_call(
        functools.partial(_ln_matmul_kernel, tp=tp),
        grid=(tp + ts, n // tn),
        in_specs=[
            xp_spec, xs_spec,
            pl.BlockSpec((1, d), lambda i, j: (0, 0)),
            pl.BlockSpec((1, d), lambda i, j: (0, 0)),
            pl.BlockSpec((d, tn), lambda i, j: (0, j)),
        ],
        out_specs=pl.BlockSpec((tm, tn), lambda i, j: (i, j)),
        out_shape=jax.ShapeDtypeStruct(((tp + ts) * tm, n), BF16),
        scratch_shapes=[pltpu.VMEM((tm, d), BF16)],
        compiler_params=_cparams(("parallel", "arbitrary"), 44),
        name="ln_matmul",
    )(xp, xs, g, b, w)


def _diff_attn_kernel(par_ref, q_ref, k_ref, v_ref, km_ref, vm_ref, g_ref, o_ref, kall_ref, vall_ref, bias_ref):
    h = pl.program_id(0)
    b = pl.program_id(1)
    qi = pl.program_id(2)
    tq = q_ref.shape[0]
    n_keys = kall_ref.shape[0]
    shift = bias_ref.shape[1] - n_keys

    @pl.when(jnp.logical_and(b == 0, qi == 0))
    def _():
        i = lax.broadcasted_iota(jnp.int32, (tq, 1), 0)
        j = lax.broadcasted_iota(jnp.int32, (1, bias_ref.shape[1]), 1)
        bias_ref[...] = -par_ref[1 + h] * jnp.abs(i + (META_PAD + shift) - j).astype(F32)

    @pl.when(qi == 0)
    def _():
        kall_ref[0:META_PAD, :] = km_ref[...]
        kall_ref[META_PAD:, :] = k_ref[...]
        vall_ref[0:META_PAD, :] = vm_ref[...]
        vall_ref[META_PAD:, :] = v_ref[...]

    lam = par_ref[0]
    q = q_ref[...]
    lane = lax.broadcasted_iota(jnp.int32, (1, 2 * A_HEAD_DIM), 1)
    scale = jnp.asarray(A_HEAD_DIM ** -0.5, BF16)
    zero = jnp.zeros((), BF16)
    q1 = jnp.where(lane < A_HEAD_DIM, q, zero) * scale
    q2 = jnp.where(lane >= A_HEAD_DIM, q, zero) * scale
    k_all = kall_ref[...]
    nt = (((1,), (1,)), ((), ()))
    s1 = lax.dot_general(q1, k_all, nt, preferred_element_type=F32)
    s2 = lax.dot_general(q2, k_all, nt, preferred_element_type=F32)

    off = pl.multiple_of(shift - qi * tq, LANES)
    lane_m = lax.broadcasted_iota(jnp.int32, (1, META_PAD), 1)
    pad_mask = jnp.where(lane_m < META_PAD - N_META, NEG_BIG, 0.0)
    bias_meta = bias_ref[:, pl.ds(off, META_PAD)] + pad_mask
    bias_seq = bias_ref[:, pl.ds(off + META_PAD, n_keys - META_PAD)]

    def softmax_parts(s):
        s_meta = s[:, 0:META_PAD] + bias_meta
        s_seq = s[:, META_PAD:] + bias_seq
        m = jnp.maximum(jnp.max(s_meta, axis=-1, keepdims=True), jnp.max(s_seq, axis=-1, keepdims=True))
        p_meta = jnp.exp(s_meta - m)
        p_seq = jnp.exp(s_seq - m)
        l = jnp.sum(p_meta, axis=-1, keepdims=True) + jnp.sum(p_seq, axis=-1, keepdims=True)
        return p_meta.astype(BF16), p_seq.astype(BF16), l

    p1_meta, p1_seq, l1 = softmax_parts(s1)
    p2_meta, p2_seq, l2 = softmax_parts(s2)
    a1 = (1.0 / l1).astype(BF16)
    a2 = (lam / l2).astype(BF16)
    w_meta = p1_meta * a1 - p2_meta * a2
    w_seq = p1_seq * a1 - p2_seq * a2
    o = (jnp.dot(w_meta, vall_ref[0:META_PAD, :], preferred_element_type=F32)
         + jnp.dot(w_seq, vall_ref[META_PAD:, :], preferred_element_type=F32))
    ms = jnp.mean(o * o, axis=-1, keepdims=True)
    o_ref[...] = (o * lax.rsqrt(ms + EPS) * g_ref[...]).astype(o_ref.dtype)


def _diff_attn(proj, meta_proj, params, subln_g, n_seq, seq, tq):
    n_tok = proj.shape[0]
    nq = seq // tq
    hw = 2 * A_HEAD_DIM
    return pl.pallas_call(
        _diff_attn_kernel,
        grid=(A_HEADS, n_seq, nq),
        in_specs=[
            pl.BlockSpec(memory_space=pltpu.SMEM),
            pl.BlockSpec((tq, hw), lambda h, b, i: (b * nq + i, h)),
            pl.BlockSpec((seq, hw), lambda h, b, i: (b, A_HEADS + h)),
            pl.BlockSpec((seq, hw), lambda h, b, i: (b, 2 * A_HEADS + h)),
            pl.BlockSpec((META_PAD, hw), lambda h, b, i: (0, A_HEADS + h)),
            pl.BlockSpec((META_PAD, hw), lambda h, b, i: (0, 2 * A_HEADS + h)),
            pl.BlockSpec((1, hw), lambda h, b, i: (0, 0)),
        ],
        out_specs=pl.BlockSpec((tq, hw), lambda h, b, i: (b * nq + i, h)),
        out_shape=jax.ShapeDtypeStruct((n_tok, A_HEADS * hw), BF16),
        scratch_shapes=[pltpu.VMEM((META_PAD + seq, hw), BF16), pltpu.VMEM((META_PAD + seq, hw), BF16),
                        pltpu.VMEM((tq, META_PAD + 2 * seq - tq), F32)],
        compiler_params=_cparams(("arbitrary", "arbitrary", "arbitrary"), 48),
        name="diff_attn",
    )(params, proj, proj, proj, meta_proj, meta_proj, subln_g)


def _split_cumsum(tri, x):
    hi = x.astype(BF16)
    lo = (x - hi.astype(F32)).astype(BF16)
    return (jnp.dot(tri, hi, preferred_element_type=F32) + jnp.dot(tri, lo, preferred_element_type=F32))


def _hgrn_chunk(q, k, logf, v_bf, state_t, reverse):
    c = q.shape[0]
    r = lax.broadcasted_iota(jnp.int32, (c, c), 0)
    s = lax.broadcasted_iota(jnp.int32, (c, c), 1)
    if reverse:
        visible = s >= r
        ref_row, last_row = c - 1 - c // 2, 0
    else:
        visible = s <= r
        ref_row, last_row = c // 2, c - 1
    tri = jnp.where(visible, 1.0, 0.0).astype(BF16)
    b = _split_cumsum(tri, logf)
    b_ref = b[ref_row:ref_row + 1, :]
    b_last = b[last_row:last_row + 1, :]
    nt = (((1,), (1,)), ((), ()))
    tn = (((0,), (0,)), ((), ()))
    qd = (q * jnp.exp(b - b_ref)).astype(BF16)
    kd = (k * jnp.exp(b_ref - b)).astype(BF16)
    scores = lax.dot_general(qd, kd, nt, preferred_element_type=F32)
    scores = jnp.where(visible, scores, 0.0).astype(BF16)
    o_intra = jnp.dot(scores, v_bf, preferred_element_type=F32)
    qe = (q * jnp.exp(b)).astype(BF16)
    o_inter = lax.dot_general(qe, state_t.astype(BF16), nt, preferred_element_type=F32)
    ks = (k * jnp.exp(b_last - b)).astype(BF16)
    d_state_t = lax.dot_general(v_bf, ks, tn, preferred_element_type=F32)
    new_state_t = state_t * jnp.exp(b_last) + d_state_t
    return o_intra + o_inter, new_state_t


def _forget_gate(logit, lb):
    f = lb + (1.0 - lb) * jax.nn.sigmoid(logit)
    return 1.0 - f, jnp.log(f)


def _chunk_sum_operators(reverse):
    g = HGRN_GROUP
    r = lax.broadcasted_iota(jnp.int32, (g, g), 0)
    s = lax.broadcasted_iota(jnp.int32, (g, g), 1)
    shift = H_CHUNK.bit_length() - 1
    same = (r >> shift) == (s >> shift)
    s_loc = s & (H_CHUNK - 1)
    if reverse:
        cum, ref = s >= r, s_loc >= H_CHUNK - 1 - H_CHUNK // 2
    else:
        cum, ref = s <= r, s_loc <= H_CHUNK // 2
    ops = [jnp.logical_and(same, cum), jnp.logical_and(same, ref), same]
    return jnp.concatenate([jnp.where(o, 1.0, 0.0) for o in ops], axis=0).astype(BF16)


def _hgrn_kernel(q_ref, ff_ref, fb_ref, v_ref, hg_ref, mff_ref, mv_ref, lbf_ref, lbb_ref, g_ref, o_ref,
                 qdf_ref, kdf_ref, qdb_ref, kdb_ref, qe_ref, ks_ref, dec_ref, ds_ref, sp_ref, acc_ref):
    seq = q_ref.shape[0]
    n_heads = q_ref.shape[1] // H_DV
    n_chunks = seq // H_CHUNK
    dk = H_EXPAND
    nt = (((1,), (1,)), ((), ()))
    tn = (((0,), (0,)), ((), ()))
    ops_f = _chunk_sum_operators(False)
    ops_b = _chunk_sum_operators(True)
    r = lax.broadcasted_iota(jnp.int32, (2 * H_CHUNK, 2 * H_CHUNK), 0)
    s = lax.broadcasted_iota(jnp.int32, (2 * H_CHUNK, 2 * H_CHUNK), 1)
    same_chunk = (r >= H_CHUNK) == (s >= H_CHUNK)
    vis_f = jnp.logical_and(same_chunk, s <= r)
    vis_b = jnp.logical_and(same_chunk, s >= r)
    mrow = lax.broadcasted_iota(jnp.int32, (H_CHUNK, 1), 0)
    is_meta = mrow >= H_CHUNK - N_META
    zeros_q = jnp.zeros((H_CHUNK, dk), F32)
    state0 = jnp.zeros((H_DV, dk), F32)
    chunks_per_group = HGRN_GROUP // H_CHUNK

    def chunk_rows(x):
        return jnp.concatenate([x[c * H_CHUNK:c * H_CHUNK + 1] for c in range(chunks_per_group)], axis=0)

    def per_chunk(y):
        return jnp.concatenate([jnp.broadcast_to(y[c:c + 1], (H_CHUNK, y.shape[1]))
                                for c in range(chunks_per_group)], axis=0)

    def dec_row(c):
        return pl.ds(pl.multiple_of(c * DEC_ROWS, DEC_ROWS), DEC_ROWS)

    for j in range(n_heads):
        head = pl.ds(j * H_DV, H_DV)
        lbf = lbf_ref[:, head]
        lbb = lbb_ref[:, head]

        def prepare(g, carry):
            rows = pl.ds(pl.multiple_of(g * HGRN_GROUP, HGRN_GROUP), HGRN_GROUP)
            q = jax.nn.silu(q_ref[rows, head].astype(F32))
            for logit_ref, lb, ops, qd_ref, kd_ref, col in ((ff_ref, lbf, ops_f, qdf_ref, kdf_ref, 0),
                                                           (fb_ref, lbb, ops_b, qdb_ref, kdb_ref, dk)):
                k, logf = _forget_gate(logit_ref[rows, head].astype(F32), lb)
                hi = logf.astype(BF16)
                lo = (logf - hi.astype(F32)).astype(BF16)
                sums = jnp.dot(ops, jnp.concatenate([hi, lo], axis=1), preferred_element_type=F32)
                sums = sums[:, 0:dk] + sums[:, dk:]
                b = sums[0:HGRN_GROUP]
                b_mid = sums[HGRN_GROUP:2 * HGRN_GROUP]
                b_all = sums[2 * HGRN_GROUP:]
                e_mid = per_chunk(jnp.exp(chunk_rows(b_mid)))
                e_all = jnp.exp(chunk_rows(b_all))
                e_rest = per_chunk(jnp.exp(chunk_rows(b_all) - chunk_rows(b_mid)))
                qd = q * jnp.exp(b - b_mid)
                kd = k * jnp.exp(b_mid - b)
                qd_ref[rows, :] = qd.astype(BF16)
                kd_ref[rows, :] = kd.astype(BF16)
                qe_ref[rows, col:col + dk] = (qd * e_mid).astype(BF16)
                ks_ref[rows, col:col + dk] = (kd * e_rest).astype(BF16)
                group_rows = chunks_per_group * DEC_ROWS
                dec_ref[pl.ds(pl.multiple_of(g * group_rows, group_rows), group_rows), col:col + dk] = (
                    jnp.concatenate([jnp.broadcast_to(e_all[c:c + 1], (DEC_ROWS, dk))
                                     for c in range(chunks_per_group)], axis=0))
            return carry

        lax.fori_loop(0, seq // HGRN_GROUP, prepare, 0, unroll=2)

        def intra(p, carry):
            rows = pl.ds(pl.multiple_of(p * (2 * H_CHUNK), 2 * H_CHUNK), 2 * H_CHUNK)
            s_f = lax.dot_general(qdf_ref[rows, :], kdf_ref[rows, :], nt, preferred_element_type=F32)
            s_b = lax.dot_general(qdb_ref[rows, :], kdb_ref[rows, :], nt, preferred_element_type=F32)
            w = (jnp.where(vis_f, s_f, 0.0) + jnp.where(vis_b, s_b, 0.0)).astype(BF16)
            acc_ref[rows, head] = jnp.dot(w, v_ref[rows, head], preferred_element_type=F32)
            for half in range(2):
                c = 2 * p + half
                rows_c = pl.ds(pl.multiple_of(c * H_CHUNK, H_CHUNK), H_CHUNK)
                ds_ref[pl.ds(pl.multiple_of(c * H_DV, H_DV), H_DV), :] = lax.dot_general(
                    v_ref[rows_c, head], ks_ref[rows_c, :], tn, preferred_element_type=F32)
            return carry

        lax.fori_loop(0, n_chunks // 2, intra, 0, unroll=HGRN_UNROLL // 2)

        mk, mlogf = _forget_gate(mff_ref[:, head].astype(F32), lbf)
        mk = jnp.where(is_meta, mk, 0.0)
        mlogf = jnp.where(is_meta, mlogf, 0.0)
        state_f0 = _hgrn_chunk(zeros_q, mk, mlogf, mv_ref[:, head], state0, reverse=False)[1]

        def scan(i, carry):
            st_f, st_b = carry
            cf = i
            cb = n_chunks - 1 - i
            blk_f = pl.ds(pl.multiple_of(cf * H_DV, H_DV), H_DV)
            blk_b = pl.ds(pl.multiple_of(cb * H_DV, H_DV), H_DV)
            sp_ref[blk_f, 0:dk] = st_f.astype(BF16)
            sp_ref[blk_b, dk:] = st_b.astype(BF16)
            st_f = st_f * dec_ref[dec_row(cf), 0:dk][0:1] + ds_ref[blk_f, 0:dk]
            st_b = st_b * dec_ref[dec_row(cb), dk:][0:1] + ds_ref[blk_b, dk:]
            return st_f, st_b

        lax.fori_loop(0, n_chunks, scan, (state_f0, state0))

        def inter(c, carry):
            rows = pl.ds(pl.multiple_of(c * H_CHUNK, H_CHUNK), H_CHUNK)
            state = sp_ref[pl.ds(pl.multiple_of(c * H_DV, H_DV), H_DV), :]
            acc_ref[rows, head] += lax.dot_general(qe_ref[rows, :], state, nt, preferred_element_type=F32)
            return carry

        lax.fori_loop(0, n_chunks, inter, 0, unroll=HGRN_UNROLL)

    for j in range(n_heads):
        head = pl.ds(j * H_DV, H_DV)
        o = acc_ref[:, head]
        ms = jnp.mean(o * o, axis=-1, keepdims=True)
        normed = o * lax.rsqrt(ms + EPS) * g_ref[...]
        o_ref[:, head] = (normed * jax.nn.sigmoid(hg_ref[:, head].astype(F32))).astype(o_ref.dtype)


def _hgrn(proj, meta_proj, lb_fwd, lb_bwd, norm_g, n_seq, seq):
    n_tok = proj.shape[0]
    hps = HGRN_HEADS_PER_STEP
    w = hps * H_DV
    col0 = 3 * A_HEADS * (2 * A_HEAD_DIM) // w
    per_sec = H_HEADS // hps
    meta_blk = META_PAD // H_CHUNK - 1

    def sec(k):
        return pl.BlockSpec((seq, w), lambda b, h, k=k: (b, col0 + k * per_sec + h))

    def meta_sec(k):
        return pl.BlockSpec((H_CHUNK, w), lambda b, h, k=k: (meta_blk, col0 + k * per_sec + h))

    return pl.pallas_call(
        _hgrn_kernel,
        grid=(n_seq, per_sec),
        in_specs=[
            sec(0), sec(1), sec(2), sec(3), sec(4),
            meta_sec(1), meta_sec(3),
            pl.BlockSpec((1, w), lambda b, h: (0, h)),
            pl.BlockSpec((1, w), lambda b, h: (0, h)),
            pl.BlockSpec((1, H_DV), lambda b, h: (0, 0)),
        ],
        out_specs=pl.BlockSpec((seq, w), lambda b, h: (b, h)),
        out_shape=jax.ShapeDtypeStruct((n_tok, H_HEADS * H_DV), BF16),
        scratch_shapes=[
            pltpu.VMEM((seq, H_EXPAND), BF16), pltpu.VMEM((seq, H_EXPAND), BF16),
            pltpu.VMEM((seq, H_EXPAND), BF16), pltpu.VMEM((seq, H_EXPAND), BF16),
            pltpu.VMEM((seq, 2 * H_EXPAND), BF16),
            pltpu.VMEM((seq, 2 * H_EXPAND), BF16),
            pltpu.VMEM((seq // H_CHUNK * DEC_ROWS, 2 * H_EXPAND), F32),
            pltpu.VMEM((seq // H_CHUNK * H_DV, 2 * H_EXPAND), F32),
            pltpu.VMEM((seq // H_CHUNK * H_DV, 2 * H_EXPAND), BF16),
            pltpu.VMEM((seq, w), F32),
        ],
        compiler_params=_cparams(("parallel", "parallel"), 48),
        name="hgrn",
    )(proj, proj, proj, proj, proj, meta_proj, meta_proj, lb_fwd, lb_bwd, norm_g)


def _merge_kernel(ao_ref, ho_ref, ga_ref, gh_ref, xp_ref, xs_ref, wa_ref, wh_ref, wo_ref, lng_ref, lnb_ref,
                  l1g_ref, l1b_ref, hf_ref, hp_ref, *, tp):
    pa = jnp.dot(ao_ref[...], wa_ref[...], preferred_element_type=F32)
    ph = jnp.dot(ho_ref[...], wh_ref[...], preferred_element_type=F32)
    merged = (jax.nn.sigmoid(ga_ref[...].astype(F32)) * pa + jax.nn.sigmoid(gh_ref[...].astype(F32)) * ph)
    t = jnp.dot(merged.astype(BF16), wo_ref[...], preferred_element_type=F32)

    def finish(x_ref):
        x_in = _layer_norm_f32(x_ref[...], lng_ref[...], lnb_ref[...])
        h = _layer_norm_f32(DN_ALPHA * x_in + t, l1g_ref[...], l1b_ref[...])
        hf_ref[...] = h
        hp_ref[...] = _pack_bf16_pair(h)

    @pl.when(pl.program_id(0) < tp)
    def _():
        finish(xp_ref)

    @pl.when(pl.program_id(0) >= tp)
    def _():
        finish(xs_ref)


def _merge(ao, ho, proj, xp, xs, wa, wh, wo, ln_g, ln_b, l1g, l1b, tm):
    d = xp.shape[1]
    tp, ts = xp.shape[0] // tm, xs.shape[0] // tm
    n_tok = (tp + ts) * tm
    gate_blk0 = (proj.shape[1] - 2 * d) // d
    vec = pl.BlockSpec((1, d), lambda i: (0, 0))
    xp_spec, xs_spec = _two_group_specs(tm, d, tp, False)
    return pl.pallas_call(
        functools.partial(_merge_kernel, tp=tp),
        grid=(tp + ts,),
        in_specs=[
            pl.BlockSpec((tm, ao.shape[1]), lambda i: (i, 0)),
            pl.BlockSpec((tm, ho.shape[1]), lambda i: (i, 0)),
            pl.BlockSpec((tm, d), lambda i: (i, gate_blk0)),
            pl.BlockSpec((tm, d), lambda i: (i, gate_blk0 + 1)),
            xp_spec, xs_spec,
            _resident(wa.shape), _resident(wh.shape), _resident(wo.shape),
            vec, vec, vec, vec,
        ],
        out_specs=[pl.BlockSpec((tm, d), lambda i: (i, 0)), pl.BlockSpec((tm, d // 2), lambda i: (i, 0))],
        out_shape=[jax.ShapeDtypeStruct((n_tok, d), F32), jax.ShapeDtypeStruct((n_tok, d // 2), U32)],
        compiler_params=_cparams(("parallel",), 56),
        name="merge",
    )(ao, ho, proj, proj, xp, xs, wa, wh, wo, ln_g, ln_b, l1g, l1b)


def _first_argmax(x, iota, size):
    m = jnp.max(x, axis=0, keepdims=True)
    idx = jnp.min(jnp.where(x == m, iota, size), axis=0, keepdims=True)
    return m, idx


def _router_kernel(h_ref, wrt_ref, bias_ref, idx_ref, gate_ref, rank_ref, cnt_ref, base_ref):
    tm = h_ref.shape[0]
    half = h_ref.shape[1]

    @pl.when(pl.program_id(0) == 0)
    def _():
        base_ref[...] = jnp.zeros_like(base_ref)

    nt = (((1,), (1,)), ((), ()))
    h_lo, h_hi = _unpack_bf16_pair(h_ref[...])
    logits = (lax.dot_general(wrt_ref[:, 0:half], h_lo, nt, preferred_element_type=F32)
              + lax.dot_general(wrt_ref[:, half:], h_hi, nt, preferred_element_type=F32))
    scores = jax.nn.sigmoid(logits)
    biased = scores + bias_ref[...]
    neg_inf = jnp.asarray(-jnp.inf, F32)

    iota_g = lax.broadcasted_iota(jnp.int32, (GROUP_SIZE, tm), 0)
    iota_n = lax.broadcasted_iota(jnp.int32, (N_GROUPS, tm), 0)
    grp = jnp.full((N_GROUPS, tm), neg_inf, F32)
    for g in range(N_GROUPS):
        xg = biased[g * GROUP_SIZE:(g + 1) * GROUP_SIZE, :]
        m1, i1 = _first_argmax(xg, iota_g, GROUP_SIZE)
        m2 = jnp.max(jnp.where(iota_g == i1, neg_inf, xg), axis=0, keepdims=True)
        grp = jnp.where(iota_n == g, m1 + m2, grp)

    keep_f = jnp.zeros((N_GROUPS, tm), F32)
    for _ in range(TOPK_GROUPS):
        _, ig = _first_argmax(grp, iota_n, N_GROUPS)
        sel = iota_n == ig
        keep_f = jnp.where(sel, 1.0, keep_f)
        grp = jnp.where(sel, neg_inf, grp)

    masked = jnp.concatenate(
        [jnp.where(keep_f[g:g + 1, :] > 0.5, biased[g * GROUP_SIZE:(g + 1) * GROUP_SIZE, :], neg_inf)
         for g in range(N_GROUPS)], axis=0)

    iota_e = lax.broadcasted_iota(jnp.int32, (N_EXPERTS, tm), 0)
    chosen = jnp.zeros((N_EXPERTS, tm), F32)
    idxs, gates = [], []
    for _ in range(TOP_K):
        _, ie = _first_argmax(masked, iota_e, N_EXPERTS)
        sel = iota_e == ie
        gates.append(jnp.sum(jnp.where(sel, scores, 0.0), axis=0, keepdims=True))
        idxs.append(ie)
        chosen = jnp.where(sel, 1.0, chosen)
        masked = jnp.where(sel, neg_inf, masked)

    denom = gates[0]
    for gk in gates[1:]:
        denom = denom + gk

    r = lax.broadcasted_iota(jnp.int32, (tm, tm), 0)
    c = lax.broadcasted_iota(jnp.int32, (tm, tm), 1)
    before = jnp.where(r < c, 1.0, 0.0).astype(BF16)
    pos = jnp.dot(chosen.astype(BF16), before, preferred_element_type=F32) + base_ref[...]
    for k in range(TOP_K):
        sel = iota_e == idxs[k]
        rank_ref[k:k + 1, :] = jnp.sum(jnp.where(sel, pos, 0.0), axis=0, keepdims=True).astype(jnp.int32)
        idx_ref[k:k + 1, :] = idxs[k]
        gate_ref[k:k + 1, :] = gates[k] / denom * ROUTED_SCALE

    base_ref[...] = base_ref[...] + jnp.sum(chosen, axis=1, keepdims=True)
    cnt_ref[...] = base_ref[...]


def _router(h_pk, w_router_t, bias_col, tm):
    n_tok, half = h_pk.shape
    row_blk = pl.BlockSpec((TOP_K, tm), lambda i: (0, i))
    return pl.pallas_call(
        _router_kernel,
        grid=(n_tok // tm,),
        in_specs=[
            pl.BlockSpec((tm, half), lambda i: (i, 0)),
            _resident(w_router_t.shape),
            pl.BlockSpec((N_EXPERTS, 1), lambda i: (0, 0)),
        ],
        out_specs=[row_blk, row_blk, row_blk, pl.BlockSpec((N_EXPERTS, 1), lambda i: (0, 0))],
        out_shape=[
            jax.ShapeDtypeStruct((TOP_K, n_tok), jnp.int32),
            jax.ShapeDtypeStruct((TOP_K, n_tok), F32),
            jax.ShapeDtypeStruct((TOP_K, n_tok), jnp.int32),
            jax.ShapeDtypeStruct((N_EXPERTS, 1), F32),
        ],
        scratch_shapes=[pltpu.VMEM((N_EXPERTS, 1), F32)],
        compiler_params=_cparams(("arbitrary",), 32),
        name="router",
    )(h_pk, w_router_t, bias_col)


def _dest_kernel(idx_ref, rank_ref, start_ref, dest_ref):
    tm = idx_ref.shape[1]
    iota_e = lax.broadcasted_iota(jnp.int32, (N_EXPERTS, tm), 0)
    start = start_ref[...]
    for k in range(TOP_K):
        sel = iota_e == idx_ref[k:k + 1, :]
        base = jnp.sum(jnp.where(sel, start, 0.0), axis=0, keepdims=True)
        dest_ref[k:k + 1, :] = base.astype(jnp.int32) + rank_ref[k:k + 1, :]


def _dest(idx, rank, start_col, tm):
    n_tok = idx.shape[1]
    row_blk = pl.BlockSpec((TOP_K, tm), lambda i: (0, i))
    return pl.pallas_call(
        _dest_kernel,
        grid=(n_tok // tm,),
        in_specs=[row_blk, row_blk, pl.BlockSpec((N_EXPERTS, 1), lambda i: (0, 0))],
        out_specs=row_blk,
        out_shape=jax.ShapeDtypeStruct((TOP_K, n_tok), jnp.int32),
        compiler_params=_cparams(("parallel",), 32),
        name="dest",
    )(idx, rank, start_col)


def _dispatch_kernel(pstart_ref, cnt_ref, tail_ref, dest_ref, h_ref, xs_ref, zero_ref, sem, pad_sem, tail_sem):
    tm = h_ref.shape[0]

    def pad_copy(e, j):
        return pltpu.make_async_copy(zero_ref.at[pl.ds(0, 1)],
                                     xs_ref.at[pl.ds(pstart_ref[e] + cnt_ref[e] + j, 1)], pad_sem)

    def tail_copy():
        return pltpu.make_async_copy(
            zero_ref, xs_ref.at[pl.ds(pl.multiple_of(tail_ref[0], ROW_ALIGN), EXPERT_ROWS)], tail_sem)

    def for_each_pad_row(action):
        def body(e, carry):
            n_pad = (ROW_ALIGN - cnt_ref[e] % ROW_ALIGN) % ROW_ALIGN
            for j in range(ROW_ALIGN - 1):
                @pl.when(j < n_pad)
                def _():
                    action(pad_copy(e, j))
            return carry
        lax.fori_loop(0, N_EXPERTS, body, 0)

    @pl.when(pl.program_id(0) == 0)
    def _():
        zero_ref[...] = jnp.zeros_like(zero_ref)
        tail_copy().start()
        for_each_pad_row(lambda c: c.start())
        for_each_pad_row(lambda c: c.wait())
        tail_copy().wait()

    def row_copy(r, k):
        return pltpu.make_async_copy(h_ref.at[pl.ds(r, 1)], xs_ref.at[pl.ds(dest_ref[0, k * tm + r], 1)], sem)

    def start(r, carry):
        for k in range(TOP_K):
            row_copy(r, k).start()
        return carry

    def wait(r, carry):
        for k in range(TOP_K):
            row_copy(r, k).wait()
        return carry

    lax.fori_loop(0, tm, start, 0)
    lax.fori_loop(0, tm, wait, 0)


def _dispatch(h_pk, dest_tiles, pstart, counts, tail, n_rows, tm):
    n_tok, half = h_pk.shape
    grid_spec = pltpu.PrefetchScalarGridSpec(
        num_scalar_prefetch=3,
        grid=(n_tok // tm,),
        in_specs=[
            pl.BlockSpec((None, 1, TOP_K * tm), lambda i, *_: (i, 0, 0), memory_space=pltpu.SMEM),
            pl.BlockSpec((tm, half), lambda i, *_: (i, 0)),
        ],
        out_specs=pl.BlockSpec(memory_space=pl.ANY),
        scratch_shapes=[pltpu.VMEM((EXPERT_ROWS, half), U32), pltpu.SemaphoreType.DMA,
                        pltpu.SemaphoreType.DMA, pltpu.SemaphoreType.DMA],
    )
    return pl.pallas_call(
        _dispatch_kernel,
        grid_spec=grid_spec,
        out_shape=jax.ShapeDtypeStruct((n_rows, half), U32),
        compiler_params=_cparams(("arbitrary",), 32, disable_bounds_checks=True),
        name="dispatch",
    )(pstart, counts, tail, dest_tiles, h_pk)


def _experts_kernel(row_ref, e_ref, valid_ref, newe_ref, slot_ref, nexte_ref,
                    x_hbm, wg_hbm, wu_hbm, wd_hbm, y_hbm,
                    x_buf, y_buf, wg_buf, wu_buf, wd_buf, wgu_bf, wd_bf, x_sems, y_sems, sems):
    i = pl.program_id(0)
    n_steps = pl.num_programs(0)
    ff = wg_buf.shape[2]
    cur = i % 2

    def window(step):
        return pl.ds(pl.multiple_of(row_ref[step], ROW_ALIGN), EXPERT_ROWS)

    def x_copy(step, slot):
        return pltpu.make_async_copy(x_hbm.at[window(step)], x_buf.at[slot], x_sems.at[slot])

    def y_copy(step, slot):
        return pltpu.make_async_copy(y_buf.at[slot], y_hbm.at[window(step)], y_sems.at[slot])

    @pl.when(i == 0)
    def _():
        x_copy(0, 0).start()

    def weight_copies(e, slot):
        copies = []
        for m, (src, dst) in enumerate(((wg_hbm, wg_buf), (wu_hbm, wu_buf), (wd_hbm, wd_buf))):
            rows = dst.shape[1] // WEIGHT_DMA_CHUNKS
            for c in range(WEIGHT_DMA_CHUNKS):
                part = pl.ds(c * rows, rows)
                copies.append(pltpu.make_async_copy(src.at[0, e, part], dst.at[slot, part],
                                                    sems.at[slot, m * WEIGHT_DMA_CHUNKS + c]))
        return copies

    @pl.when(newe_ref[i] == 1)
    def _():
        slot = slot_ref[i]
        e = e_ref[i]

        @pl.when(i == 0)
        def _():
            for c in weight_copies(e, slot):
                c.start()

        for c in weight_copies(e, slot):
            c.wait()
        nxt = nexte_ref[i]

        @pl.when(nxt >= 0)
        def _():
            for c in weight_copies(nxt, 1 - slot):
                c.start()

        wgu_bf[:, 0:ff] = wg_buf[slot].astype(BF16)
        wgu_bf[:, ff:] = wu_buf[slot].astype(BF16)
        wd_bf[...] = wd_buf[slot].astype(BF16)

    @pl.when(valid_ref[i] == 1)
    def _():
        nxt_step = jnp.minimum(i + 1, n_steps - 1)
        has_next = jnp.logical_and(i + 1 < n_steps, valid_ref[nxt_step] == 1)
        x_copy(i, cur).wait()

        @pl.when(has_next)
        def _():
            x_copy(nxt_step, 1 - cur).start()

        gu = _dot_packed(x_buf[cur], wgu_bf)
        hid = (jax.nn.silu(gu[:, 0:ff]) * gu[:, ff:]).astype(BF16)
        y_buf[cur] = _pack_bf16_pair(jnp.dot(hid, wd_bf[...], preferred_element_type=F32))

        @pl.when(i > 0)
        def _():
            y_copy(i - 1, 1 - cur).wait()

        y_copy(i, cur).start()

        @pl.when(jnp.logical_not(has_next))
        def _():
            y_copy(i, cur).wait()


def _experts(x_sorted, steps, w_gate, w_up, w_down):
    n_rows, half = x_sorted.shape
    d = 2 * half
    ff = w_gate.shape[-1]
    hbm = pl.BlockSpec(memory_space=pl.ANY)
    grid_spec = pltpu.PrefetchScalarGridSpec(
        num_scalar_prefetch=len(steps),
        grid=(steps[0].shape[0],),
        in_specs=[hbm, hbm, hbm, hbm],
        out_specs=hbm,
        scratch_shapes=[pltpu.VMEM((2, EXPERT_ROWS, half), U32), pltpu.VMEM((2, EXPERT_ROWS, half), U32),
                        pltpu.VMEM((2, d, ff), F32), pltpu.VMEM((2, d, ff), F32), pltpu.VMEM((2, ff, d), F32),
                        pltpu.VMEM((d, 2 * ff), BF16), pltpu.VMEM((ff, d), BF16),
                        pltpu.SemaphoreType.DMA((2,)), pltpu.SemaphoreType.DMA((2,)),
                        pltpu.SemaphoreType.DMA((2, 3 * WEIGHT_DMA_CHUNKS))],
    )
    return pl.pallas_call(
        _experts_kernel,
        grid_spec=grid_spec,
        out_shape=jax.ShapeDtypeStruct((n_rows, half), U32),
        compiler_params=_cparams(("arbitrary",), 52, disable_bounds_checks=True),
        name="experts",
    )(*steps, x_sorted, w_gate, w_up, w_down)


def _final_kernel(dest_ref, gate_ref, hf_ref, hp_ref, y_ref, wgu_ref, wd_ref, g_ref, b_ref, op_ref, os_ref,
                  ybuf, sem, *, tp):
    tm = hf_ref.shape[0]
    ff = wd_ref.shape[0]

    def row_copy(r, k):
        return pltpu.make_async_copy(y_ref.at[pl.ds(dest_ref[0, k * tm + r], 1)], ybuf.at[k, pl.ds(r, 1)], sem)

    def start(r, carry):
        for k in range(TOP_K):
            row_copy(r, k).start()
        return carry

    def wait(r, carry):
        for k in range(TOP_K):
            row_copy(r, k).wait()
        return carry

    lax.fori_loop(0, tm, start, 0)

    gu = _dot_packed(hp_ref[...], wgu_ref)
    hid = (jax.nn.silu(gu[:, 0:ff]) * gu[:, ff:]).astype(BF16)
    acc = DN_ALPHA * hf_ref[...] + jnp.dot(hid, wd_ref[...], preferred_element_type=F32)

    lax.fori_loop(0, tm, wait, 0)
    gate = gate_ref[...]
    half = ybuf.shape[2]
    acc_lo, acc_hi = acc[:, :half], acc[:, half:]
    for k in range(TOP_K):
        packed = ybuf[k]
        g_k = gate[:, k:k + 1]
        acc_lo = acc_lo + lax.bitcast_convert_type(packed << 16, F32) * g_k
        acc_hi = acc_hi + lax.bitcast_convert_type(packed & jnp.uint32(HIGH_HALF), F32) * g_k
    out = _layer_norm_f32(jnp.concatenate([acc_lo, acc_hi], axis=1), g_ref[...], b_ref[...])

    @pl.when(pl.program_id(0) < tp)
    def _():
        op_ref[...] = out

    @pl.when(pl.program_id(0) >= tp)
    def _():
        os_ref[...] = out


def _final(h_f32, h_pk, y_sorted, dest_tiles, gate_t, w_gu, w_d, g, b, n_prompt, tm):
    n_tok, d = h_f32.shape
    tp = n_prompt // tm
    ts = n_tok // tm - tp
    vec = pl.BlockSpec((1, d), lambda i: (0, 0))
    op_spec, os_spec = _two_group_specs(tm, d, tp, False)
    return pl.pallas_call(
        functools.partial(_final_kernel, tp=tp),
        grid=(tp + ts,),
        in_specs=[
            pl.BlockSpec((None, 1, TOP_K * tm), lambda i: (i, 0, 0), memory_space=pltpu.SMEM),
            pl.BlockSpec((tm, TOP_K), lambda i: (i, 0)),
            pl.BlockSpec((tm, d), lambda i: (i, 0)),
            pl.BlockSpec((tm, d // 2), lambda i: (i, 0)),
            pl.BlockSpec(memory_space=pl.ANY),
            _resident(w_gu.shape), _resident(w_d.shape), vec, vec,
        ],
        out_specs=[op_spec, os_spec],
        out_shape=[jax.ShapeDtypeStruct((tp * tm, d), F32), jax.ShapeDtypeStruct((ts * tm, d), F32)],
        scratch_shapes=[pltpu.VMEM((TOP_K, tm, d // 2), U32), pltpu.SemaphoreType.DMA],
        compiler_params=_cparams(("arbitrary",), 48, disable_bounds_checks=True),
        name="final",
    )(dest_tiles, gate_t, h_f32, h_pk, y_sorted, w_gu, w_d, g, b)


def _lower_bounds(gamma):
    return jnp.cumsum(jax.nn.softmax(gamma.astype(F32), axis=0), axis=0)


def _tile_major(dest, tm):
    k, n = dest.shape
    return jnp.transpose(dest.reshape(k, n // tm, tm), (1, 0, 2)).reshape(n // tm, 1, k * tm)


def _expert_steps(counts, n_assign):
    rows = EXPERT_ROWS
    n_steps = (n_assign + N_EXPERTS * (ROW_ALIGN - 1)) // rows + N_EXPERTS
    padded =(counts + ROW_ALIGN - 1) // ROW_ALIGN * ROW_ALIGN
    pend = jnp.cumsum(padded)
    pstart = pend - padded
    per_e = (padded + rows - 1) // rows
    st_end = jnp.cumsum(per_e)
    st_start = st_end - per_e
    total = st_end[-1]
    i = jnp.arange(n_steps, dtype=jnp.int32)
    ic = jnp.minimum(i, total - 1)
    e = jnp.sum((st_end[None, :] <= ic[:, None]).astype(jnp.int32), axis=1)
    j = ic - st_start[e]
    row0 = pstart[e] + jnp.minimum(j * rows, jnp.maximum(padded[e] - rows, 0))
    valid = i < total
    prev_e = jnp.concatenate([jnp.full((1,), -1, jnp.int32), e[:-1].astype(jnp.int32)])
    new_e = jnp.where(jnp.logical_and(valid, e != prev_e), 1, 0)
    nonempty = counts > 0
    slot = (jnp.cumsum(nonempty.astype(jnp.int32)) - 1) % 2
    ids = jnp.arange(N_EXPERTS, dtype=jnp.int32)
    later = jnp.concatenate([jnp.where(nonempty, ids, N_EXPERTS)[1:], jnp.full((1,), N_EXPERTS, jnp.int32)])
    nxt = jnp.flip(lax.cummin(jnp.flip(later)))
    nxt = jnp.where(nxt < N_EXPERTS, nxt, -1)
    as_i32 = lambda v: v.astype(jnp.int32)
    steps = (as_i32(row0), as_i32(e), as_i32(valid), as_i32(new_e), as_i32(slot[e]), as_i32(nxt[e]))
    n_rows = n_assign + N_EXPERTS * (ROW_ALIGN - 1) + rows
    return as_i32(pstart), as_i32(pend[-1:]), n_rows, steps


def kernel(x_prompt, x_sample, meta_tokens, ln_in_g, ln_in_b, hg_gamma_fwd, hg_gamma_bwd, w_in, lambda_q1, lambda_k1, lambda_q2, lambda_k2, attn_subln_g, hg_norm_g, w_branch_attn, w_branch_hgrn, w_out, ln1_g, ln1_b, w_router, router_bias, w_sh_gate, w_sh_up, w_sh_down, w_ex_gate, w_ex_up, w_ex_down, ln2_g, ln2_b):
    bp, seq, d = x_prompt.shape
    bs, seq_s, _ = x_sample.shape
    assert seq == seq_s and w_in.shape[0] == DEPTH
    n_seq = bp + bs
    n_tok = n_seq * seq
    layer = 0

    xp = x_prompt.reshape(bp * seq, d)
    xs = x_sample.reshape(bs * seq, d)
    row = lambda v: v.reshape(1, -1).astype(F32)
    ln_g, ln_b = row(ln_in_g), row(ln_in_b)
    w_in_bf = w_in[layer].astype(BF16)

    proj = _ln_matmul(xp, xs, ln_g, ln_b, w_in_bf, tm=min(1024, bs * seq), tn=1024)
    meta = meta_tokens.astype(F32)
    meta_proj = _ln_matmul(meta, meta, ln_g, ln_b, w_in_bf, tm=N_META, tn=1024)[:N_META]
    meta_proj = jnp.pad(meta_proj, ((META_PAD - N_META, 0), (0, 0)))

    lam = (jnp.exp(jnp.sum(lambda_q1[layer].astype(F32) * lambda_k1[layer].astype(F32)))
           - jnp.exp(jnp.sum(lambda_q2[layer].astype(F32) * lambda_k2[layer].astype(F32))) + LAMBDA_INIT)
    slopes = 2.0 ** (-8.0 * jnp.arange(1, A_HEADS + 1, dtype=F32) / A_HEADS)
    attn_params = jnp.concatenate([lam.reshape(1), slopes]).astype(F32)
    subln_g = row(attn_subln_g[layer]) * (1.0 - LAMBDA_INIT)
    ao = _diff_attn(proj, meta_proj, attn_params, subln_g, n_seq, seq, tq=min(256, seq))

    lb_fwd = row(_lower_bounds(hg_gamma_fwd)[layer])
    lb_bwd = row(_lower_bounds(hg_gamma_bwd)[layer])
    ho = _hgrn(proj, meta_proj, lb_fwd, lb_bwd, row(hg_norm_g[layer]), n_seq, seq)

    h_f32, h_pk = _merge(ao, ho, proj, xp, xs, w_branch_attn[layer].astype(BF16),
                         w_branch_hgrn[layer].astype(BF16), w_out[layer].astype(BF16), ln_g, ln_b,
                         row(ln1_g[layer]), row(ln1_b[layer]), tm=min(256, bs * seq))

    idx, gate, rank, counts = _router(h_pk, w_router[layer].T.astype(BF16),
                                      router_bias[layer].astype(F32).reshape(N_EXPERTS, 1), tm=min(512, n_tok))
    counts = counts[:, 0].astype(jnp.int32)
    pstart, tail, n_rows, steps = _expert_steps(counts, n_tok * TOP_K)
    dest = _dest(idx, rank, pstart.astype(F32).reshape(N_EXPERTS, 1), tm=min(2048, n_tok))

    tm_d = min(256, n_tok)
    x_sorted = _dispatch(h_pk, _tile_major(dest, tm_d), pstart, counts, tail, n_rows, tm_d)
    y_sorted = _experts(x_sorted, steps, w_ex_gate, w_ex_up, w_ex_down)

    w_sh_gu = jnp.concatenate([w_sh_gate[layer], w_sh_up[layer]], axis=1).astype(BF16)
    tm_f = min(256, bs * seq)
    y_prompt, y_sample = _final(h_f32, h_pk, y_sorted, _tile_major(dest, tm_f), gate.T, w_sh_gu,
                                w_sh_down[layer].astype(BF16), row(ln2_g[layer]), row(ln2_b[layer]),
                                bp * seq, tm_f)
    return (y_prompt.reshape(bp, seq, d), y_sample.reshape(bs, seq, d))
```

```python
import functools
import math

import jax
import jax.numpy as jnp
from jax import lax
from jax.experimental import pallas as pl
from jax.experimental.pallas import tpu as pltpu

F32 = jnp.float32
BF16 = jnp.bfloat16
U32 = jnp.uint32

N_META = 16
A_HEADS = 8
A_HEAD_DIM = 64
H_HEADS = 8
H_EXPAND = 128
H_DV = 128
H_CHUNK = 64
N_EXPERTS = 256
TOP_K = 8
N_GROUPS = 8
TOPK_GROUPS = 4
GROUP_SIZE = N_EXPERTS // N_GROUPS
ROUTED_SCALE = 2.5
DEPTH = 1
DN_ALPHA = (2 * DEPTH) ** 0.25
EPS = 1e-5
LAMBDA_INIT = 0.8 - 0.6 * math.exp(-0.3 * 0)

LANES = 128
HGRN_HEADS_PER_STEP = 4
HGRN_GROUP = 256
DEC_ROWS = 8
HGRN_UNROLL = 8
META_PAD = 128
NEG_BIG = -1e30
EXPERT_ROWS = 256
WEIGHT_DMA_CHUNKS = 4
DMA_THREADS = 2
WEIGHT_DMA_THREAD = 1
ROW_ALIGN = 8
HIGH_HALF = 0xFFFF0000


def _cparams(semantics, vmem_mb, **kw):
    return pltpu.CompilerParams(dimension_semantics=semantics, vmem_limit_bytes=vmem_mb * 1024 * 1024, **kw)


def _layer_norm_f32(x, g, b):
    mu = jnp.mean(x, axis=-1, keepdims=True)
    xc = x - mu
    var = jnp.mean(xc * xc, axis=-1, keepdims=True)
    return xc * lax.rsqrt(var + EPS) * g + b


def _resident(shape):
    nd = len(shape)
    return pl.BlockSpec(shape, lambda *_: (0,) * nd, pipeline_mode=pl.Buffered(1))


def _pack_bf16_pair(x):
    k = x.shape[1] // 2
    lo = lax.bitcast_convert_type(x[:, :k].astype(BF16).astype(F32), U32) >> 16
    hi = lax.bitcast_convert_type(x[:, k:].astype(BF16).astype(F32), U32) & jnp.uint32(HIGH_HALF)
    return lo | hi


def _unpack_bf16_pair(p):
    lo = lax.bitcast_convert_type(p << 16, F32).astype(BF16)
    hi = lax.bitcast_convert_type(p & jnp.uint32(HIGH_HALF), F32).astype(BF16)
    return lo, hi


def _dot_packed(p, w_ref):
    k = p.shape[1]
    lo, hi = _unpack_bf16_pair(p)
    return (jnp.dot(lo, w_ref[0:k, :], preferred_element_type=F32)
            + jnp.dot(hi, w_ref[k:, :], preferred_element_type=F32))


def _two_group_specs(tm, d, tp, extra_grid_axes):
    if extra_grid_axes:
        return (pl.BlockSpec((tm, d), lambda i, j: (jnp.minimum(i, tp - 1), 0), pipeline_mode=pl.Buffered(1)),
                pl.BlockSpec((tm, d), lambda i, j: (jnp.maximum(i - tp, 0), 0), pipeline_mode=pl.Buffered(1)))
    return (pl.BlockSpec((tm, d), lambda i: (jnp.minimum(i, tp - 1), 0)),
            pl.BlockSpec((tm, d), lambda i: (jnp.maximum(i - tp, 0), 0)))


def _ln_matmul_kernel(xp_ref, xs_ref, g_ref, b_ref, w_ref, o_ref, xn_ref, *, tp):
    i = pl.program_id(0)
    first_col = pl.program_id(1) == 0

    @pl.when(jnp.logical_and(first_col, i < tp))
    def _():
        xn_ref[...] = _layer_norm_f32(xp_ref[...], g_ref[...], b_ref[...]).astype(BF16)

    @pl.when(jnp.logical_and(first_col, i >= tp))
    def _():
        xn_ref[...] = _layer_norm_f32(xs_ref[...], g_ref[...], b_ref[...]).astype(BF16)

    o_ref[...] = jnp.dot(xn_ref[...], w_ref[...], preferred_element_type=F32).astype(o_ref.dtype)


def _ln_matmul(xp, xs, g, b, w, tm, tn):
    d = xp.shape[1]
    n = w.shape[1]
    tp, ts = xp.shape[0] // tm, xs.shape[0] // tm
    xp_spec, xs_spec = _two_group_specs(tm, d, tp, True)
    return pl.pallas_call(
        functools.partial(_ln_matmul_kernel, tp=tp),
        grid=(tp + ts, n // tn),
        in_specs=[
            xp_spec, xs_spec,
            pl.BlockSpec((1, d), lambda i, j: (0, 0)),
            pl.BlockSpec((1, d), lambda i, j: (0, 0)),
            pl.BlockSpec((d, tn), lambda i, j: (0, j)),
        ],
        out_specs=pl.BlockSpec((tm, tn), lambda i, j: (i, j)),
        out_shape=jax.ShapeDtypeStruct(((tp + ts) * tm, n), BF16),
        scratch_shapes=[pltpu.VMEM((tm, d), BF16)],
        compiler_params=_cparams(("parallel", "arbitrary"), 44),
        name="ln_matmul",
    )(xp, xs, g, b, w)


def _diff_attn_kernel(par_ref, q_ref, k_ref, v_ref, km_ref, vm_ref, g_ref, o_ref, kall_ref, vall_ref, bias_ref):
    h = pl.program_id(0)
    b = pl.program_id(1)
    qi = pl.program_id(2)
    tq = q_ref.shape[0]
    n_keys = kall_ref.shape[0]
    shift = bias_ref.shape[1] - n_keys

    @pl.when(jnp.logical_and(b == 0, qi == 0))
    def _():
        i = lax.broadcasted_iota(jnp.int32, (tq, 1), 0)
        j = lax.broadcasted_iota(jnp.int32, (1, bias_ref.shape[1]), 1)
        bias_ref[...] = -par_ref[1 + h] * jnp.abs(i + (META_PAD + shift) - j).astype(F32)

    @pl.when(qi == 0)
    def _():
        kall_ref[0:META_PAD, :] = km_ref[...]
        kall_ref[META_PAD:, :] = k_ref[...]
        vall_ref[0:META_PAD, :] = vm_ref[...]
        vall_ref[META_PAD:, :] = v_ref[...]

    lam = par_ref[0]
    q = q_ref[...]
    lane = lax.broadcasted_iota(jnp.int32, (1, 2 * A_HEAD_DIM), 1)
    scale = jnp.asarray(A_HEAD_DIM ** -0.5, BF16)
    zero = jnp.zeros((), BF16)
    q1 = jnp.where(lane < A_HEAD_DIM, q, zero) * scale
    q2 = jnp.where(lane >= A_HEAD_DIM, q, zero) * scale
    k_all = kall_ref[...]
    nt = (((1,), (1,)), ((), ()))
    s1 = lax.dot_general(q1, k_all, nt, preferred_element_type=F32)
    s2 = lax.dot_general(q2, k_all, nt, preferred_element_type=F32)

    off = pl.multiple_of(shift - qi * tq, LANES)
    lane_m = lax.broadcasted_iota(jnp.int32, (1, META_PAD), 1)
    pad_mask = jnp.where(lane_m < META_PAD - N_META, NEG_BIG, 0.0)
    bias_meta = bias_ref[:, pl.ds(off, META_PAD)] + pad_mask
    bias_seq = bias_ref[:, pl.ds(off + META_PAD, n_keys - META_PAD)]

    def softmax_parts(s):
        s_meta = s[:, 0:META_PAD] + bias_meta
        s_seq = s[:, META_PAD:] + bias_seq
        m = jnp.maximum(jnp.max(s_meta, axis=-1, keepdims=True), jnp.max(s_seq, axis=-1, keepdims=True))
        p_meta = jnp.exp(s_meta - m)
        p_seq = jnp.exp(s_seq - m)
        l = jnp.sum(p_meta, axis=-1, keepdims=True) + jnp.sum(p_seq, axis=-1, keepdims=True)
        return p_meta.astype(BF16), p_seq.astype(BF16), l

    p1_meta, p1_seq, l1 = softmax_parts(s1)
    p2_meta, p2_seq, l2 = softmax_parts(s2)
    a1 = (1.0 / l1).astype(BF16)
    a2 = (lam / l2).astype(BF16)
    w_meta = p1_meta * a1 - p2_meta * a2
    w_seq = p1_seq * a1 - p2_seq * a2
    o = (jnp.dot(w_meta, vall_ref[0:META_PAD, :], preferred_element_type=F32)
         + jnp.dot(w_seq, vall_ref[META_PAD:, :], preferred_element_type=F32))
    ms = jnp.mean(o * o, axis=-1, keepdims=True)
    o_ref[...] = (o * lax.rsqrt(ms + EPS) * g_ref[...]).astype(o_ref.dtype)


def _diff_attn(proj, meta_proj, params, subln_g, n_seq, seq, tq):
    n_tok = proj.shape[0]
    nq = seq // tq
    hw = 2 * A_HEAD_DIM
    return pl.pallas_call(
        _diff_attn_kernel,
        grid=(A_HEADS, n_seq, nq),
        in_specs=[
            pl.BlockSpec(memory_space=pltpu.SMEM),
            pl.BlockSpec((tq, hw), lambda h, b, i: (b * nq + i, h)),
            pl.BlockSpec((seq, hw), lambda h, b, i: (b, A_HEADS + h)),
            pl.BlockSpec((seq, hw), lambda h, b, i: (b, 2 * A_HEADS + h)),
            pl.BlockSpec((META_PAD, hw), lambda h, b, i: (0, A_HEADS + h)),
            pl.BlockSpec((META_PAD, hw), lambda h, b, i: (0, 2 * A_HEADS + h)),
            pl.BlockSpec((1, hw), lambda h, b, i: (0, 0)),
        ],
        out_specs=pl.BlockSpec((tq, hw), lambda h, b, i: (b * nq + i, h)),
        out_shape=jax.ShapeDtypeStruct((n_tok, A_HEADS * hw), BF16),
        scratch_shapes=[pltpu.VMEM((META_PAD + seq, hw), BF16), pltpu.VMEM((META_PAD + seq, hw), BF16),
                        pltpu.VMEM((tq, META_PAD + 2 * seq - tq), F32)],
        compiler_params=_cparams(("arbitrary", "arbitrary", "arbitrary"), 48),
        name="diff_attn",
    )(params, proj, proj, proj, meta_proj, meta_proj, subln_g)


def _split_cumsum(tri, x):
    hi = x.astype(BF16)
    lo = (x - hi.astype(F32)).astype(BF16)
    return (jnp.dot(tri, hi, preferred_element_type=F32) + jnp.dot(tri, lo, preferred_element_type=F32))


def _hgrn_chunk(q, k, logf, v_bf, state_t, reverse):
    c = q.shape[0]
    r = lax.broadcasted_iota(jnp.int32, (c, c), 0)
    s = lax.broadcasted_iota(jnp.int32, (c, c), 1)
    if reverse:
        visible = s >= r
        ref_row, last_row = c - 1 - c // 2, 0
    else:
        visible = s <= r
        ref_row, last_row = c // 2, c - 1
    tri = jnp.where(visible, 1.0, 0.0).astype(BF16)
    b = _split_cumsum(tri, logf)
    b_ref = b[ref_row:ref_row + 1, :]
    b_last = b[last_row:last_row + 1, :]
    nt = (((1,), (1,)), ((), ()))
    tn = (((0,), (0,)), ((), ()))
    qd = (q * jnp.exp(b - b_ref)).astype(BF16)
    kd = (k * jnp.exp(b_ref - b)).astype(BF16)
    scores = lax.dot_general(qd, kd, nt, preferred_element_type=F32)
    scores = jnp.where(visible, scores, 0.0).astype(BF16)
    o_intra = jnp.dot(scores, v_bf, preferred_element_type=F32)
    qe = (q * jnp.exp(b)).astype(BF16)
    o_inter = lax.dot_general(qe, state_t.astype(BF16), nt, preferred_element_type=F32)
    ks = (k * jnp.exp(b_last - b)).astype(BF16)
    d_state_t = lax.dot_general(v_bf, ks, tn, preferred_element_type=F32)
    new_state_t = state_t * jnp.exp(b_last) + d_state_t
    return o_intra + o_inter, new_state_t


def _forget_gate(logit, lb):
    f = lb + (1.0 - lb) * jax.nn.sigmoid(logit)
    return 1.0 - f, jnp.log(f)


def _chunk_sum_operators(reverse):
    g = HGRN_GROUP
    r = lax.broadcasted_iota(jnp.int32, (g, g), 0)
    s = lax.broadcasted_iota(jnp.int32, (g, g), 1)
    shift = H_CHUNK.bit_length() - 1
    same = (r >> shift) == (s >> shift)
    s_loc = s & (H_CHUNK - 1)
    if reverse:
        cum, ref = s >= r, s_loc >= H_CHUNK - 1 - H_CHUNK // 2
    else:
        cum, ref = s <= r, s_loc <= H_CHUNK // 2
    ops = [jnp.logical_and(same, cum), jnp.logical_and(same, ref), same]
    return jnp.concatenate([jnp.where(o, 1.0, 0.0) for o in ops], axis=0).astype(BF16)


def _hgrn_kernel(q_ref, ff_ref, fb_ref, v_ref, hg_ref, mff_ref, mv_ref, lbf_ref, lbb_ref, g_ref, o_ref,
                 qdf_ref, kdf_ref, qdb_ref, kdb_ref, qe_ref, ks_ref, dec_ref, ds_ref, sp_ref, acc_ref):
    seq = q_ref.shape[0]
    n_heads = q_ref.shape[1] // H_DV
    n_chunks = seq // H_CHUNK
    dk = H_EXPAND
    nt = (((1,), (1,)), ((), ()))
    tn = (((0,), (0,)), ((), ()))
    ops_f = _chunk_sum_operators(False)
    ops_b = _chunk_sum_operators(True)
    r = lax.broadcasted_iota(jnp.int32, (2 * H_CHUNK, 2 * H_CHUNK), 0)
    s = lax.broadcasted_iota(jnp.int32, (2 * H_CHUNK, 2 * H_CHUNK), 1)
    same_chunk = (r >= H_CHUNK) == (s >= H_CHUNK)
    vis_f = jnp.logical_and(same_chunk, s <= r)
    vis_b = jnp.logical_and(same_chunk, s >= r)
    mrow = lax.broadcasted_iota(jnp.int32, (H_CHUNK, 1), 0)
    is_meta = mrow >= H_CHUNK - N_META
    zeros_q = jnp.zeros((H_CHUNK, dk), F32)
    state0 = jnp.zeros((H_DV, dk), F32)
    chunks_per_group = HGRN_GROUP // H_CHUNK

    def chunk_rows(x):
        return jnp.concatenate([x[c * H_CHUNK:c * H_CHUNK + 1] for c in range(chunks_per_group)], axis=0)

    def per_chunk(y):
        return jnp.concatenate([jnp.broadcast_to(y[c:c + 1], (H_CHUNK, y.shape[1]))
                                for c in range(chunks_per_group)], axis=0)

    def dec_row(c):
        return pl.ds(pl.multiple_of(c * DEC_ROWS, DEC_ROWS), DEC_ROWS)

    for j in range(n_heads):
        head = pl.ds(j * H_DV, H_DV)
        lbf = lbf_ref[:, head]
        lbb = lbb_ref[:, head]

        def prepare(g, carry):
            rows = pl.ds(pl.multiple_of(g * HGRN_GROUP, HGRN_GROUP), HGRN_GROUP)
            q = jax.nn.silu(q_ref[rows, head].astype(F32))
            for logit_ref, lb, ops, qd_ref, kd_ref, col in ((ff_ref, lbf, ops_f, qdf_ref, kdf_ref, 0),
                                                           (fb_ref, lbb, ops_b, qdb_ref, kdb_ref, dk)):
                k, logf = _forget_gate(logit_ref[rows, head].astype(F32), lb)
                hi = logf.astype(BF16)
                lo = (logf - hi.astype(F32)).astype(BF16)
                sums = jnp.dot(ops, jnp.concatenate([hi, lo], axis=1), preferred_element_type=F32)
                sums = sums[:, 0:dk] + sums[:, dk:]
                b = sums[0:HGRN_GROUP]
                b_mid = sums[HGRN_GROUP:2 * HGRN_GROUP]
                b_all = sums[2 * HGRN_GROUP:]
                e_mid = per_chunk(jnp.exp(chunk_rows(b_mid)))
                e_all = jnp.exp(chunk_rows(b_all))
                e_rest = per_chunk(jnp.exp(chunk_rows(b_all) - chunk_rows(b_mid)))
                qd = q * jnp.exp(b - b_mid)
                kd = k * jnp.exp(b_mid - b)
                qd_ref[rows, :] = qd.astype(BF16)
                kd_ref[rows, :] = kd.astype(BF16)
                qe_ref[rows, col:col + dk] = (qd * e_mid).astype(BF16)
                ks_ref[rows, col:col + dk] = (kd * e_rest).astype(BF16)
                group_rows = chunks_per_group * DEC_ROWS
                dec_ref[pl.ds(pl.multiple_of(g * group_rows, group_rows), group_rows), col:col + dk] = (
                    jnp.concatenate([jnp.broadcast_to(e_all[c:c + 1], (DEC_ROWS, dk))
                                     for c in range(chunks_per_group)], axis=0))
            return carry

        lax.fori_loop(0, seq // HGRN_GROUP, prepare, 0, unroll=2)

        def intra(p, carry):
            rows = pl.ds(pl.multiple_of(p * (2 * H_CHUNK), 2 * H_CHUNK), 2 * H_CHUNK)
            s_f = lax.dot_general(qdf_ref[rows, :], kdf_ref[rows, :], nt, preferred_element_type=F32)
            s_b = lax.dot_general(qdb_ref[rows, :], kdb_ref[rows, :], nt, preferred_element_type=F32)
            w = (jnp.where(vis_f, s_f, 0.0) + jnp.where(vis_b, s_b, 0.0)).astype(BF16)
            acc_ref[rows, head] = jnp.dot(w, v_ref[rows, head], preferred_element_type=F32)
            for half in range(2):
                c = 2 * p + half
                rows_c = pl.ds(pl.multiple_of(c * H_CHUNK, H_CHUNK), H_CHUNK)
                ds_ref[pl.ds(pl.multiple_of(c * H_DV, H_DV), H_DV), :] = lax.dot_general(
                    v_ref[rows_c, head], ks_ref[rows_c, :], tn, preferred_element_type=F32)
            return carry

        lax.fori_loop(0, n_chunks // 2, intra, 0, unroll=HGRN_UNROLL // 2)

        mk, mlogf = _forget_gate(mff_ref[:, head].astype(F32), lbf)
        mk = jnp.where(is_meta, mk, 0.0)
        mlogf = jnp.where(is_meta, mlogf, 0.0)
        state_f0 = _hgrn_chunk(zeros_q, mk, mlogf, mv_ref[:, head], state0, reverse=False)[1]

        def scan(i, carry):
            st_f, st_b = carry
            cf = i
            cb = n_chunks - 1 - i
            blk_f = pl.ds(pl.multiple_of(cf * H_DV, H_DV), H_DV)
            blk_b = pl.ds(pl.multiple_of(cb * H_DV, H_DV), H_DV)
            sp_ref[blk_f, 0:dk] = st_f.astype(BF16)
            sp_ref[blk_b, dk:] = st_b.astype(BF16)
            st_f = st_f * dec_ref[dec_row(cf), 0:dk][0:1] + ds_ref[blk_f, 0:dk]
            st_b = st_b * dec_ref[dec_row(cb), dk:][0:1] + ds_ref[blk_b, dk:]
            return st_f, st_b

        lax.fori_loop(0, n_chunks, scan, (state_f0, state0))

        def inter(c, carry):
            rows = pl.ds(pl.multiple_of(c * H_CHUNK, H_CHUNK), H_CHUNK)
            state = sp_ref[pl.ds(pl.multiple_of(c * H_DV, H_DV), H_DV), :]
            acc_ref[rows, head] += lax.dot_general(qe_ref[rows, :], state, nt, preferred_element_type=F32)
            return carry

        lax.fori_loop(0, n_chunks, inter, 0, unroll=HGRN_UNROLL)

    for j in range(n_heads):
        head = pl.ds(j * H_DV, H_DV)
        o = acc_ref[:, head]
        ms = jnp.mean(o * o, axis=-1, keepdims=True)
        normed = o * lax.rsqrt(ms + EPS) * g_ref[...]
        o_ref[:, head] = (normed * jax.nn.sigmoid(hg_ref[:, head].astype(F32))).astype(o_ref.dtype)


def _hgrn(proj, meta_proj, lb_fwd, lb_bwd, norm_g, n_seq, seq):
    n_tok = proj.shape[0]
    hps = HGRN_HEADS_PER_STEP
    w = hps * H_DV
    col0 = 3 * A_HEADS * (2 * A_HEAD_DIM) // w
    per_sec = H_HEADS // hps
    meta_blk = META_PAD // H_CHUNK - 1

    def sec(k):
        return pl.BlockSpec((seq, w), lambda b, h, k=k: (b, col0 + k * per_sec + h))

    def meta_sec(k):
        return pl.BlockSpec((H_CHUNK, w), lambda b, h, k=k: (meta_blk, col0 + k * per_sec + h))

    return pl.pallas_call(
        _hgrn_kernel,
        grid=(n_seq, per_sec),
        in_specs=[
            sec(0), sec(1), sec(2), sec(3), sec(4),
            meta_sec(1), meta_sec(3),
            pl.BlockSpec((1, w), lambda b, h: (0, h)),
            pl.BlockSpec((1, w), lambda b, h: (0, h)),
            pl.BlockSpec((1, H_DV), lambda b, h: (0, 0)),
        ],
        out_specs=pl.BlockSpec((seq, w), lambda b, h: (b, h)),
        out_shape=jax.ShapeDtypeStruct((n_tok, H_HEADS * H_DV), BF16),
        scratch_shapes=[
            pltpu.VMEM((seq, H_EXPAND), BF16), pltpu.VMEM((seq, H_EXPAND), BF16),
            pltpu.VMEM((seq, H_EXPAND), BF16), pltpu.VMEM((seq, H_EXPAND), BF16),
            pltpu.VMEM((seq, 2 * H_EXPAND), BF16),
            pltpu.VMEM((seq, 2 * H_EXPAND), BF16),
            pltpu.VMEM((seq // H_CHUNK * DEC_ROWS, 2 * H_EXPAND), F32),
            pltpu.VMEM((seq // H_CHUNK * H_DV, 2 * H_EXPAND), F32),
            pltpu.VMEM((seq // H_CHUNK * H_DV, 2 * H_EXPAND), BF16),
            pltpu.VMEM((seq, w), F32),
        ],
        compiler_params=_cparams(("parallel", "parallel"), 48),
        name="hgrn",
    )(proj, proj, proj, proj, proj, meta_proj, meta_proj, lb_fwd, lb_bwd, norm_g)


def _merge_kernel(ao_ref, ho_ref, ga_ref, gh_ref, xp_ref, xs_ref, wa_ref, wh_ref, wo_ref, lng_ref, lnb_ref,
                  l1g_ref, l1b_ref, hf_ref, hp_ref, *, tp):
    pa = jnp.dot(ao_ref[...], wa_ref[...], preferred_element_type=F32)
    ph = jnp.dot(ho_ref[...], wh_ref[...], preferred_element_type=F32)
    merged = (jax.nn.sigmoid(ga_ref[...].astype(F32)) * pa + jax.nn.sigmoid(gh_ref[...].astype(F32)) * ph)
    t = jnp.dot(merged.astype(BF16), wo_ref[...], preferred_element_type=F32)

    def finish(x_ref):
        x_in = _layer_norm_f32(x_ref[...], lng_ref[...], lnb_ref[...])
        h = _layer_norm_f32(DN_ALPHA * x_in + t, l1g_ref[...], l1b_ref[...])
        hf_ref[...] = h
        hp_ref[...] = _pack_bf16_pair(h)

    @pl.when(pl.program_id(0) < tp)
    def _():
        finish(xp_ref)

    @pl.when(pl.program_id(0) >= tp)
    def _():
        finish(xs_ref)


def _merge(ao, ho, proj, xp, xs, wa, wh, wo, ln_g, ln_b, l1g, l1b, tm):
    d = xp.shape[1]
    tp, ts = xp.shape[0] // tm, xs.shape[0] // tm
    n_tok = (tp + ts) * tm
    gate_blk0 = (proj.shape[1] - 2 * d) // d
    vec = pl.BlockSpec((1, d), lambda i: (0, 0))
    xp_spec, xs_spec = _two_group_specs(tm, d, tp, False)
    return pl.pallas_call(
        functools.partial(_merge_kernel, tp=tp),
        grid=(tp + ts,),
        in_specs=[
            pl.BlockSpec((tm, ao.shape[1]), lambda i: (i, 0)),
            pl.BlockSpec((tm, ho.shape[1]), lambda i: (i, 0)),
            pl.BlockSpec((tm, d), lambda i: (i, gate_blk0)),
            pl.BlockSpec((tm, d), lambda i: (i, gate_blk0 + 1)),
            xp_spec, xs_spec,
            _resident(wa.shape), _resident(wh.shape), _resident(wo.shape),
            vec, vec, vec, vec,
        ],
        out_specs=[pl.BlockSpec((tm, d), lambda i: (i, 0)), pl.BlockSpec((tm, d // 2), lambda i: (i, 0))],
        out_shape=[jax.ShapeDtypeStruct((n_tok, d), F32), jax.ShapeDtypeStruct((n_tok, d // 2), U32)],
        compiler_params=_cparams(("parallel",), 56),
        name="merge",
    )(ao, ho, proj, proj, xp, xs, wa, wh, wo, ln_g, ln_b, l1g, l1b)


def _first_argmax(x, iota, size):
    m = jnp.max(x, axis=0, keepdims=True)
    idx = jnp.min(jnp.where(x == m, iota, size), axis=0, keepdims=True)
    return m, idx


def _router_kernel(h_ref, wrt_ref, bias_ref, idx_ref, gate_ref, rank_ref, cnt_ref, base_ref):
    tm = h_ref.shape[0]
    half = h_ref.shape[1]

    @pl.when(pl.program_id(0) == 0)
    def _():
        base_ref[...] = jnp.zeros_like(base_ref)

    nt = (((1,), (1,)), ((), ()))
    h_lo, h_hi = _unpack_bf16_pair(h_ref[...])
    logits = (lax.dot_general(wrt_ref[:, 0:half], h_lo, nt, preferred_element_type=F32)
              + lax.dot_general(wrt_ref[:, half:], h_hi, nt, preferred_element_type=F32))
    scores = jax.nn.sigmoid(logits)
    biased = scores + bias_ref[...]
    neg_inf = jnp.asarray(-jnp.inf, F32)

    iota_g = lax.broadcasted_iota(jnp.int32, (GROUP_SIZE, tm), 0)
    iota_n = lax.broadcasted_iota(jnp.int32, (N_GROUPS, tm), 0)
    grp = jnp.full((N_GROUPS, tm), neg_inf, F32)
    for g in range(N_GROUPS):
        xg = biased[g * GROUP_SIZE:(g + 1) * GROUP_SIZE, :]
        m1, i1 = _first_argmax(xg, iota_g, GROUP_SIZE)
        m2 = jnp.max(jnp.where(iota_g == i1, neg_inf, xg), axis=0, keepdims=True)
        grp = jnp.where(iota_n == g, m1 + m2, grp)

    keep_f = jnp.zeros((N_GROUPS, tm), F32)
    for _ in range(TOPK_GROUPS):
        _, ig = _first_argmax(grp, iota_n, N_GROUPS)
        sel = iota_n == ig
        keep_f = jnp.where(sel, 1.0, keep_f)
        grp = jnp.where(sel, neg_inf, grp)

    masked = jnp.concatenate(
        [jnp.where(keep_f[g:g + 1, :] > 0.5, biased[g * GROUP_SIZE:(g + 1) * GROUP_SIZE, :], neg_inf)
         for g in range(N_GROUPS)], axis=0)

    iota_e = lax.broadcasted_iota(jnp.int32, (N_EXPERTS, tm), 0)
    chosen = jnp.zeros((N_EXPERTS, tm), F32)
    idxs, gates = [], []
    for _ in range(TOP_K):
        _, ie = _first_argmax(masked, iota_e, N_EXPERTS)
        sel = iota_e == ie
        gates.append(jnp.sum(jnp.where(sel, scores, 0.0), axis=0, keepdims=True))
        idxs.append(ie)
        chosen = jnp.where(sel, 1.0, chosen)
        masked = jnp.where(sel, neg_inf, masked)

    denom = gates[0]
    for gk in gates[1:]:
        denom = denom + gk

    r = lax.broadcasted_iota(jnp.int32, (tm, tm), 0)
    c = lax.broadcasted_iota(jnp.int32, (tm, tm), 1)
    before = jnp.where(r < c, 1.0, 0.0).astype(BF16)
    pos = jnp.dot(chosen.astype(BF16), before, preferred_element_type=F32) + base_ref[...]
    for k in range(TOP_K):
        sel = iota_e == idxs[k]
        rank_ref[k:k + 1, :] = jnp.sum(jnp.where(sel, pos, 0.0), axis=0, keepdims=True).astype(jnp.int32)
        idx_ref[k:k + 1, :] = idxs[k]
        gate_ref[k:k + 1, :] = gates[k] / denom * ROUTED_SCALE

    base_ref[...] = base_ref[...] + jnp.sum(chosen, axis=1, keepdims=True)
    cnt_ref[...] = base_ref[...]


def _router(h_pk, w_router_t, bias_col, tm):
    n_tok, half = h_pk.shape
    row_blk = pl.BlockSpec((TOP_K, tm), lambda i: (0, i))
    return pl.pallas_call(
        _router_kernel,
        grid=(n_tok // tm,),
        in_specs=[
            pl.BlockSpec((tm, half), lambda i: (i, 0)),
            _resident(w_router_t.shape),
            pl.BlockSpec((N_EXPERTS, 1), lambda i: (0, 0)),
        ],
        out_specs=[row_blk, row_blk, row_blk, pl.BlockSpec((N_EXPERTS, 1), lambda i: (0, 0))],
        out_shape=[
            jax.ShapeDtypeStruct((TOP_K, n_tok), jnp.int32),
            jax.ShapeDtypeStruct((TOP_K, n_tok), F32),
            jax.ShapeDtypeStruct((TOP_K, n_tok), jnp.int32),
            jax.ShapeDtypeStruct((N_EXPERTS, 1), F32),
        ],
        scratch_shapes=[pltpu.VMEM((N_EXPERTS, 1), F32)],
        compiler_params=_cparams(("arbitrary",), 32),
        name="router",
    )(h_pk, w_router_t, bias_col)


def _dest_kernel(idx_ref, rank_ref, start_ref, dest_ref):
    tm = idx_ref.shape[1]
    iota_e = lax.broadcasted_iota(jnp.int32, (N_EXPERTS, tm), 0)
    start = start_ref[...]
    for k in range(TOP_K):
        sel = iota_e == idx_ref[k:k + 1, :]
        base = jnp.sum(jnp.where(sel, start, 0.0), axis=0, keepdims=True)
        dest_ref[k:k + 1, :] = base.astype(jnp.int32) + rank_ref[k:k + 1, :]


def _dest(idx, rank, start_col, tm):
    n_tok = idx.shape[1]
    row_blk = pl.BlockSpec((TOP_K, tm), lambda i: (0, i))
    return pl.pallas_call(
        _dest_kernel,
        grid=(n_tok // tm,),
        in_specs=[row_blk, row_blk, pl.BlockSpec((N_EXPERTS, 1), lambda i: (0, 0))],
        out_specs=row_blk,
        out_shape=jax.ShapeDtypeStruct((TOP_K, n_tok), jnp.int32),
        compiler_params=_cparams(("parallel",), 32),
        name="dest",
    )(idx, rank, start_col)


def _dispatch_kernel(pstart_ref, cnt_ref, tail_ref, dest_ref, h_ref, xs_ref, zero_ref, sem, pad_sem, tail_sem):
    tm = h_ref.shape[0]

    def pad_copy(e, j):
        return pltpu.make_async_copy(zero_ref.at[pl.ds(0, 1)],
                                     xs_ref.at[pl.ds(pstart_ref[e] + cnt_ref[e] + j, 1)], pad_sem)

    def tail_copy():
        return pltpu.make_async_copy(
            zero_ref, xs_ref.at[pl.ds(pl.multiple_of(tail_ref[0], ROW_ALIGN), EXPERT_ROWS)], tail_sem)

    def for_each_pad_row(action):
        def body(e, carry):
            n_pad = (ROW_ALIGN - cnt_ref[e] % ROW_ALIGN) % ROW_ALIGN
            for j in range(ROW_ALIGN - 1):
                @pl.when(j < n_pad)
                def _():
                    action(pad_copy(e, j))
            return carry
        lax.fori_loop(0, N_EXPERTS, body, 0)

    @pl.when(pl.program_id(0) == 0)
    def _():
        zero_ref[...] = jnp.zeros_like(zero_ref)
        tail_copy().start()
        for_each_pad_row(lambda c: c.start())
        for_each_pad_row(lambda c: c.wait())
        tail_copy().wait()

    def row_copy(r, k):
        return pltpu.make_async_copy(h_ref.at[pl.ds(r, 1)], xs_ref.at[pl.ds(dest_ref[0, k * tm + r], 1)], sem)

    def start(r, carry):
        for k in range(TOP_K):
            row_copy(r, k).start(priority=k % DMA_THREADS)
        return carry

    def wait(r, carry):
        for k in range(TOP_K):
            row_copy(r, k).wait()
        return carry

    lax.fori_loop(0, tm, start, 0)
    lax.fori_loop(0, tm, wait, 0)


def _dispatch(h_pk, dest_tiles, pstart, counts, tail, n_rows, tm):
    n_tok, half = h_pk.shape
    grid_spec = pltpu.PrefetchScalarGridSpec(
        num_scalar_prefetch=3,
        grid=(n_tok // tm,),
        in_specs=[
            pl.BlockSpec((None, 1, TOP_K * tm), lambda i, *_: (i, 0, 0), memory_space=pltpu.SMEM),
            pl.BlockSpec((tm, half), lambda i, *_: (i, 0)),
        ],
        out_specs=pl.BlockSpec(memory_space=pl.ANY),
        scratch_shapes=[pltpu.VMEM((EXPERT_ROWS, half), U32), pltpu.SemaphoreType.DMA,
                        pltpu.SemaphoreType.DMA, pltpu.SemaphoreType.DMA],
    )
    return pl.pallas_call(
        _dispatch_kernel,
        grid_spec=grid_spec,
        out_shape=jax.ShapeDtypeStruct((n_rows, half), U32),
        compiler_params=_cparams(("arbitrary",), 32, disable_bounds_checks=True),
        name="dispatch",
    )(pstart, counts, tail, dest_tiles, h_pk)


def _experts_kernel(row_ref, e_ref, valid_ref, newe_ref, slot_ref, nexte_ref,
                    x_hbm, wg_hbm, wu_hbm, wd_hbm, y_hbm,
                    x_buf, y_buf, wg_buf, wu_buf, wd_buf, wgu_bf, wd_bf, x_sems, y_sems, sems):
    i = pl.program_id(0)
    n_steps = pl.num_programs(0)
    ff = wg_buf.shape[2]
    cur = i % 2

    def window(step):
        return pl.ds(pl.multiple_of(row_ref[step], ROW_ALIGN), EXPERT_ROWS)

    def x_copy(step, slot):
        return pltpu.make_async_copy(x_hbm.at[window(step)], x_buf.at[slot], x_sems.at[slot])

    def y_copy(step, slot):
        return pltpu.make_async_copy(y_buf.at[slot], y_hbm.at[window(step)], y_sems.at[slot])

    @pl.when(i == 0)
    def _():
        x_copy(0, 0).start()

    def weight_copies(e, slot):
        copies = []
        for m, (src, dst) in enumerate(((wg_hbm, wg_buf), (wu_hbm, wu_buf), (wd_hbm, wd_buf))):
            rows = dst.shape[1] // WEIGHT_DMA_CHUNKS
            for c in range(WEIGHT_DMA_CHUNKS):
                part = pl.ds(c * rows, rows)
                copies.append(pltpu.make_async_copy(src.at[0, e, part], dst.at[slot, part],
                                                    sems.at[slot, m * WEIGHT_DMA_CHUNKS + c]))
        return copies

    @pl.when(newe_ref[i] == 1)
    def _():
        slot = slot_ref[i]
        e = e_ref[i]

        @pl.when(i == 0)
        def _():
            for c in weight_copies(e, slot):
                c.start(priority=WEIGHT_DMA_THREAD)

        for c in weight_copies(e, slot):
            c.wait()
        nxt = nexte_ref[i]

        @pl.when(nxt >= 0)
        def _():
            for c in weight_copies(nxt, 1 - slot):
                c.start(priority=WEIGHT_DMA_THREAD)

        wgu_bf[:, 0:ff] = wg_buf[slot].astype(BF16)
        wgu_bf[:, ff:] = wu_buf[slot].astype(BF16)
        wd_bf[...] = wd_buf[slot].astype(BF16)

    @pl.when(valid_ref[i] == 1)
    def _():
        nxt_step = jnp.minimum(i + 1, n_steps - 1)
        has_next = jnp.logical_and(i + 1 < n_steps, valid_ref[nxt_step] == 1)
        x_copy(i, cur).wait()

        @pl.when(has_next)
        def _():
            x_copy(nxt_step, 1 - cur).start()

        gu = _dot_packed(x_buf[cur], wgu_bf)
        hid = (jax.nn.silu(gu[:, 0:ff]) * gu[:, ff:]).astype(BF16)
        y_buf[cur] = _pack_bf16_pair(jnp.dot(hid, wd_bf[...], preferred_element_type=F32))

        @pl.when(i > 0)
        def _():
            y_copy(i - 1, 1 - cur).wait()

        y_copy(i, cur).start()

        @pl.when(jnp.logical_not(has_next))
        def _():
            y_copy(i, cur).wait()


def _experts(x_sorted, steps, w_gate, w_up, w_down):
    n_rows, half = x_sorted.shape
    d = 2 * half
    ff = w_gate.shape[-1]
    hbm = pl.BlockSpec(memory_space=pl.ANY)
    grid_spec = pltpu.PrefetchScalarGridSpec(
        num_scalar_prefetch=len(steps),
        grid=(steps[0].shape[0],),
        in_specs=[hbm, hbm, hbm, hbm],
        out_specs=hbm,
        scratch_shapes=[pltpu.VMEM((2, EXPERT_ROWS, half), U32), pltpu.VMEM((2, EXPERT_ROWS, half), U32),
                        pltpu.VMEM((2, d, ff), F32), pltpu.VMEM((2, d, ff), F32), pltpu.VMEM((2, ff, d), F32),
                        pltpu.VMEM((d, 2 * ff), BF16), pltpu.VMEM((ff, d), BF16),
                        pltpu.SemaphoreType.DMA((2,)), pltpu.SemaphoreType.DMA((2,)),
                        pltpu.SemaphoreType.DMA((2, 3 * WEIGHT_DMA_CHUNKS))],
    )
    return pl.pallas_call(
        _experts_kernel,
        grid_spec=grid_spec,
        out_shape=jax.ShapeDtypeStruct((n_rows, half), U32),
        compiler_params=_cparams(("arbitrary",), 52, disable_bounds_checks=True),
        name="experts",
    )(*steps, x_sorted, w_gate, w_up, w_down)


def _final_kernel(dest_ref, gate_ref, hf_ref, hp_ref, y_ref, wgu_ref, wd_ref, g_ref, b_ref, op_ref, os_ref,
                  ybuf, sem, *, tp):
    tm = hf_ref.shape[0]
    ff = wd_ref.shape[0]

    def row_copy(r, k):
        return pltpu.make_async_copy(y_ref.at[pl.ds(dest_ref[0, k * tm + r], 1)], ybuf.at[k, pl.ds(r, 1)], sem)

    def start(r, carry):
        for k in range(TOP_K):
            row_copy(r, k).start(priority=k % DMA_THREADS)
        return carry

    def wait(r, carry):
        for k in range(TOP_K):
            row_copy(r, k).wait()
        return carry

    lax.fori_loop(0, tm, start, 0)

    gu = _dot_packed(hp_ref[...], wgu_ref)
    hid = (jax.nn.silu(gu[:, 0:ff]) * gu[:, ff:]).astype(BF16)
    acc = DN_ALPHA * hf_ref[...] + jnp.dot(hid, wd_ref[...], preferred_element_type=F32)

    lax.fori_loop(0, tm, wait, 0)
    gate = gate_ref[...]
    half = ybuf.shape[2]
    acc_lo, acc_hi = acc[:, :half], acc[:, half:]
    for k in range(TOP_K):
        packed = ybuf[k]
        g_k = gate[:, k:k + 1]
        acc_lo = acc_lo + lax.bitcast_convert_type(packed << 16, F32) * g_k
        acc_hi = acc_hi + lax.bitcast_convert_type(packed & jnp.uint32(HIGH_HALF), F32) * g_k
    out = _layer_norm_f32(jnp.concatenate([acc_lo, acc_hi], axis=1), g_ref[...], b_ref[...])

    @pl.when(pl.program_id(0) < tp)
    def _():
        op_ref[...] = out

    @pl.when(pl.program_id(0) >= tp)
    def _():
        os_ref[...] = out


def _final(h_f32, h_pk, y_sorted, dest_tiles, gate_t, w_gu, w_d, g, b, n_prompt, tm):
    n_tok, d = h_f32.shape
    tp = n_prompt // tm
    ts = n_tok // tm - tp
    vec = pl.BlockSpec((1, d), lambda i: (0, 0))
    op_spec, os_spec = _two_group_specs(tm, d, tp, False)
    return pl.pallas_call(
        functools.partial(_final_kernel, tp=tp),
        grid=(tp + ts,),
        in_specs=[
            pl.BlockSpec((None, 1, TOP_K * tm), lambda i: (i, 0, 0), memory_space=pltpu.SMEM),
            pl.BlockSpec((tm, TOP_K), lambda i: (i, 0)),
            pl.BlockSpec((tm, d), lambda i: (i, 0)),
            pl.BlockSpec((tm, d // 2), lambda i: (i, 0)),
            pl.BlockSpec(memory_space=pl.ANY),
            _resident(w_gu.shape), _resident(w_d.shape), vec, vec,
        ],
        out_specs=[op_spec, os_spec],
        out_shape=[jax.ShapeDtypeStruct((tp * tm, d), F32), jax.ShapeDtypeStruct((ts * tm, d), F32)],
        scratch_shapes=[pltpu.VMEM((TOP_K, tm, d // 2), U32), pltpu.SemaphoreType.DMA],
        compiler_params=_cparams(("arbitrary",), 48, disable_bounds_checks=True),
        name="final",
    )(dest_tiles, gate_t, h_f32, h_pk, y_sorted, w_gu, w_d, g, b)


def _lower_bounds(gamma):
    return jnp.cumsum(jax.nn.softmax(gamma.astype(F32), axis=0), axis=0)


def _tile_major(dest, tm):
    k, n = dest.shape
    return jnp.transpose(dest.reshape(k, n // tm, tm), (1, 0, 2)).reshape(n // tm, 1, k * tm)


def _expert_steps(counts, n_assign):
    rows = EXPERT_ROWS
    n_steps = (n_assign + N_EXPERTS * (ROW_ALIGN - 1)) // rows + N_EXPERTS
    padded =(counts + ROW_ALIGN - 1) // ROW_ALIGN * ROW_ALIGN
    pend = jnp.cumsum(padded)
    pstart = pend - padded
    per_e = (padded + rows - 1) // rows
    st_end = jnp.cumsum(per_e)
    st_start = st_end - per_e
    total = st_end[-1]
    i = jnp.arange(n_steps, dtype=jnp.int32)
    ic = jnp.minimum(i, total - 1)
    e = jnp.sum((st_end[None, :] <= ic[:, None]).astype(jnp.int32), axis=1)
    j = ic - st_start[e]
    row0 = pstart[e] + jnp.minimum(j * rows, jnp.maximum(padded[e] - rows, 0))
    valid = i < total
    prev_e = jnp.concatenate([jnp.full((1,), -1, jnp.int32), e[:-1].astype(jnp.int32)])
    new_e = jnp.where(jnp.logical_and(valid, e != prev_e), 1, 0)
    nonempty = counts > 0
    slot = (jnp.cumsum(nonempty.astype(jnp.int32)) - 1) % 2
    ids = jnp.arange(N_EXPERTS, dtype=jnp.int32)
    later = jnp.concatenate([jnp.where(nonempty, ids, N_EXPERTS)[1:], jnp.full((1,), N_EXPERTS, jnp.int32)])
    nxt = jnp.flip(lax.cummin(jnp.flip(later)))
    nxt = jnp.where(nxt < N_EXPERTS, nxt, -1)
    as_i32 = lambda v: v.astype(jnp.int32)
    steps = (as_i32(row0), as_i32(e), as_i32(valid), as_i32(new_e), as_i32(slot[e]), as_i32(nxt[e]))
    n_rows = n_assign + N_EXPERTS * (ROW_ALIGN - 1) + rows
    return as_i32(pstart), as_i32(pend[-1:]), n_rows, steps


def kernel(x_prompt, x_sample, meta_tokens, ln_in_g, ln_in_b, hg_gamma_fwd, hg_gamma_bwd, w_in, lambda_q1, lambda_k1, lambda_q2, lambda_k2, attn_subln_g, hg_norm_g, w_branch_attn, w_branch_hgrn, w_out, ln1_g, ln1_b, w_router, router_bias, w_sh_gate, w_sh_up, w_sh_down, w_ex_gate, w_ex_up, w_ex_down, ln2_g, ln2_b):
    bp, seq, d = x_prompt.shape
    bs, seq_s, _ = x_sample.shape
    assert seq == seq_s and w_in.shape[0] == DEPTH
    n_seq = bp + bs
    n_tok = n_seq * seq
    layer = 0

    xp = x_prompt.reshape(bp * seq, d)
    xs = x_sample.reshape(bs * seq, d)
    row = lambda v: v.reshape(1, -1).astype(F32)
    ln_g, ln_b = row(ln_in_g), row(ln_in_b)
    w_in_bf = w_in[layer].astype(BF16)

    proj = _ln_matmul(xp, xs, ln_g, ln_b, w_in_bf, tm=min(1024, bs * seq), tn=1024)
    meta = meta_tokens.astype(F32)
    meta_proj = _ln_matmul(meta, meta, ln_g, ln_b, w_in_bf, tm=N_META, tn=1024)[:N_META]
    meta_proj = jnp.pad(meta_proj, ((META_PAD - N_META, 0), (0, 0)))

    lam = (jnp.exp(jnp.sum(lambda_q1[layer].astype(F32) * lambda_k1[layer].astype(F32)))
           - jnp.exp(jnp.sum(lambda_q2[layer].astype(F32) * lambda_k2[layer].astype(F32))) + LAMBDA_INIT)
    slopes = 2.0 ** (-8.0 * jnp.arange(1, A_HEADS + 1, dtype=F32) / A_HEADS)
    attn_params = jnp.concatenate([lam.reshape(1), slopes]).astype(F32)
    subln_g = row(attn_subln_g[layer]) * (1.0 - LAMBDA_INIT)
    ao = _diff_attn(proj, meta_proj, attn_params, subln_g, n_seq, seq, tq=min(256, seq))

    lb_fwd = row(_lower_bounds(hg_gamma_fwd)[layer])
    lb_bwd = row(_lower_bounds(hg_gamma_bwd)[layer])
    ho = _hgrn(proj, meta_proj, lb_fwd, lb_bwd, row(hg_norm_g[layer]), n_seq, seq)

    h_f32, h_pk = _merge(ao, ho, proj, xp, xs, w_branch_attn[layer].astype(BF16),
                         w_branch_hgrn[layer].astype(BF16), w_out[layer].astype(BF16), ln_g, ln_b,
                         row(ln1_g[layer]), row(ln1_b[layer]), tm=min(256, bs * seq))

    idx, gate, rank, counts = _router(h_pk, w_router[layer].T.astype(BF16),
                                      router_bias[layer].astype(F32).reshape(N_EXPERTS, 1), tm=min(512, n_tok))
    counts = counts[:, 0].astype(jnp.int32)
    pstart, tail, n_rows, steps = _expert_steps(counts, n_tok * TOP_K)
    dest = _dest(idx, rank, pstart.astype(F32).reshape(N_EXPERTS, 1), tm=min(2048, n_tok))

    tm_d = min(256, n_tok)
    x_sorted = _dispatch(h_pk, _tile_major(dest, tm_d), pstart, counts, tail, n_rows, tm_d)
    y_sorted = _experts(x_sorted, steps, w_ex_gate, w_ex_up, w_ex_down)

    w_sh_gu = jnp.concatenate([w_sh_gate[layer], w_sh_up[layer]], axis=1).astype(BF16)
    tm_f = min(256, bs * seq)
    y_prompt, y_sample = _final(h_f32, h_pk, y_sorted, _tile_major(dest, tm_f), gate.T, w_sh_gu,
                                w_sh_down[layer].astype(BF16), row(ln2_g[layer]), row(ln2_b[layer]),
                                bp * seq, tm_f)
    return (y_prompt.reshape(bp, seq, d), y_sample.reshape(bs, seq, d))
```

```python
import functools
import math

import jax
import jax.numpy as jnp
from jax import lax
from jax.experimental import pallas as pl
from jax.experimental.pallas import tpu as pltpu

F32 = jnp.float32
BF16 = jnp.bfloat16
U32 = jnp.uint32

N_META = 16
A_HEADS = 8
A_HEAD_DIM = 64
H_HEADS = 8
H_EXPAND = 128
H_DV = 128
H_CHUNK = 64
N_EXPERTS = 256
TOP_K = 8
N_GROUPS = 8
TOPK_GROUPS = 4
GROUP_SIZE = N_EXPERTS // N_GROUPS
ROUTED_SCALE = 2.5
DEPTH = 1
DN_ALPHA = (2 * DEPTH) ** 0.25
EPS = 1e-5
LAMBDA_INIT = 0.8 - 0.6 * math.exp(-0.3 * 0)

LANES = 128
HGRN_HEADS_PER_STEP = 4
HGRN_GROUP = 256
DEC_ROWS = 8
HGRN_UNROLL = 8
META_PAD = 128
NEG_BIG = -1e30
EXPERT_ROWS = 256
WEIGHT_DMA_CHUNKS = 4
DMA_THREADS = 2
WEIGHT_DMA_THREAD = 1
ROW_ALIGN = 8
WINDOWS_PER_STEP = 2
ATTN_ROWS = 256
HIGH_HALF = 0xFFFF0000


def _cparams(semantics, vmem_mb, **kw):
    return pltpu.CompilerParams(dimension_semantics=semantics, vmem_limit_bytes=vmem_mb * 1024 * 1024, **kw)


def _layer_norm_f32(x, g, b):
    mu = jnp.mean(x, axis=-1, keepdims=True)
    xc = x - mu
    var = jnp.mean(xc * xc, axis=-1, keepdims=True)
    return xc * lax.rsqrt(var + EPS) * g + b


def _resident(shape):
    nd = len(shape)
    return pl.BlockSpec(shape, lambda *_: (0,) * nd, pipeline_mode=pl.Buffered(1))


def _pack_bf16_pair(x):
    k = x.shape[1] // 2
    lo = lax.bitcast_convert_type(x[:, :k].astype(BF16).astype(F32), U32) >> 16
    hi = lax.bitcast_convert_type(x[:, k:].astype(BF16).astype(F32), U32) & jnp.uint32(HIGH_HALF)
    return lo | hi


def _unpack_bf16_pair(p):
    lo = lax.bitcast_convert_type(p << 16, F32).astype(BF16)
    hi = lax.bitcast_convert_type(p & jnp.uint32(HIGH_HALF), F32).astype(BF16)
    return lo, hi


def _dot_packed(p, w_ref):
    k = p.shape[1]
    lo, hi = _unpack_bf16_pair(p)
    return (jnp.dot(lo, w_ref[0:k, :], preferred_element_type=F32)
            + jnp.dot(hi, w_ref[k:, :], preferred_element_type=F32))


def _two_group_specs(tm, d, tp, extra_grid_axes):
    if extra_grid_axes:
        return (pl.BlockSpec((tm, d), lambda i, j: (jnp.minimum(i, tp - 1), 0), pipeline_mode=pl.Buffered(1)),
                pl.BlockSpec((tm, d), lambda i, j: (jnp.maximum(i - tp, 0), 0), pipeline_mode=pl.Buffered(1)))
    return (pl.BlockSpec((tm, d), lambda i: (jnp.minimum(i, tp - 1), 0)),
            pl.BlockSpec((tm, d), lambda i: (jnp.maximum(i - tp, 0), 0)))


def _ln_matmul_kernel(xp_ref, xs_ref, g_ref, b_ref, w_ref, o_ref, xn_ref, *, tp):
    i = pl.program_id(0)
    first_col = pl.program_id(1) == 0

    @pl.when(jnp.logical_and(first_col, i < tp))
    def _():
        xn_ref[...] = _layer_norm_f32(xp_ref[...], g_ref[...], b_ref[...]).astype(BF16)

    @pl.when(jnp.logical_and(first_col, i >= tp))
    def _():
        xn_ref[...] = _layer_norm_f32(xs_ref[...], g_ref[...], b_ref[...]).astype(BF16)

    o_ref[...] = jnp.dot(xn_ref[...], w_ref[...], preferred_element_type=F32).astype(o_ref.dtype)


def _ln_matmul(xp, xs, g, b, w, tm, tn):
    d = xp.shape[1]
    n = w.shape[1]
    tp, ts = xp.shape[0] // tm, xs.shape[0] // tm
    xp_spec, xs_spec = _two_group_specs(tm, d, tp, True)
    return pl.pallas_call(
        functools.partial(_ln_matmul_kernel, tp=tp),
        grid=(tp + ts, n // tn),
        in_specs=[
            xp_spec, xs_spec,
            pl.BlockSpec((1, d), lambda i, j: (0, 0)),
            pl.BlockSpec((1, d), lambda i, j: (0, 0)),
            pl.BlockSpec((d, tn), lambda i, j: (0, j)),
        ],
        out_specs=pl.BlockSpec((tm, tn), lambda i, j: (i, j)),
        out_shape=jax.ShapeDtypeStruct(((tp + ts) * tm, n), BF16),
        scratch_shapes=[pltpu.VMEM((tm, d), BF16)],
        compiler_params=_cparams(("parallel", "arbitrary"), 44),
        name="ln_matmul",
    )(xp, xs, g, b, w)


def _diff_attn_kernel(par_ref, q_ref, k_ref, v_ref, km_ref, vm_ref, g_ref, o_ref, kall_ref, vall_ref, bias_ref):
    h = pl.program_id(0)
    b = pl.program_id(1)
    qi = pl.program_id(2)
    tq = q_ref.shape[0]
    sub = bias_ref.shape[0]
    n_keys = kall_ref.shape[0]
    shift = bias_ref.shape[1] - n_keys

    @pl.when(jnp.logical_and(b == 0, qi == 0))
    def _():
        i = lax.broadcasted_iota(jnp.int32, (sub, 1), 0)
        j = lax.broadcasted_iota(jnp.int32, (1, bias_ref.shape[1]), 1)
        bias_ref[...] = -par_ref[1 + h] * jnp.abs(i + (META_PAD + shift) - j).astype(F32)

    @pl.when(qi == 0)
    def _():
        kall_ref[0:META_PAD, :] = km_ref[...]
        kall_ref[META_PAD:, :] = k_ref[...]
        vall_ref[0:META_PAD, :] = vm_ref[...]
        vall_ref[META_PAD:, :] = v_ref[...]

    lam = par_ref[0]
    lane = lax.broadcasted_iota(jnp.int32, (1, 2 * A_HEAD_DIM), 1)
    scale = jnp.asarray(A_HEAD_DIM ** -0.5, BF16)
    zero = jnp.zeros((), BF16)
    lane_m = lax.broadcasted_iota(jnp.int32, (1, META_PAD), 1)
    pad_mask = jnp.where(lane_m < META_PAD - N_META, NEG_BIG, 0.0)
    nt = (((1,), (1,)), ((), ()))

    def one_pass(r, carry):
        rows = pl.ds(pl.multiple_of(r * sub, sub), sub)
        q = q_ref[rows, :]
        q1 = jnp.where(lane < A_HEAD_DIM, q, zero) * scale
        q2 = jnp.where(lane >= A_HEAD_DIM, q, zero) * scale
        k_all = kall_ref[...]
        s1 = lax.dot_general(q1, k_all, nt, preferred_element_type=F32)
        s2 = lax.dot_general(q2, k_all, nt, preferred_element_type=F32)

        off = pl.multiple_of(shift - (qi * tq + r * sub), LANES)
        bias_meta = bias_ref[:, pl.ds(off, META_PAD)] + pad_mask
        bias_seq = bias_ref[:, pl.ds(off + META_PAD, n_keys - META_PAD)]

        def softmax_parts(s):
            s_meta = s[:, 0:META_PAD] + bias_meta
            s_seq = s[:, META_PAD:] + bias_seq
            m = jnp.maximum(jnp.max(s_meta, axis=-1, keepdims=True), jnp.max(s_seq, axis=-1, keepdims=True))
            p_meta = jnp.exp(s_meta - m)
            p_seq = jnp.exp(s_seq - m)
            l = jnp.sum(p_meta, axis=-1, keepdims=True) + jnp.sum(p_seq, axis=-1, keepdims=True)
            return p_meta.astype(BF16), p_seq.astype(BF16), l

        p1_meta, p1_seq, l1 = softmax_parts(s1)
        p2_meta, p2_seq, l2 = softmax_parts(s2)
        a1 = (1.0 / l1).astype(BF16)
        a2 = (lam / l2).astype(BF16)
        w_meta = p1_meta * a1 - p2_meta * a2
        w_seq = p1_seq * a1 - p2_seq * a2
        o = (jnp.dot(w_meta, vall_ref[0:META_PAD, :], preferred_element_type=F32)
             + jnp.dot(w_seq, vall_ref[META_PAD:, :], preferred_element_type=F32))
        ms = jnp.mean(o * o, axis=-1, keepdims=True)
        o_ref[rows, :] = (o * lax.rsqrt(ms + EPS) * g_ref[...]).astype(o_ref.dtype)
        return carry

    lax.fori_loop(0, tq // sub, one_pass, 0)


def _diff_attn(proj, meta_proj, params, subln_g, n_seq, seq, tq):
    n_tok = proj.shape[0]
    nq = seq // tq
    sub = min(ATTN_ROWS, tq)
    hw = 2 * A_HEAD_DIM
    return pl.pallas_call(
        _diff_attn_kernel,
        grid=(A_HEADS, n_seq, nq),
        in_specs=[
            pl.BlockSpec(memory_space=pltpu.SMEM),
            pl.BlockSpec((tq, hw), lambda h, b, i: (b * nq + i, h)),
            pl.BlockSpec((seq, hw), lambda h, b, i: (b, A_HEADS + h)),
            pl.BlockSpec((seq, hw), lambda h, b, i: (b, 2 * A_HEADS + h)),
            pl.BlockSpec((META_PAD, hw), lambda h, b, i: (0, A_HEADS + h)),
            pl.BlockSpec((META_PAD, hw), lambda h, b, i: (0, 2 * A_HEADS + h)),
            pl.BlockSpec((1, hw), lambda h, b, i: (0, 0)),
        ],
        out_specs=pl.BlockSpec((tq, hw), lambda h, b, i: (b * nq + i, h)),
        out_shape=jax.ShapeDtypeStruct((n_tok, A_HEADS * hw), BF16),
        scratch_shapes=[pltpu.VMEM((META_PAD + seq, hw), BF16), pltpu.VMEM((META_PAD + seq, hw), BF16),
                        pltpu.VMEM((sub, META_PAD + 2 * seq - sub), F32)],
        compiler_params=_cparams(("arbitrary", "arbitrary", "arbitrary"), 48),
        name="diff_attn",
    )(params, proj, proj, proj, meta_proj, meta_proj, subln_g)


def _split_cumsum(tri, x):
    hi = x.astype(BF16)
    lo = (x - hi.astype(F32)).astype(BF16)
    return (jnp.dot(tri, hi, preferred_element_type=F32) + jnp.dot(tri, lo, preferred_element_type=F32))


def _hgrn_chunk(q, k, logf, v_bf, state_t, reverse):
    c = q.shape[0]
    r = lax.broadcasted_iota(jnp.int32, (c, c), 0)
    s = lax.broadcasted_iota(jnp.int32, (c, c), 1)
    if reverse:
        visible = s >= r
        ref_row, last_row = c - 1 - c // 2, 0
    else:
        visible = s <= r
        ref_row, last_row = c // 2, c - 1
    tri = jnp.where(visible, 1.0, 0.0).astype(BF16)
    b = _split_cumsum(tri, logf)
    b_ref = b[ref_row:ref_row + 1, :]
    b_last = b[last_row:last_row + 1, :]
    nt = (((1,), (1,)), ((), ()))
    tn = (((0,), (0,)), ((), ()))
    qd = (q * jnp.exp(b - b_ref)).astype(BF16)
    kd = (k * jnp.exp(b_ref - b)).astype(BF16)
    scores = lax.dot_general(qd, kd, nt, preferred_element_type=F32)
    scores = jnp.where(visible, scores, 0.0).astype(BF16)
    o_intra = jnp.dot(scores, v_bf, preferred_element_type=F32)
    qe = (q * jnp.exp(b)).astype(BF16)
    o_inter = lax.dot_general(qe, state_t.astype(BF16), nt, preferred_element_type=F32)
    ks = (k * jnp.exp(b_last - b)).astype(BF16)
    d_state_t = lax.dot_general(v_bf, ks, tn, preferred_element_type=F32)
    new_state_t = state_t * jnp.exp(b_last) + d_state_t
    return o_intra + o_inter, new_state_t


def _forget_gate(logit, lb):
    f = lb + (1.0 - lb) * jax.nn.sigmoid(logit)
    return 1.0 - f, jnp.log(f)


def _chunk_sum_operators(reverse):
    g = HGRN_GROUP
    r = lax.broadcasted_iota(jnp.int32, (g, g), 0)
    s = lax.broadcasted_iota(jnp.int32, (g, g), 1)
    shift = H_CHUNK.bit_length() - 1
    same = (r >> shift) == (s >> shift)
    s_loc = s & (H_CHUNK - 1)
    if reverse:
        cum, ref = s >= r, s_loc >= H_CHUNK - 1 - H_CHUNK // 2
    else:
        cum, ref = s <= r, s_loc <= H_CHUNK // 2
    ops = [jnp.logical_and(same, cum), jnp.logical_and(same, ref), same]
    return jnp.concatenate([jnp.where(o, 1.0, 0.0) for o in ops], axis=0).astype(BF16)


def _hgrn_kernel(q_ref, ff_ref, fb_ref, v_ref, hg_ref, mff_ref, mv_ref, lbf_ref, lbb_ref, g_ref, o_ref,
                 qdf_ref, kdf_ref, qdb_ref, kdb_ref, qe_ref, ks_ref, dec_ref, ds_ref, sp_ref, acc_ref):
    seq = q_ref.shape[0]
    n_heads = q_ref.shape[1] // H_DV
    n_chunks = seq // H_CHUNK
    dk = H_EXPAND
    nt = (((1,), (1,)), ((), ()))
    tn = (((0,), (0,)), ((), ()))
    ops_f = _chunk_sum_operators(False)
    ops_b = _chunk_sum_operators(True)
    r = lax.broadcasted_iota(jnp.int32, (2 * H_CHUNK, 2 * H_CHUNK), 0)
    s = lax.broadcasted_iota(jnp.int32, (2 * H_CHUNK, 2 * H_CHUNK), 1)
    same_chunk = (r >= H_CHUNK) == (s >= H_CHUNK)
    vis_f = jnp.logical_and(same_chunk, s <= r)
    vis_b = jnp.logical_and(same_chunk, s >= r)
    mrow = lax.broadcasted_iota(jnp.int32, (H_CHUNK, 1), 0)
    is_meta = mrow >= H_CHUNK - N_META
    zeros_q = jnp.zeros((H_CHUNK, dk), F32)
    state0 = jnp.zeros((H_DV, dk), F32)
    chunks_per_group = HGRN_GROUP // H_CHUNK

    def chunk_rows(x):
        return jnp.concatenate([x[c * H_CHUNK:c * H_CHUNK + 1] for c in range(chunks_per_group)], axis=0)

    def per_chunk(y):
        return jnp.concatenate([jnp.broadcast_to(y[c:c + 1], (H_CHUNK, y.shape[1]))
                                for c in range(chunks_per_group)], axis=0)

    def dec_row(c):
        return pl.ds(pl.multiple_of(c * DEC_ROWS, DEC_ROWS), DEC_ROWS)

    for j in range(n_heads):
        head = pl.ds(j * H_DV, H_DV)
        lbf = lbf_ref[:, head]
        lbb = lbb_ref[:, head]

        def prepare(g, carry):
            rows = pl.ds(pl.multiple_of(g * HGRN_GROUP, HGRN_GROUP), HGRN_GROUP)
            q = jax.nn.silu(q_ref[rows, head].astype(F32))
            for logit_ref, lb, ops, qd_ref, kd_ref, col in ((ff_ref, lbf, ops_f, qdf_ref, kdf_ref, 0),
                                                           (fb_ref, lbb, ops_b, qdb_ref, kdb_ref, dk)):
                k, logf = _forget_gate(logit_ref[rows, head].astype(F32), lb)
                hi = logf.astype(BF16)
                lo = (logf - hi.astype(F32)).astype(BF16)
                sums = jnp.dot(ops, jnp.concatenate([hi, lo], axis=1), preferred_element_type=F32)
                sums = sums[:, 0:dk] + sums[:, dk:]
                b = sums[0:HGRN_GROUP]
                b_mid = sums[HGRN_GROUP:2 * HGRN_GROUP]
                b_all = sums[2 * HGRN_GROUP:]
                e_mid = per_chunk(jnp.exp(chunk_rows(b_mid)))
                e_all = jnp.exp(chunk_rows(b_all))
                e_rest = per_chunk(jnp.exp(chunk_rows(b_all) - chunk_rows(b_mid)))
                qd = q * jnp.exp(b - b_mid)
                kd = k * jnp.exp(b_mid - b)
                qd_ref[rows, :] = qd.astype(BF16)
                kd_ref[rows, :] = kd.astype(BF16)
                qe_ref[rows, col:col + dk] = (qd * e_mid).astype(BF16)
                ks_ref[rows, col:col + dk] = (kd * e_rest).astype(BF16)
                group_rows = chunks_per_group * DEC_ROWS
                dec_ref[pl.ds(pl.multiple_of(g * group_rows, group_rows), group_rows), col:col + dk] = (
                    jnp.concatenate([jnp.broadcast_to(e_all[c:c + 1], (DEC_ROWS, dk))
                                     for c in range(chunks_per_group)], axis=0))
            return carry

        lax.fori_loop(0, seq // HGRN_GROUP, prepare, 0, unroll=2)

        def intra(p, carry):
            rows = pl.ds(pl.multiple_of(p * (2 * H_CHUNK), 2 * H_CHUNK), 2 * H_CHUNK)
            s_f = lax.dot_general(qdf_ref[rows, :], kdf_ref[rows, :], nt, preferred_element_type=F32)
            s_b = lax.dot_general(qdb_ref[rows, :], kdb_ref[rows, :], nt, preferred_element_type=F32)
            w = (jnp.where(vis_f, s_f, 0.0) + jnp.where(vis_b, s_b, 0.0)).astype(BF16)
            acc_ref[rows, head] = jnp.dot(w, v_ref[rows, head], preferred_element_type=F32)
            for half in range(2):
                c = 2 * p + half
                rows_c = pl.ds(pl.multiple_of(c * H_CHUNK, H_CHUNK), H_CHUNK)
                ds_ref[pl.ds(pl.multiple_of(c * H_DV, H_DV), H_DV), :] = lax.dot_general(
                    v_ref[rows_c, head], ks_ref[rows_c, :], tn, preferred_element_type=F32)
            return carry

        lax.fori_loop(0, n_chunks // 2, intra, 0, unroll=HGRN_UNROLL // 2)

        mk, mlogf = _forget_gate(mff_ref[:, head].astype(F32), lbf)
        mk = jnp.where(is_meta, mk, 0.0)
        mlogf = jnp.where(is_meta, mlogf, 0.0)
        state_f0 = _hgrn_chunk(zeros_q, mk, mlogf, mv_ref[:, head], state0, reverse=False)[1]

        def scan(i, carry):
            st_f, st_b = carry
            cf = i
            cb = n_chunks - 1 - i
            blk_f = pl.ds(pl.multiple_of(cf * H_DV, H_DV), H_DV)
            blk_b = pl.ds(pl.multiple_of(cb * H_DV, H_DV), H_DV)
            sp_ref[blk_f, 0:dk] = st_f.astype(BF16)
            sp_ref[blk_b, dk:] = st_b.astype(BF16)
            st_f = st_f * dec_ref[dec_row(cf), 0:dk][0:1] + ds_ref[blk_f, 0:dk]
            st_b = st_b * dec_ref[dec_row(cb), dk:][0:1] + ds_ref[blk_b, dk:]
            return st_f, st_b

        lax.fori_loop(0, n_chunks, scan, (state_f0, state0))

        def inter(c, carry):
            rows = pl.ds(pl.multiple_of(c * H_CHUNK, H_CHUNK), H_CHUNK)
            state = sp_ref[pl.ds(pl.multiple_of(c * H_DV, H_DV), H_DV), :]
            acc_ref[rows, head] += lax.dot_general(qe_ref[rows, :], state, nt, preferred_element_type=F32)
            return carry

        lax.fori_loop(0, n_chunks, inter, 0, unroll=HGRN_UNROLL)

    for j in range(n_heads):
        head = pl.ds(j * H_DV, H_DV)
        o = acc_ref[:, head]
        ms = jnp.mean(o * o, axis=-1, keepdims=True)
        normed = o * lax.rsqrt(ms + EPS) * g_ref[...]
        o_ref[:, head] = (normed * jax.nn.sigmoid(hg_ref[:, head].astype(F32))).astype(o_ref.dtype)


def _hgrn(proj, meta_proj, lb_fwd, lb_bwd, norm_g, n_seq, seq):
    n_tok = proj.shape[0]
    hps = HGRN_HEADS_PER_STEP
    w = hps * H_DV
    col0 = 3 * A_HEADS * (2 * A_HEAD_DIM) // w
    per_sec = H_HEADS // hps
    meta_blk = META_PAD // H_CHUNK - 1

    def sec(k):
        return pl.BlockSpec((seq, w), lambda b, h, k=k: (b, col0 + k * per_sec + h))

    def meta_sec(k):
        return pl.BlockSpec((H_CHUNK, w), lambda b, h, k=k: (meta_blk, col0 + k * per_sec + h))

    return pl.pallas_call(
        _hgrn_kernel,
        grid=(n_seq, per_sec),
        in_specs=[
            sec(0), sec(1), sec(2), sec(3), sec(4),
            meta_sec(1), meta_sec(3),
            pl.BlockSpec((1, w), lambda b, h: (0, h)),
            pl.BlockSpec((1, w), lambda b, h: (0, h)),
            pl.BlockSpec((1, H_DV), lambda b, h: (0, 0)),
        ],
        out_specs=pl.BlockSpec((seq, w), lambda b, h: (b, h)),
        out_shape=jax.ShapeDtypeStruct((n_tok, H_HEADS * H_DV), BF16),
        scratch_shapes=[
            pltpu.VMEM((seq, H_EXPAND), BF16), pltpu.VMEM((seq, H_EXPAND), BF16),
            pltpu.VMEM((seq, H_EXPAND), BF16), pltpu.VMEM((seq, H_EXPAND), BF16),
            pltpu.VMEM((seq, 2 * H_EXPAND), BF16),
            pltpu.VMEM((seq, 2 * H_EXPAND), BF16),
            pltpu.VMEM((seq // H_CHUNK * DEC_ROWS, 2 * H_EXPAND), F32),
            pltpu.VMEM((seq // H_CHUNK * H_DV, 2 * H_EXPAND), F32),
            pltpu.VMEM((seq // H_CHUNK * H_DV, 2 * H_EXPAND), BF16),
            pltpu.VMEM((seq, w), F32),
        ],
        compiler_params=_cparams(("parallel", "parallel"), 48),
        name="hgrn",
    )(proj, proj, proj, proj, proj, meta_proj, meta_proj, lb_fwd, lb_bwd, norm_g)


def _merge_kernel(ao_ref, ho_ref, ga_ref, gh_ref, xp_ref, xs_ref, wa_ref, wh_ref, wo_ref, lng_ref, lnb_ref,
                  l1g_ref, l1b_ref, hf_ref, hp_ref, *, tp):
    pa = jnp.dot(ao_ref[...], wa_ref[...], preferred_element_type=F32)
    ph = jnp.dot(ho_ref[...], wh_ref[...], preferred_element_type=F32)
    merged = (jax.nn.sigmoid(ga_ref[...].astype(F32)) * pa + jax.nn.sigmoid(gh_ref[...].astype(F32)) * ph)
    t = jnp.dot(merged.astype(BF16), wo_ref[...], preferred_element_type=F32)

    def finish(x_ref):
        x_in = _layer_norm_f32(x_ref[...], lng_ref[...], lnb_ref[...])
        h = _layer_norm_f32(DN_ALPHA * x_in + t, l1g_ref[...], l1b_ref[...])
        hf_ref[...] = h
        hp_ref[...] = _pack_bf16_pair(h)

    @pl.when(pl.program_id(0) < tp)
    def _():
        finish(xp_ref)

    @pl.when(pl.program_id(0) >= tp)
    def _():
        finish(xs_ref)


def _merge(ao, ho, proj, xp, xs, wa, wh, wo, ln_g, ln_b, l1g, l1b, tm):
    d = xp.shape[1]
    tp, ts = xp.shape[0] // tm, xs.shape[0] // tm
    n_tok = (tp + ts) * tm
    gate_blk0 = (proj.shape[1] - 2 * d) // d
    vec = pl.BlockSpec((1, d), lambda i: (0, 0))
    xp_spec, xs_spec = _two_group_specs(tm, d, tp, False)
    return pl.pallas_call(
        functools.partial(_merge_kernel, tp=tp),
        grid=(tp + ts,),
        in_specs=[
            pl.BlockSpec((tm, ao.shape[1]), lambda i: (i, 0)),
            pl.BlockSpec((tm, ho.shape[1]), lambda i: (i, 0)),
            pl.BlockSpec((tm, d), lambda i: (i, gate_blk0)),
            pl.BlockSpec((tm, d), lambda i: (i, gate_blk0 + 1)),
            xp_spec, xs_spec,
            _resident(wa.shape), _resident(wh.shape), _resident(wo.shape),
            vec, vec, vec, vec,
        ],
        out_specs=[pl.BlockSpec((tm, d), lambda i: (i, 0)), pl.BlockSpec((tm, d // 2), lambda i: (i, 0))],
        out_shape=[jax.ShapeDtypeStruct((n_tok, d), F32), jax.ShapeDtypeStruct((n_tok, d // 2), U32)],
        compiler_params=_cparams(("parallel",), 56),
        name="merge",
    )(ao, ho, proj, proj, xp, xs, wa, wh, wo, ln_g, ln_b, l1g, l1b)


def _first_argmax(x, iota, size):
    m = jnp.max(x, axis=0, keepdims=True)
    idx = jnp.min(jnp.where(x == m, iota, size), axis=0, keepdims=True)
    return m, idx


def _router_kernel(h_ref, wrt_ref, bias_ref, idx_ref, gate_ref, rank_ref, cnt_ref, base_ref):
    tm = h_ref.shape[0]
    half = h_ref.shape[1]

    @pl.when(pl.program_id(0) == 0)
    def _():
        base_ref[...] = jnp.zeros_like(base_ref)

    nt = (((1,), (1,)), ((), ()))
    h_lo, h_hi = _unpack_bf16_pair(h_ref[...])
    logits = (lax.dot_general(wrt_ref[:, 0:half], h_lo, nt, preferred_element_type=F32)
              + lax.dot_general(wrt_ref[:, half:], h_hi, nt, preferred_element_type=F32))
    scores = jax.nn.sigmoid(logits)
    biased = scores + bias_ref[...]
    neg_inf = jnp.asarray(-jnp.inf, F32)

    iota_g = lax.broadcasted_iota(jnp.int32, (GROUP_SIZE, tm), 0)
    iota_n = lax.broadcasted_iota(jnp.int32, (N_GROUPS, tm), 0)
    grp = jnp.full((N_GROUPS, tm), neg_inf, F32)
    for g in range(N_GROUPS):
        xg = biased[g * GROUP_SIZE:(g + 1) * GROUP_SIZE, :]
        m1, i1 = _first_argmax(xg, iota_g, GROUP_SIZE)
        m2 = jnp.max(jnp.where(iota_g == i1, neg_inf, xg), axis=0, keepdims=True)
        grp = jnp.where(iota_n == g, m1 + m2, grp)

    keep_f = jnp.zeros((N_GROUPS, tm), F32)
    for _ in range(TOPK_GROUPS):
        _, ig = _first_argmax(grp, iota_n, N_GROUPS)
        sel = iota_n == ig
        keep_f = jnp.where(sel, 1.0, keep_f)
        grp = jnp.where(sel, neg_inf, grp)

    masked = jnp.concatenate(
        [jnp.where(keep_f[g:g + 1, :] > 0.5, biased[g * GROUP_SIZE:(g + 1) * GROUP_SIZE, :], neg_inf)
         for g in range(N_GROUPS)], axis=0)

    iota_e = lax.broadcasted_iota(jnp.int32, (N_EXPERTS, tm), 0)
    chosen = jnp.zeros((N_EXPERTS, tm), F32)
    idxs, gates = [], []
    for _ in range(TOP_K):
        _, ie = _first_argmax(masked, iota_e, N_EXPERTS)
        sel = iota_e == ie
        gates.append(jnp.sum(jnp.where(sel, scores, 0.0), axis=0, keepdims=True))
        idxs.append(ie)
        chosen = jnp.where(sel, 1.0, chosen)
        masked = jnp.where(sel, neg_inf, masked)

    denom = gates[0]
    for gk in gates[1:]:
        denom = denom + gk

    r = lax.broadcasted_iota(jnp.int32, (tm, tm), 0)
    c = lax.broadcasted_iota(jnp.int32, (tm, tm), 1)
    before = jnp.where(r < c, 1.0, 0.0).astype(BF16)
    pos = jnp.dot(chosen.astype(BF16), before, preferred_element_type=F32) + base_ref[...]
    for k in range(TOP_K):
        sel = iota_e == idxs[k]
        rank_ref[k:k + 1, :] = jnp.sum(jnp.where(sel, pos, 0.0), axis=0, keepdims=True).astype(jnp.int32)
        idx_ref[k:k + 1, :] = idxs[k]
        gate_ref[k:k + 1, :] = gates[k] / denom * ROUTED_SCALE

    base_ref[...] = base_ref[...] + jnp.sum(chosen, axis=1, keepdims=True)
    cnt_ref[...] = base_ref[...]


def _router(h_pk, w_router_t, bias_col, tm):
    n_tok, half = h_pk.shape
    row_blk = pl.BlockSpec((TOP_K, tm), lambda i: (0, i))
    return pl.pallas_call(
        _router_kernel,
        grid=(n_tok // tm,),
        in_specs=[
            pl.BlockSpec((tm, half), lambda i: (i, 0)),
            _resident(w_router_t.shape),
            pl.BlockSpec((N_EXPERTS, 1), lambda i: (0, 0)),
        ],
        out_specs=[row_blk, row_blk, row_blk, pl.BlockSpec((N_EXPERTS, 1), lambda i: (0, 0))],
        out_shape=[
            jax.ShapeDtypeStruct((TOP_K, n_tok), jnp.int32),
            jax.ShapeDtypeStruct((TOP_K, n_tok), F32),
            jax.ShapeDtypeStruct((TOP_K, n_tok), jnp.int32),
            jax.ShapeDtypeStruct((N_EXPERTS, 1), F32),
        ],
        scratch_shapes=[pltpu.VMEM((N_EXPERTS, 1), F32)],
        compiler_params=_cparams(("arbitrary",), 32),
        name="router",
    )(h_pk, w_router_t, bias_col)


def _dest_kernel(idx_ref, rank_ref, start_ref, dest_ref):
    tm = idx_ref.shape[1]
    iota_e = lax.broadcasted_iota(jnp.int32, (N_EXPERTS, tm), 0)
    start = start_ref[...]
    for k in range(TOP_K):
        sel = iota_e == idx_ref[k:k + 1, :]
        base = jnp.sum(jnp.where(sel, start, 0.0), axis=0, keepdims=True)
        dest_ref[k:k + 1, :] = base.astype(jnp.int32) + rank_ref[k:k + 1, :]


def _dest(idx, rank, start_col, tm):
    n_tok = idx.shape[1]
    row_blk = pl.BlockSpec((TOP_K, tm), lambda i: (0, i))
    return pl.pallas_call(
        _dest_kernel,
        grid=(n_tok // tm,),
        in_specs=[row_blk, row_blk, pl.BlockSpec((N_EXPERTS, 1), lambda i: (0, 0))],
        out_specs=row_blk,
        out_shape=jax.ShapeDtypeStruct((TOP_K, n_tok), jnp.int32),
        compiler_params=_cparams(("parallel",), 32),
        name="dest",
    )(idx, rank, start_col)


def _dispatch_kernel(pstart_ref, cnt_ref, tail_ref, dest_ref, h_ref, xs_ref, zero_ref, sem, pad_sem, tail_sem):
    tm = h_ref.shape[0]

    def pad_copy(e, j):
        return pltpu.make_async_copy(zero_ref.at[pl.ds(0, 1)],
                                     xs_ref.at[pl.ds(pstart_ref[e] + cnt_ref[e] + j, 1)], pad_sem)

    def tail_copy():
        return pltpu.make_async_copy(
            zero_ref, xs_ref.at[pl.ds(pl.multiple_of(tail_ref[0], ROW_ALIGN), EXPERT_ROWS)], tail_sem)

    def for_each_pad_row(action):
        def body(e, carry):
            n_pad = (ROW_ALIGN - cnt_ref[e] % ROW_ALIGN) % ROW_ALIGN
            for j in range(ROW_ALIGN - 1):
                @pl.when(j < n_pad)
                def _():
                    action(pad_copy(e, j))
            return carry
        lax.fori_loop(0, N_EXPERTS, body, 0)

    @pl.when(pl.program_id(0) == 0)
    def _():
        zero_ref[...] = jnp.zeros_like(zero_ref)
        tail_copy().start()
        for_each_pad_row(lambda c: c.start())
        for_each_pad_row(lambda c: c.wait())
        tail_copy().wait()

    def row_copy(r, k):
        return pltpu.make_async_copy(h_ref.at[pl.ds(r, 1)], xs_ref.at[pl.ds(dest_ref[0, k * tm + r], 1)], sem)

    def start(r, carry):
        for k in range(TOP_K):
            row_copy(r, k).start(priority=k % DMA_THREADS)
        return carry

    def wait(r, carry):
        for k in range(TOP_K):
            row_copy(r, k).wait()
        return carry

    lax.fori_loop(0, tm, start, 0)
    lax.fori_loop(0, tm, wait, 0)


def _dispatch(h_pk, dest_tiles, pstart, counts, tail, n_rows, tm):
    n_tok, half = h_pk.shape
    grid_spec = pltpu.PrefetchScalarGridSpec(
        num_scalar_prefetch=3,
        grid=(n_tok // tm,),
        in_specs=[
            pl.BlockSpec((None, 1, TOP_K * tm), lambda i, *_: (i, 0, 0), memory_space=pltpu.SMEM),
            pl.BlockSpec((tm, half), lambda i, *_: (i, 0)),
        ],
        out_specs=pl.BlockSpec(memory_space=pl.ANY),
        scratch_shapes=[pltpu.VMEM((EXPERT_ROWS, half), U32), pltpu.SemaphoreType.DMA,
                        pltpu.SemaphoreType.DMA, pltpu.SemaphoreType.DMA],
    )
    return pl.pallas_call(
        _dispatch_kernel,
        grid_spec=grid_spec,
        out_shape=jax.ShapeDtypeStruct((n_rows, half), U32),
        compiler_params=_cparams(("arbitrary",), 32, disable_bounds_checks=True),
        name="dispatch",
    )(pstart, counts, tail, dest_tiles, h_pk)


def _experts_kernel(row_ref, e_ref, valid_ref, newe_ref, slot_ref, nexte_ref,
                    x_hbm, wg_hbm, wu_hbm, wd_hbm, y_hbm,
                    x_buf, y_buf, wg_buf, wu_buf, wd_buf, wgu_bf, wd_bf, x_sems, y_sems, sems):
    n_steps = pl.num_programs(0) * WINDOWS_PER_STEP
    ff = wg_buf.shape[2]

    def window(step):
        return pl.ds(pl.multiple_of(row_ref[step], ROW_ALIGN), EXPERT_ROWS)

    def x_copy(step, slot):
        return pltpu.make_async_copy(x_hbm.at[window(step)], x_buf.at[slot], x_sems.at[slot])

    def y_copy(step, slot):
        return pltpu.make_async_copy(y_buf.at[slot], y_hbm.at[window(step)], y_sems.at[slot])

    def weight_copies(e, slot):
        copies = []
        for m, (src, dst) in enumerate(((wg_hbm, wg_buf), (wu_hbm, wu_buf), (wd_hbm, wd_buf))):
            rows = dst.shape[1] // WEIGHT_DMA_CHUNKS
            for c in range(WEIGHT_DMA_CHUNKS):
                part = pl.ds(c * rows, rows)
                copies.append(pltpu.make_async_copy(src.at[0, e, part], dst.at[slot, part],
                                                    sems.at[slot, m * WEIGHT_DMA_CHUNKS + c]))
        return copies

    def process(i, cur):
        @pl.when(i == 0)
        def _():
            x_copy(0, 0).start()

        @pl.when(newe_ref[i] == 1)
        def _():
            slot = slot_ref[i]
            e = e_ref[i]

            @pl.when(i == 0)
            def _():
                for c in weight_copies(e, slot):
                    c.start(priority=WEIGHT_DMA_THREAD)

            for c in weight_copies(e, slot):
                c.wait()
            nxt = nexte_ref[i]

            @pl.when(nxt >= 0)
            def _():
                for c in weight_copies(nxt, 1 - slot):
                    c.start(priority=WEIGHT_DMA_THREAD)

            wgu_bf[:, 0:ff] = wg_buf[slot].astype(BF16)
            wgu_bf[:, ff:] = wu_buf[slot].astype(BF16)
            wd_bf[...] = wd_buf[slot].astype(BF16)

        @pl.when(valid_ref[i] == 1)
        def _():
            nxt_step = jnp.minimum(i + 1, n_steps - 1)
            has_next = jnp.logical_and(i + 1 < n_steps, valid_ref[nxt_step] == 1)
            x_copy(i, cur).wait()

            @pl.when(has_next)
            def _():
                x_copy(nxt_step, 1 - cur).start()

            gu = _dot_packed(x_buf[cur], wgu_bf)
            hid = (jax.nn.silu(gu[:, 0:ff]) * gu[:, ff:]).astype(BF16)
            y_buf[cur] = _pack_bf16_pair(jnp.dot(hid, wd_bf[...], preferred_element_type=F32))

            @pl.when(i > 0)
            def _():
                y_copy(i - 1, 1 - cur).wait()

            y_copy(i, cur).start()

            @pl.when(jnp.logical_not(has_next))
            def _():
                y_copy(i, cur).wait()

    for u in range(WINDOWS_PER_STEP):
        process(pl.program_id(0) * WINDOWS_PER_STEP + u, u % 2)


def _experts(x_sorted, steps, w_gate, w_up, w_down):
    n_rows, half = x_sorted.shape
    d = 2 * half
    ff = w_gate.shape[-1]
    hbm = pl.BlockSpec(memory_space=pl.ANY)
    grid_spec = pltpu.PrefetchScalarGridSpec(
        num_scalar_prefetch=len(steps),
        grid=(steps[0].shape[0] // WINDOWS_PER_STEP,),
        in_specs=[hbm, hbm, hbm, hbm],
        out_specs=hbm,
        scratch_shapes=[pltpu.VMEM((2, EXPERT_ROWS, half), U32), pltpu.VMEM((2, EXPERT_ROWS, half), U32),
                        pltpu.VMEM((2, d, ff), F32), pltpu.VMEM((2, d, ff), F32), pltpu.VMEM((2, ff, d), F32),
                        pltpu.VMEM((d, 2 * ff), BF16), pltpu.VMEM((ff, d), BF16),
                        pltpu.SemaphoreType.DMA((2,)), pltpu.SemaphoreType.DMA((2,)),
                        pltpu.SemaphoreType.DMA((2, 3 * WEIGHT_DMA_CHUNKS))],
    )
    return pl.pallas_call(
        _experts_kernel,
        grid_spec=grid_spec,
        out_shape=jax.ShapeDtypeStruct((n_rows, half), U32),
        compiler_params=_cparams(("arbitrary",), 52, disable_bounds_checks=True),
        name="experts",
    )(*steps, x_sorted, w_gate, w_up, w_down)


def _final_kernel(dest_ref, gate_ref, hf_ref, hp_ref, y_ref, wgu_ref, wd_ref, g_ref, b_ref, op_ref, os_ref,
                  ybuf, sem, *, tp):
    tm = hf_ref.shape[0]
    ff = wd_ref.shape[0]

    def row_copy(r, k):
        return pltpu.make_async_copy(y_ref.at[pl.ds(dest_ref[0, k * tm + r], 1)], ybuf.at[k, pl.ds(r, 1)], sem)

    def start(r, carry):
        for k in range(TOP_K):
            row_copy(r, k).start(priority=k % DMA_THREADS)
        return carry

    def wait(r, carry):
        for k in range(TOP_K):
            row_copy(r, k).wait()
        return carry

    lax.fori_loop(0, tm, start, 0)

    gu = _dot_packed(hp_ref[...], wgu_ref)
    hid = (jax.nn.silu(gu[:, 0:ff]) * gu[:, ff:]).astype(BF16)
    acc = DN_ALPHA * hf_ref[...] + jnp.dot(hid, wd_ref[...], preferred_element_type=F32)

    lax.fori_loop(0, tm, wait, 0)
    gate = gate_ref[...]
    half = ybuf.shape[2]
    acc_lo, acc_hi = acc[:, :half], acc[:, half:]
    for k in range(TOP_K):
        packed = ybuf[k]
        g_k = gate[:, k:k + 1]
        acc_lo = acc_lo + lax.bitcast_convert_type(packed << 16, F32) * g_k
        acc_hi = acc_hi + lax.bitcast_convert_type(packed & jnp.uint32(HIGH_HALF), F32) * g_k
    out = _layer_norm_f32(jnp.concatenate([acc_lo, acc_hi], axis=1), g_ref[...], b_ref[...])

    @pl.when(pl.program_id(0) < tp)
    def _():
        op_ref[...] = out

    @pl.when(pl.program_id(0) >= tp)
    def _():
        os_ref[...] = out


def _final(h_f32, h_pk, y_sorted, dest_tiles, gate_t, w_gu, w_d, g, b, n_prompt, tm):
    n_tok, d = h_f32.shape
    tp = n_prompt // tm
    ts = n_tok // tm - tp
    vec = pl.BlockSpec((1, d), lambda i: (0, 0))
    op_spec, os_spec = _two_group_specs(tm, d, tp, False)
    return pl.pallas_call(
        functools.partial(_final_kernel, tp=tp),
        grid=(tp + ts,),
        in_specs=[
            pl.BlockSpec((None, 1, TOP_K * tm), lambda i: (i, 0, 0), memory_space=pltpu.SMEM),
            pl.BlockSpec((tm, TOP_K), lambda i: (i, 0)),
            pl.BlockSpec((tm, d), lambda i: (i, 0)),
            pl.BlockSpec((tm, d // 2), lambda i: (i, 0)),
            pl.BlockSpec(memory_space=pl.ANY),
            _resident(w_gu.shape), _resident(w_d.shape), vec, vec,
        ],
        out_specs=[op_spec, os_spec],
        out_shape=[jax.ShapeDtypeStruct((tp * tm, d), F32), jax.ShapeDtypeStruct((ts * tm, d), F32)],
        scratch_shapes=[pltpu.VMEM((TOP_K, tm, d // 2), U32), pltpu.SemaphoreType.DMA],
        compiler_params=_cparams(("arbitrary",), 48, disable_bounds_checks=True),
        name="final",
    )(dest_tiles, gate_t, h_f32, h_pk, y_sorted, w_gu, w_d, g, b)


def _lower_bounds(gamma):
    return jnp.cumsum(jax.nn.softmax(gamma.astype(F32), axis=0), axis=0)


def _tile_major(dest, tm):
    k, n = dest.shape
    return jnp.transpose(dest.reshape(k, n // tm, tm), (1, 0, 2)).reshape(n // tm, 1, k * tm)


def _expert_steps(counts, n_assign):
    rows = EXPERT_ROWS
    n_steps = (n_assign + N_EXPERTS * (ROW_ALIGN - 1)) // rows + N_EXPERTS
    n_steps = (n_steps + WINDOWS_PER_STEP - 1) // WINDOWS_PER_STEP * WINDOWS_PER_STEP
    padded = (counts + ROW_ALIGN - 1) // ROW_ALIGN * ROW_ALIGN
    pend = jnp.cumsum(padded)
    pstart = pend - padded
    per_e = (padded + rows - 1) // rows
    st_end = jnp.cumsum(per_e)
    st_start = st_end - per_e
    total = st_end[-1]
    i = jnp.arange(n_steps, dtype=jnp.int32)
    ic = jnp.minimum(i, total - 1)
    e = jnp.sum((st_end[None, :] <= ic[:, None]).astype(jnp.int32), axis=1)
    j = ic - st_start[e]
    row0 = pstart[e] + jnp.minimum(j * rows, jnp.maximum(padded[e] - rows, 0))
    valid = i < total
    prev_e = jnp.concatenate([jnp.full((1,), -1, jnp.int32), e[:-1].astype(jnp.int32)])
    new_e = jnp.where(jnp.logical_and(valid, e != prev_e), 1, 0)
    nonempty = counts > 0
    slot = (jnp.cumsum(nonempty.astype(jnp.int32)) - 1) % 2
    ids = jnp.arange(N_EXPERTS, dtype=jnp.int32)
    later = jnp.concatenate([jnp.where(nonempty, ids, N_EXPERTS)[1:], jnp.full((1,), N_EXPERTS, jnp.int32)])
    nxt = jnp.flip(lax.cummin(jnp.flip(later)))
    nxt = jnp.where(nxt < N_EXPERTS, nxt, -1)
    as_i32 = lambda v: v.astype(jnp.int32)
    steps = (as_i32(row0), as_i32(e), as_i32(valid), as_i32(new_e), as_i32(slot[e]), as_i32(nxt[e]))
    n_rows = n_assign + N_EXPERTS * (ROW_ALIGN - 1) + rows
    return as_i32(pstart), as_i32(pend[-1:]), n_rows, steps


def kernel(x_prompt, x_sample, meta_tokens, ln_in_g, ln_in_b, hg_gamma_fwd, hg_gamma_bwd, w_in, lambda_q1, lambda_k1, lambda_q2, lambda_k2, attn_subln_g, hg_norm_g, w_branch_attn, w_branch_hgrn, w_out, ln1_g, ln1_b, w_router, router_bias, w_sh_gate, w_sh_up, w_sh_down, w_ex_gate, w_ex_up, w_ex_down, ln2_g, ln2_b):
    bp, seq, d = x_prompt.shape
    bs, seq_s, _ = x_sample.shape
    assert seq == seq_s and w_in.shape[0] == DEPTH
    n_seq = bp + bs
    n_tok = n_seq * seq
    layer = 0

    xp = x_prompt.reshape(bp * seq, d)
    xs = x_sample.reshape(bs * seq, d)
    row = lambda v: v.reshape(1, -1).astype(F32)
    ln_g, ln_b = row(ln_in_g), row(ln_in_b)
    w_in_bf = w_in[layer].astype(BF16)

    proj = _ln_matmul(xp, xs, ln_g, ln_b, w_in_bf, tm=min(1024, bs * seq), tn=1024)
    meta = meta_tokens.astype(F32)
    meta_proj = _ln_matmul(meta, meta, ln_g, ln_b, w_in_bf, tm=N_META, tn=1024)[:N_META]
    meta_proj = jnp.pad(meta_proj, ((META_PAD - N_META, 0), (0, 0)))

    lam = (jnp.exp(jnp.sum(lambda_q1[layer].astype(F32) * lambda_k1[layer].astype(F32)))
           - jnp.exp(jnp.sum(lambda_q2[layer].astype(F32) * lambda_k2[layer].astype(F32))) + LAMBDA_INIT)
    slopes = 2.0 ** (-8.0 * jnp.arange(1, A_HEADS + 1, dtype=F32) / A_HEADS)
    attn_params = jnp.concatenate([lam.reshape(1), slopes]).astype(F32)
    subln_g = row(attn_subln_g[layer]) * (1.0 - LAMBDA_INIT)
    ao = _diff_attn(proj, meta_proj, attn_params, subln_g, n_seq, seq, tq=min(512, seq))

    lb_fwd = row(_lower_bounds(hg_gamma_fwd)[layer])
    lb_bwd = row(_lower_bounds(hg_gamma_bwd)[layer])
    ho = _hgrn(proj, meta_proj, lb_fwd, lb_bwd, row(hg_norm_g[layer]), n_seq, seq)

    h_f32, h_pk = _merge(ao, ho, proj, xp, xs, w_branch_attn[layer].astype(BF16),
                         w_branch_hgrn[layer].astype(BF16), w_out[layer].astype(BF16), ln_g, ln_b,
                         row(ln1_g[layer]), row(ln1_b[layer]), tm=min(256, bs * seq))

    idx, gate, rank, counts = _router(h_pk, w_router[layer].T.astype(BF16),
                                      router_bias[layer].astype(F32).reshape(N_EXPERTS, 1), tm=min(512, n_tok))
    counts = counts[:, 0].astype(jnp.int32)
    pstart, tail, n_rows, steps = _expert_steps(counts, n_tok * TOP_K)
    dest = _dest(idx, rank, pstart.astype(F32).reshape(N_EXPERTS, 1), tm=min(2048, n_tok))

    tm_d = min(256, n_tok)
    x_sorted = _dispatch(h_pk, _tile_major(dest, tm_d), pstart, counts, tail, n_rows, tm_d)
    y_sorted = _experts(x_sorted, steps, w_ex_gate, w_ex_up, w_ex_down)

    w_sh_gu = jnp.concatenate([w_sh_gate[layer], w_sh_up[layer]], axis=1).astype(BF16)
    tm_f = min(256, bs * seq)
    y_prompt, y_sample = _final(h_f32, h_pk, y_sorted, _tile_major(dest, tm_f), gate.T, w_sh_gu,
                                w_sh_down[layer].astype(BF16), row(ln2_g[layer]), row(ln2_b[layer]),
                                bp * seq, tm_f)
    return (y_prompt.reshape(bp, seq, d), y_sample.reshape(bs, seq, d))
```

```python
import functools
import math

import jax
import jax.numpy as jnp
from jax import lax
from jax.experimental import pallas as pl
from jax.experimental.pallas import tpu as pltpu

F32 = jnp.float32
BF16 = jnp.bfloat16
U32 = jnp.uint32

N_META = 16
A_HEADS = 8
A_HEAD_DIM = 64
H_HEADS = 8
H_EXPAND = 128
H_DV = 128
H_CHUNK = 64
N_EXPERTS = 256
TOP_K = 8
N_GROUPS = 8
TOPK_GROUPS = 4
GROUP_SIZE = N_EXPERTS // N_GROUPS
ROUTED_SCALE = 2.5
DEPTH = 1
DN_ALPHA = (2 * DEPTH) ** 0.25
EPS = 1e-5
LAMBDA_INIT = 0.8 - 0.6 * math.exp(-0.3 * 0)

LANES = 128
HGRN_HEADS_PER_STEP = 4
HGRN_GROUP = 256
DEC_ROWS = 8
HGRN_UNROLL = 16
META_PAD = 128
NEG_BIG = -1e30
EXPERT_ROWS = 256
WEIGHT_DMA_CHUNKS = 4
DMA_THREADS = 2
WEIGHT_DMA_THREAD = 1
ROW_ALIGN = 8
WINDOWS_PER_STEP = 2
ATTN_ROWS = 256
HIGH_HALF = 0xFFFF0000


def _cparams(semantics, vmem_mb, **kw):
    return pltpu.CompilerParams(dimension_semantics=semantics, vmem_limit_bytes=vmem_mb * 1024 * 1024, **kw)


def _layer_norm_f32(x, g, b):
    mu = jnp.mean(x, axis=-1, keepdims=True)
    xc = x - mu
    var = jnp.mean(xc * xc, axis=-1, keepdims=True)
    return xc * lax.rsqrt(var + EPS) * g + b


def _resident(shape):
    nd = len(shape)
    return pl.BlockSpec(shape, lambda *_: (0,) * nd, pipeline_mode=pl.Buffered(1))


def _pack_bf16_pair(x):
    k = x.shape[1] // 2
    lo = lax.bitcast_convert_type(x[:, :k].astype(BF16).astype(F32), U32) >> 16
    hi = lax.bitcast_convert_type(x[:, k:].astype(BF16).astype(F32), U32) & jnp.uint32(HIGH_HALF)
    return lo | hi


def _unpack_bf16_pair(p):
    lo = lax.bitcast_convert_type(p << 16, F32).astype(BF16)
    hi = lax.bitcast_convert_type(p & jnp.uint32(HIGH_HALF), F32).astype(BF16)
    return lo, hi


def _dot_packed(p, w_ref):
    k = p.shape[1]
    lo, hi = _unpack_bf16_pair(p)
    return (jnp.dot(lo, w_ref[0:k, :], preferred_element_type=F32)
            + jnp.dot(hi, w_ref[k:, :], preferred_element_type=F32))


def _two_group_specs(tm, d, tp, extra_grid_axes):
    if extra_grid_axes:
        return (pl.BlockSpec((tm, d), lambda i, j: (jnp.minimum(i, tp - 1), 0), pipeline_mode=pl.Buffered(1)),
                pl.BlockSpec((tm, d), lambda i, j: (jnp.maximum(i - tp, 0), 0), pipeline_mode=pl.Buffered(1)))
    return (pl.BlockSpec((tm, d), lambda i: (jnp.minimum(i, tp - 1), 0)),
            pl.BlockSpec((tm, d), lambda i: (jnp.maximum(i - tp, 0), 0)))


def _ln_matmul_kernel(xp_ref, xs_ref, g_ref, b_ref, w_ref, o_ref, xn_ref, *, tp):
    i = pl.program_id(0)
    first_col = pl.program_id(1) == 0

    @pl.when(jnp.logical_and(first_col, i < tp))
    def _():
        xn_ref[...] = _layer_norm_f32(xp_ref[...], g_ref[...], b_ref[...]).astype(BF16)

    @pl.when(jnp.logical_and(first_col, i >= tp))
    def _():
        xn_ref[...] = _layer_norm_f32(xs_ref[...], g_ref[...], b_ref[...]).astype(BF16)

    o_ref[...] = jnp.dot(xn_ref[...], w_ref[...], preferred_element_type=F32).astype(o_ref.dtype)


def _ln_matmul(xp, xs, g, b, w, tm, tn):
    d = xp.shape[1]
    n = w.shape[1]
    tp, ts = xp.shape[0] // tm, xs.shape[0] // tm
    xp_spec, xs_spec = _two_group_specs(tm, d, tp, True)
    return pl.pallas_call(
        functools.partial(_ln_matmul_kernel, tp=tp),
        grid=(tp + ts, n // tn),
        in_specs=[
            xp_spec, xs_spec,
            pl.BlockSpec((1, d), lambda i, j: (0, 0)),
            pl.BlockSpec((1, d), lambda i, j: (0, 0)),
            pl.BlockSpec((d, tn), lambda i, j: (0, j)),
        ],
        out_specs=pl.BlockSpec((tm, tn), lambda i, j: (i, j)),
        out_shape=jax.ShapeDtypeStruct(((tp + ts) * tm, n), BF16),
        scratch_shapes=[pltpu.VMEM((tm, d), BF16)],
        compiler_params=_cparams(("parallel", "arbitrary"), 44),
        name="ln_matmul",
    )(xp, xs, g, b, w)


def _diff_attn_kernel(par_ref, q_ref, k_ref, v_ref, km_ref, vm_ref, g_ref, o_ref, kall_ref, vall_ref, bias_ref):
    h = pl.program_id(0)
    b = pl.program_id(1)
    qi = pl.program_id(2)
    tq = q_ref.shape[0]
    sub = bias_ref.shape[0]
    n_keys = kall_ref.shape[0]
    shift = bias_ref.shape[1] - n_keys

    @pl.when(jnp.logical_and(b == 0, qi == 0))
    def _():
        i = lax.broadcasted_iota(jnp.int32, (sub, 1), 0)
        j = lax.broadcasted_iota(jnp.int32, (1, bias_ref.shape[1]), 1)
        bias_ref[...] = -par_ref[1 + h] * jnp.abs(i + (META_PAD + shift) - j).astype(F32)

    @pl.when(qi == 0)
    def _():
        kall_ref[0:META_PAD, :] = km_ref[...]
        kall_ref[META_PAD:, :] = k_ref[...]
        vall_ref[0:META_PAD, :] = vm_ref[...]
        vall_ref[META_PAD:, :] = v_ref[...]

    lam = par_ref[0]
    lane = lax.broadcasted_iota(jnp.int32, (1, 2 * A_HEAD_DIM), 1)
    scale = jnp.asarray(A_HEAD_DIM ** -0.5, BF16)
    zero = jnp.zeros((), BF16)
    lane_m = lax.broadcasted_iota(jnp.int32, (1, META_PAD), 1)
    pad_mask = jnp.where(lane_m < META_PAD - N_META, NEG_BIG, 0.0)
    nt = (((1,), (1,)), ((), ()))

    def one_pass(r, carry):
        rows = pl.ds(pl.multiple_of(r * sub, sub), sub)
        q = q_ref[rows, :]
        q1 = jnp.where(lane < A_HEAD_DIM, q, zero) * scale
        q2 = jnp.where(lane >= A_HEAD_DIM, q, zero) * scale
        k_all = kall_ref[...]
        s1 = lax.dot_general(q1, k_all, nt, preferred_element_type=F32)
        s2 = lax.dot_general(q2, k_all, nt, preferred_element_type=F32)

        off = pl.multiple_of(shift - (qi * tq + r * sub), LANES)
        bias_meta = bias_ref[:, pl.ds(off, META_PAD)] + pad_mask
        bias_seq = bias_ref[:, pl.ds(off + META_PAD, n_keys - META_PAD)]

        def softmax_parts(s):
            s_meta = s[:, 0:META_PAD] + bias_meta
            s_seq = s[:, META_PAD:] + bias_seq
            m = jnp.maximum(jnp.max(s_meta, axis=-1, keepdims=True), jnp.max(s_seq, axis=-1, keepdims=True))
            p_meta = jnp.exp(s_meta - m)
            p_seq = jnp.exp(s_seq - m)
            l = jnp.sum(p_meta, axis=-1, keepdims=True) + jnp.sum(p_seq, axis=-1, keepdims=True)
            return p_meta.astype(BF16), p_seq.astype(BF16), l

        p1_meta, p1_seq, l1 = softmax_parts(s1)
        p2_meta, p2_seq, l2 = softmax_parts(s2)
        a1 = (1.0 / l1).astype(BF16)
        a2 = (lam / l2).astype(BF16)
        w_meta = p1_meta * a1 - p2_meta * a2
        w_seq = p1_seq * a1 - p2_seq * a2
        o = (jnp.dot(w_meta, vall_ref[0:META_PAD, :], preferred_element_type=F32)
             + jnp.dot(w_seq, vall_ref[META_PAD:, :], preferred_element_type=F32))
        ms = jnp.mean(o * o, axis=-1, keepdims=True)
        o_ref[rows, :] = (o * lax.rsqrt(ms + EPS) * g_ref[...]).astype(o_ref.dtype)
        return carry

    lax.fori_loop(0, tq // sub, one_pass, 0, unroll=True)


def _diff_attn(proj, meta_proj, params, subln_g, n_seq, seq, tq):
    n_tok = proj.shape[0]
    nq = seq // tq
    sub = min(ATTN_ROWS, tq)
    hw = 2 * A_HEAD_DIM
    return pl.pallas_call(
        _diff_attn_kernel,
        grid=(A_HEADS, n_seq, nq),
        in_specs=[
            pl.BlockSpec(memory_space=pltpu.SMEM),
            pl.BlockSpec((tq, hw), lambda h, b, i: (b * nq + i, h)),
            pl.BlockSpec((seq, hw), lambda h, b, i: (b, A_HEADS + h)),
            pl.BlockSpec((seq, hw), lambda h, b, i: (b, 2 * A_HEADS + h)),
            pl.BlockSpec((META_PAD, hw), lambda h, b, i: (0, A_HEADS + h)),
            pl.BlockSpec((META_PAD, hw), lambda h, b, i: (0, 2 * A_HEADS + h)),
            pl.BlockSpec((1, hw), lambda h, b, i: (0, 0)),
        ],
        out_specs=pl.BlockSpec((tq, hw), lambda h, b, i: (b * nq + i, h)),
        out_shape=jax.ShapeDtypeStruct((n_tok, A_HEADS * hw), BF16),
        scratch_shapes=[pltpu.VMEM((META_PAD + seq, hw), BF16), pltpu.VMEM((META_PAD + seq, hw), BF16),
                        pltpu.VMEM((sub, META_PAD + 2 * seq - sub), F32)],
        compiler_params=_cparams(("arbitrary", "arbitrary", "arbitrary"), 48),
        name="diff_attn",
    )(params, proj, proj, proj, meta_proj, meta_proj, subln_g)


def _split_cumsum(tri, x):
    hi = x.astype(BF16)
    lo = (x - hi.astype(F32)).astype(BF16)
    return (jnp.dot(tri, hi, preferred_element_type=F32) + jnp.dot(tri, lo, preferred_element_type=F32))


def _hgrn_chunk(q, k, logf, v_bf, state_t, reverse):
    c = q.shape[0]
    r = lax.broadcasted_iota(jnp.int32, (c, c), 0)
    s = lax.broadcasted_iota(jnp.int32, (c, c), 1)
    if reverse:
        visible = s >= r
        ref_row, last_row = c - 1 - c // 2, 0
    else:
        visible = s <= r
        ref_row, last_row = c // 2, c - 1
    tri = jnp.where(visible, 1.0, 0.0).astype(BF16)
    b = _split_cumsum(tri, logf)
    b_ref = b[ref_row:ref_row + 1, :]
    b_last = b[last_row:last_row + 1, :]
    nt = (((1,), (1,)), ((), ()))
    tn = (((0,), (0,)), ((), ()))
    qd = (q * jnp.exp(b - b_ref)).astype(BF16)
    kd = (k * jnp.exp(b_ref - b)).astype(BF16)
    scores = lax.dot_general(qd, kd, nt, preferred_element_type=F32)
    scores = jnp.where(visible, scores, 0.0).astype(BF16)
    o_intra = jnp.dot(scores, v_bf, preferred_element_type=F32)
    qe = (q * jnp.exp(b)).astype(BF16)
    o_inter = lax.dot_general(qe, state_t.astype(BF16), nt, preferred_element_type=F32)
    ks = (k * jnp.exp(b_last - b)).astype(BF16)
    d_state_t = lax.dot_general(v_bf, ks, tn, preferred_element_type=F32)
    new_state_t = state_t * jnp.exp(b_last) + d_state_t
    return o_intra + o_inter, new_state_t


def _forget_gate(logit, lb):
    f = lb + (1.0 - lb) * jax.nn.sigmoid(logit)
    return 1.0 - f, jnp.log(f)


def _chunk_sum_operators(reverse):
    g = HGRN_GROUP
    r = lax.broadcasted_iota(jnp.int32, (g, g), 0)
    s = lax.broadcasted_iota(jnp.int32, (g, g), 1)
    shift = H_CHUNK.bit_length() - 1
    same = (r >> shift) == (s >> shift)
    s_loc = s & (H_CHUNK - 1)
    if reverse:
        cum, ref = s >= r, s_loc >= H_CHUNK - 1 - H_CHUNK // 2
    else:
        cum, ref = s <= r, s_loc <= H_CHUNK // 2
    ops = [jnp.logical_and(same, cum), jnp.logical_and(same, ref), same]
    return jnp.concatenate([jnp.where(o, 1.0, 0.0) for o in ops], axis=0).astype(BF16)


def _hgrn_kernel(q_ref, ff_ref, fb_ref, v_ref, hg_ref, mff_ref, mv_ref, lbf_ref, lbb_ref, g_ref, o_ref,
                 qdf_ref, kdf_ref, qdb_ref, kdb_ref, qe_ref, ks_ref, dec_ref, ds_ref, sp_ref, acc_ref):
    seq = q_ref.shape[0]
    n_heads = q_ref.shape[1] // H_DV
    n_chunks = seq // H_CHUNK
    dk = H_EXPAND
    nt = (((1,), (1,)), ((), ()))
    tn = (((0,), (0,)), ((), ()))
    ops_f = _chunk_sum_operators(False)
    ops_b = _chunk_sum_operators(True)
    r = lax.broadcasted_iota(jnp.int32, (2 * H_CHUNK, 2 * H_CHUNK), 0)
    s = lax.broadcasted_iota(jnp.int32, (2 * H_CHUNK, 2 * H_CHUNK), 1)
    same_chunk = (r >= H_CHUNK) == (s >= H_CHUNK)
    vis_f = jnp.logical_and(same_chunk, s <= r)
    vis_b = jnp.logical_and(same_chunk, s >= r)
    mrow = lax.broadcasted_iota(jnp.int32, (H_CHUNK, 1), 0)
    is_meta = mrow >= H_CHUNK - N_META
    zeros_q = jnp.zeros((H_CHUNK, dk), F32)
    state0 = jnp.zeros((H_DV, dk), F32)
    chunks_per_group = HGRN_GROUP // H_CHUNK

    def chunk_rows(x):
        return jnp.concatenate([x[c * H_CHUNK:c * H_CHUNK + 1] for c in range(chunks_per_group)], axis=0)

    def per_chunk(y):
        return jnp.concatenate([jnp.broadcast_to(y[c:c + 1], (H_CHUNK, y.shape[1]))
                                for c in range(chunks_per_group)], axis=0)

    def dec_row(c):
        return pl.ds(pl.multiple_of(c * DEC_ROWS, DEC_ROWS), DEC_ROWS)

    for j in range(n_heads):
        head = pl.ds(j * H_DV, H_DV)
        lbf = lbf_ref[:, head]
        lbb = lbb_ref[:, head]

        def prepare(g, carry):
            rows = pl.ds(pl.multiple_of(g * HGRN_GROUP, HGRN_GROUP), HGRN_GROUP)
            q = jax.nn.silu(q_ref[rows, head].astype(F32))
            for logit_ref, lb, ops, qd_ref, kd_ref, col in ((ff_ref, lbf, ops_f, qdf_ref, kdf_ref, 0),
                                                           (fb_ref, lbb, ops_b, qdb_ref, kdb_ref, dk)):
                k, logf = _forget_gate(logit_ref[rows, head].astype(F32), lb)
                hi = logf.astype(BF16)
                lo = (logf - hi.astype(F32)).astype(BF16)
                sums = jnp.dot(ops, jnp.concatenate([hi, lo], axis=1), preferred_element_type=F32)
                sums = sums[:, 0:dk] + sums[:, dk:]
                b = sums[0:HGRN_GROUP]
                b_mid = sums[HGRN_GROUP:2 * HGRN_GROUP]
                b_all = sums[2 * HGRN_GROUP:]
                e_mid = per_chunk(jnp.exp(chunk_rows(b_mid)))
                e_all = jnp.exp(chunk_rows(b_all))
                e_rest = per_chunk(jnp.exp(chunk_rows(b_all) - chunk_rows(b_mid)))
                qd = q * jnp.exp(b - b_mid)
                kd = k * jnp.exp(b_mid - b)
                qd_ref[rows, :] = qd.astype(BF16)
                kd_ref[rows, :] = kd.astype(BF16)
                qe_ref[rows, col:col + dk] = (qd * e_mid).astype(BF16)
                ks_ref[rows, col:col + dk] = (kd * e_rest).astype(BF16)
                group_rows = chunks_per_group * DEC_ROWS
                dec_ref[pl.ds(pl.multiple_of(g * group_rows, group_rows), group_rows), col:col + dk] = (
                    jnp.concatenate([jnp.broadcast_to(e_all[c:c + 1], (DEC_ROWS, dk))
                                     for c in range(chunks_per_group)], axis=0))
            return carry

        lax.fori_loop(0, seq // HGRN_GROUP, prepare, 0, unroll=2)

        def intra(p, carry):
            rows = pl.ds(pl.multiple_of(p * (2 * H_CHUNK), 2 * H_CHUNK), 2 * H_CHUNK)
            s_f = lax.dot_general(qdf_ref[rows, :], kdf_ref[rows, :], nt, preferred_element_type=F32)
            s_b = lax.dot_general(qdb_ref[rows, :], kdb_ref[rows, :], nt, preferred_element_type=F32)
            w = (jnp.where(vis_f, s_f, 0.0) + jnp.where(vis_b, s_b, 0.0)).astype(BF16)
            acc_ref[rows, head] = jnp.dot(w, v_ref[rows, head], preferred_element_type=F32)
            for half in range(2):
                c = 2 * p + half
                rows_c = pl.ds(pl.multiple_of(c * H_CHUNK, H_CHUNK), H_CHUNK)
                ds_ref[pl.ds(pl.multiple_of(c * H_DV, H_DV), H_DV), :] = lax.dot_general(
                    v_ref[rows_c, head], ks_ref[rows_c, :], tn, preferred_element_type=F32)
            return carry

        lax.fori_loop(0, n_chunks // 2, intra, 0, unroll=HGRN_UNROLL // 2)

        mk, mlogf = _forget_gate(mff_ref[:, head].astype(F32), lbf)
        mk = jnp.where(is_meta, mk, 0.0)
        mlogf = jnp.where(is_meta, mlogf, 0.0)
        state_f0 = _hgrn_chunk(zeros_q, mk, mlogf, mv_ref[:, head], state0, reverse=False)[1]

        def scan(i, carry):
            st_f, st_b = carry
            cf = i
            cb = n_chunks - 1 - i
            blk_f = pl.ds(pl.multiple_of(cf * H_DV, H_DV), H_DV)
            blk_b = pl.ds(pl.multiple_of(cb * H_DV, H_DV), H_DV)
            sp_ref[blk_f, 0:dk] = st_f.astype(BF16)
            sp_ref[blk_b, dk:] = st_b.astype(BF16)
            st_f = st_f * dec_ref[dec_row(cf), 0:dk][0:1] + ds_ref[blk_f, 0:dk]
            st_b = st_b * dec_ref[dec_row(cb), dk:][0:1] + ds_ref[blk_b, dk:]
            return st_f, st_b

        lax.fori_loop(0, n_chunks, scan, (state_f0, state0))

        def inter(c, carry):
            rows = pl.ds(pl.multiple_of(c * H_CHUNK, H_CHUNK), H_CHUNK)
            state = sp_ref[pl.ds(pl.multiple_of(c * H_DV, H_DV), H_DV), :]
            acc_ref[rows, head] += lax.dot_general(qe_ref[rows, :], state, nt, preferred_element_type=F32)
            return carry

        lax.fori_loop(0, n_chunks, inter, 0, unroll=HGRN_UNROLL)

    for j in range(n_heads):
        head = pl.ds(j * H_DV, H_DV)
        o = acc_ref[:, head]
        ms = jnp.mean(o * o, axis=-1, keepdims=True)
        normed = o * lax.rsqrt(ms + EPS) * g_ref[...]
        o_ref[:, head] = (normed * jax.nn.sigmoid(hg_ref[:, head].astype(F32))).astype(o_ref.dtype)


def _hgrn(proj, meta_proj, lb_fwd, lb_bwd, norm_g, n_seq, seq):
    n_tok = proj.shape[0]
    hps = HGRN_HEADS_PER_STEP
    w = hps * H_DV
    col0 = 3 * A_HEADS * (2 * A_HEAD_DIM) // w
    per_sec = H_HEADS // hps
    meta_blk = META_PAD // H_CHUNK - 1

    def sec(k):
        return pl.BlockSpec((seq, w), lambda b, h, k=k: (b, col0 + k * per_sec + h))

    def meta_sec(k):
        return pl.BlockSpec((H_CHUNK, w), lambda b, h, k=k: (meta_blk, col0 + k * per_sec + h))

    return pl.pallas_call(
        _hgrn_kernel,
        grid=(n_seq, per_sec),
        in_specs=[
            sec(0), sec(1), sec(2), sec(3), sec(4),
            meta_sec(1), meta_sec(3),
            pl.BlockSpec((1, w), lambda b, h: (0, h)),
            pl.BlockSpec((1, w), lambda b, h: (0, h)),
            pl.BlockSpec((1, H_DV), lambda b, h: (0, 0)),
        ],
        out_specs=pl.BlockSpec((seq, w), lambda b, h: (b, h)),
        out_shape=jax.ShapeDtypeStruct((n_tok, H_HEADS * H_DV), BF16),
        scratch_shapes=[
            pltpu.VMEM((seq, H_EXPAND), BF16), pltpu.VMEM((seq, H_EXPAND), BF16),
            pltpu.VMEM((seq, H_EXPAND), BF16), pltpu.VMEM((seq, H_EXPAND), BF16),
            pltpu.VMEM((seq, 2 * H_EXPAND), BF16),
            pltpu.VMEM((seq, 2 * H_EXPAND), BF16),
            pltpu.VMEM((seq // H_CHUNK * DEC_ROWS, 2 * H_EXPAND), F32),
            pltpu.VMEM((seq // H_CHUNK * H_DV, 2 * H_EXPAND), F32),
            pltpu.VMEM((seq // H_CHUNK * H_DV, 2 * H_EXPAND), BF16),
            pltpu.VMEM((seq, w), F32),
        ],
        compiler_params=_cparams(("parallel", "parallel"), 48),
        name="hgrn",
    )(proj, proj, proj, proj, proj, meta_proj, meta_proj, lb_fwd, lb_bwd, norm_g)


def _merge_kernel(ao_ref, ho_ref, ga_ref, gh_ref, xp_ref, xs_ref, wa_ref, wh_ref, wo_ref, lng_ref, lnb_ref,
                  l1g_ref, l1b_ref, hf_ref, hp_ref, *, tp):
    pa = jnp.dot(ao_ref[...], wa_ref[...], preferred_element_type=F32)
    ph = jnp.dot(ho_ref[...], wh_ref[...], preferred_element_type=F32)
    merged = jax.nn.sigmoid(ga_ref[...].astype(F32)) * pa + jax.nn.sigmoid(gh_ref[...].astype(F32)) * ph
    t = jnp.dot(merged.astype(BF16), wo_ref[...], preferred_element_type=F32)

    def finish(x_ref):
        x_in = _layer_norm_f32(x_ref[...], lng_ref[...], lnb_ref[...])
        h = _layer_norm_f32(DN_ALPHA * x_in + t, l1g_ref[...], l1b_ref[...])
        hf_ref[...] = h
        hp_ref[...] = _pack_bf16_pair(h)

    @pl.when(pl.program_id(0) < tp)
    def _():
        finish(xp_ref)

    @pl.when(pl.program_id(0) >= tp)
    def _():
        finish(xs_ref)


def _merge(ao, ho, proj, xp, xs, wa, wh, wo, ln_g, ln_b, l1g, l1b, tm):
    d = xp.shape[1]
    tp, ts = xp.shape[0] // tm, xs.shape[0] // tm
    n_tok = (tp + ts) * tm
    gate_blk0 = (proj.shape[1] - 2 * d) // d
    vec = pl.BlockSpec((1, d), lambda i: (0, 0))
    xp_spec, xs_spec = _two_group_specs(tm, d, tp, False)
    return pl.pallas_call(
        functools.partial(_merge_kernel, tp=tp),
        grid=(tp + ts,),
        in_specs=[
            pl.BlockSpec((tm, ao.shape[1]), lambda i: (i, 0)),
            pl.BlockSpec((tm, ho.shape[1]), lambda i: (i, 0)),
            pl.BlockSpec((tm, d), lambda i: (i, gate_blk0)),
            pl.BlockSpec((tm, d), lambda i: (i, gate_blk0 + 1)),
            xp_spec, xs_spec,
            _resident(wa.shape), _resident(wh.shape), _resident(wo.shape),
            vec, vec, vec, vec,
        ],
        out_specs=[pl.BlockSpec((tm, d), lambda i: (i, 0)), pl.BlockSpec((tm, d // 2), lambda i: (i, 0))],
        out_shape=[jax.ShapeDtypeStruct((n_tok, d), F32), jax.ShapeDtypeStruct((n_tok, d // 2), U32)],
        compiler_params=_cparams(("parallel",), 56),
        name="merge",
    )(ao, ho, proj, proj, xp, xs, wa, wh, wo, ln_g, ln_b, l1g, l1b)


def _first_argmax(x, iota, size):
    m = jnp.max(x, axis=0, keepdims=True)
    idx = jnp.min(jnp.where(x == m, iota, size), axis=0, keepdims=True)
    return m, idx


def _router_kernel(h_ref, wrt_ref, bias_ref, idx_ref, gate_ref, rank_ref, cnt_ref, base_ref):
    tm = h_ref.shape[0]
    half = h_ref.shape[1]

    @pl.when(pl.program_id(0) == 0)
    def _():
        base_ref[...] = jnp.zeros_like(base_ref)

    nt = (((1,), (1,)), ((), ()))
    h_lo, h_hi = _unpack_bf16_pair(h_ref[...])
    logits = (lax.dot_general(wrt_ref[:, 0:half], h_lo, nt, preferred_element_type=F32)
              + lax.dot_general(wrt_ref[:, half:], h_hi, nt, preferred_element_type=F32))
    scores = jax.nn.sigmoid(logits)
    biased = scores + bias_ref[...]
    neg_inf = jnp.asarray(-jnp.inf, F32)

    iota_g = lax.broadcasted_iota(jnp.int32, (GROUP_SIZE, tm), 0)
    iota_n = lax.broadcasted_iota(jnp.int32, (N_GROUPS, tm), 0)
    grp = jnp.full((N_GROUPS, tm), neg_inf, F32)
    for g in range(N_GROUPS):
        xg = biased[g * GROUP_SIZE:(g + 1) * GROUP_SIZE, :]
        m1, i1 = _first_argmax(xg, iota_g, GROUP_SIZE)
        m2 = jnp.max(jnp.where(iota_g == i1, neg_inf, xg), axis=0, keepdims=True)
        grp = jnp.where(iota_n == g, m1 + m2, grp)

    keep_f = jnp.zeros((N_GROUPS, tm), F32)
    for _ in range(TOPK_GROUPS):
        _, ig = _first_argmax(grp, iota_n, N_GROUPS)
        sel = iota_n == ig
        keep_f = jnp.where(sel, 1.0, keep_f)
        grp = jnp.where(sel, neg_inf, grp)

    masked = jnp.concatenate(
        [jnp.where(keep_f[g:g + 1, :] > 0.5, biased[g * GROUP_SIZE:(g + 1) * GROUP_SIZE, :], neg_inf)
         for g in range(N_GROUPS)], axis=0)

    iota_e = lax.broadcasted_iota(jnp.int32, (N_EXPERTS, tm), 0)
    chosen = jnp.zeros((N_EXPERTS, tm), F32)
    idxs, gates = [], []
    for _ in range(TOP_K):
        _, ie = _first_argmax(masked, iota_e, N_EXPERTS)
        sel = iota_e == ie
        gates.append(jnp.sum(jnp.where(sel, scores, 0.0), axis=0, keepdims=True))
        idxs.append(ie)
        chosen = jnp.where(sel, 1.0, chosen)
        masked = jnp.where(sel, neg_inf, masked)

    denom = gates[0]
    for gk in gates[1:]:
        denom = denom + gk

    r = lax.broadcasted_iota(jnp.int32, (tm, tm), 0)
    c = lax.broadcasted_iota(jnp.int32, (tm, tm), 1)
    before = jnp.where(r < c, 1.0, 0.0).astype(BF16)
    pos = jnp.dot(chosen.astype(BF16), before, preferred_element_type=F32) + base_ref[...]
    for k in range(TOP_K):
        sel = iota_e == idxs[k]
        rank_ref[k:k + 1, :] = jnp.sum(jnp.where(sel, pos, 0.0), axis=0, keepdims=True).astype(jnp.int32)
        idx_ref[k:k + 1, :] = idxs[k]
        gate_ref[k:k + 1, :] = gates[k] / denom * ROUTED_SCALE

    base_ref[...] = base_ref[...] + jnp.sum(chosen, axis=1, keepdims=True)
    cnt_ref[...] = base_ref[...]


def _router(h_pk, w_router_t, bias_col, tm):
    n_tok, half = h_pk.shape
    row_blk = pl.BlockSpec((TOP_K, tm), lambda i: (0, i))
    return pl.pallas_call(
        _router_kernel,
        grid=(n_tok // tm,),
        in_specs=[
            pl.BlockSpec((tm, half), lambda i: (i, 0)),
            _resident(w_router_t.shape),
            pl.BlockSpec((N_EXPERTS, 1), lambda i: (0, 0)),
        ],
        out_specs=[row_blk, row_blk, row_blk, pl.BlockSpec((N_EXPERTS, 1), lambda i: (0, 0))],
        out_shape=[
            jax.ShapeDtypeStruct((TOP_K, n_tok), jnp.int32),
            jax.ShapeDtypeStruct((TOP_K, n_tok), F32),
            jax.ShapeDtypeStruct((TOP_K, n_tok), jnp.int32),
            jax.ShapeDtypeStruct((N_EXPERTS, 1), F32),
        ],
        scratch_shapes=[pltpu.VMEM((N_EXPERTS, 1), F32)],
        compiler_params=_cparams(("arbitrary",), 32),
        name="router",
    )(h_pk, w_router_t, bias_col)


def _dest_kernel(idx_ref, rank_ref, start_ref, dest_ref):
    tm = idx_ref.shape[1]
    iota_e = lax.broadcasted_iota(jnp.int32, (N_EXPERTS, tm), 0)
    start = start_ref[...]
    for k in range(TOP_K):
        sel = iota_e == idx_ref[k:k + 1, :]
        base = jnp.sum(jnp.where(sel, start, 0.0), axis=0, keepdims=True)
        dest_ref[k:k + 1, :] = base.astype(jnp.int32) + rank_ref[k:k + 1, :]


def _dest(idx, rank, start_col, tm):
    n_tok = idx.shape[1]
    row_blk = pl.BlockSpec((TOP_K, tm), lambda i: (0, i))
    return pl.pallas_call(
        _dest_kernel,
        grid=(n_tok // tm,),
        in_specs=[row_blk, row_blk, pl.BlockSpec((N_EXPERTS, 1), lambda i: (0, 0))],
        out_specs=row_blk,
        out_shape=jax.ShapeDtypeStruct((TOP_K, n_tok), jnp.int32),
        compiler_params=_cparams(("parallel",), 32),
        name="dest",
    )(idx, rank, start_col)


def _dispatch_kernel(pstart_ref, cnt_ref, tail_ref, dest_ref, h_ref, xs_ref, zero_ref, sem, pad_sem, tail_sem):
    tm = h_ref.shape[0]

    def pad_copy(e, j):
        return pltpu.make_async_copy(zero_ref.at[pl.ds(0, 1)],
                                     xs_ref.at[pl.ds(pstart_ref[e] + cnt_ref[e] + j, 1)], pad_sem)

    def tail_copy():
        return pltpu.make_async_copy(
            zero_ref, xs_ref.at[pl.ds(pl.multiple_of(tail_ref[0], ROW_ALIGN), EXPERT_ROWS)], tail_sem)

    def for_each_pad_row(action):
        def body(e, carry):
            n_pad = (ROW_ALIGN - cnt_ref[e] % ROW_ALIGN) % ROW_ALIGN
            for j in range(ROW_ALIGN - 1):
                @pl.when(j < n_pad)
                def _():
                    action(pad_copy(e, j))
            return carry
        lax.fori_loop(0, N_EXPERTS, body, 0)

    @pl.when(pl.program_id(0) == 0)
    def _():
        zero_ref[...] = jnp.zeros_like(zero_ref)
        tail_copy().start()
        for_each_pad_row(lambda c: c.start())
        for_each_pad_row(lambda c: c.wait())
        tail_copy().wait()

    def row_copy(r, k):
        return pltpu.make_async_copy(h_ref.at[pl.ds(r, 1)], xs_ref.at[pl.ds(dest_ref[0, k * tm + r], 1)], sem)

    def start(r, carry):
        for k in range(TOP_K):
            row_copy(r, k).start(priority=k % DMA_THREADS)
        return carry

    def wait(r, carry):
        for k in range(TOP_K):
            row_copy(r, k).wait()
        return carry

    lax.fori_loop(0, tm, start, 0)
    lax.fori_loop(0, tm, wait, 0)


def _dispatch(h_pk, dest_tiles, pstart, counts, tail, n_rows, tm):
    n_tok, half = h_pk.shape
    grid_spec = pltpu.PrefetchScalarGridSpec(
        num_scalar_prefetch=3,
        grid=(n_tok // tm,),
        in_specs=[
            pl.BlockSpec((None, 1, TOP_K * tm), lambda i, *_: (i, 0, 0), memory_space=pltpu.SMEM),
            pl.BlockSpec((tm, half), lambda i, *_: (i, 0)),
        ],
        out_specs=pl.BlockSpec(memory_space=pl.ANY),
        scratch_shapes=[pltpu.VMEM((EXPERT_ROWS, half), U32), pltpu.SemaphoreType.DMA,
                        pltpu.SemaphoreType.DMA, pltpu.SemaphoreType.DMA],
    )
    return pl.pallas_call(
        _dispatch_kernel,
        grid_spec=grid_spec,
        out_shape=jax.ShapeDtypeStruct((n_rows, half), U32),
        compiler_params=_cparams(("arbitrary",), 32, disable_bounds_checks=True),
        name="dispatch",
    )(pstart, counts, tail, dest_tiles, h_pk)


def _experts_kernel(row_ref, e_ref, valid_ref, newe_ref, slot_ref, nexte_ref,
                    x_hbm, wg_hbm, wu_hbm, wd_hbm, y_hbm,
                    x_buf, y_buf, wg_buf, wu_buf, wd_buf, wgu_bf, wd_bf, x_sems, y_sems, sems):
    n_steps = pl.num_programs(0) * WINDOWS_PER_STEP
    ff = wg_buf.shape[2]

    def window(step):
        return pl.ds(pl.multiple_of(row_ref[step], ROW_ALIGN), EXPERT_ROWS)

    def x_copy(step, slot):
        return pltpu.make_async_copy(x_hbm.at[window(step)], x_buf.at[slot], x_sems.at[slot])

    def y_copy(step, slot):
        return pltpu.make_async_copy(y_buf.at[slot], y_hbm.at[window(step)], y_sems.at[slot])

    def weight_copies(e, slot):
        copies = []
        for m, (src, dst) in enumerate(((wg_hbm, wg_buf), (wu_hbm, wu_buf), (wd_hbm, wd_buf))):
            rows = dst.shape[1] // WEIGHT_DMA_CHUNKS
            for c in range(WEIGHT_DMA_CHUNKS):
                part = pl.ds(c * rows, rows)
                copies.append(pltpu.make_async_copy(src.at[0, e, part], dst.at[slot, part],
                                                    sems.at[slot, m * WEIGHT_DMA_CHUNKS + c]))
        return copies

    def process(i, cur):
        @pl.when(i == 0)
        def _():
            x_copy(0, 0).start()

        @pl.when(newe_ref[i] == 1)
        def _():
            slot = slot_ref[i]
            e = e_ref[i]

            @pl.when(i == 0)
            def _():
                for c in weight_copies(e, slot):
                    c.start(priority=WEIGHT_DMA_THREAD)

            for c in weight_copies(e, slot):
                c.wait()
            nxt = nexte_ref[i]

            @pl.when(nxt >= 0)
            def _():
                for c in weight_copies(nxt, 1 - slot):
                    c.start(priority=WEIGHT_DMA_THREAD)

            wgu_bf[:, 0:ff] = wg_buf[slot].astype(BF16)
            wgu_bf[:, ff:] = wu_buf[slot].astype(BF16)
            wd_bf[...] = wd_buf[slot].astype(BF16)

        @pl.when(valid_ref[i] == 1)
        def _():
            nxt_step = jnp.minimum(i + 1, n_steps - 1)
            has_next = jnp.logical_and(i + 1 < n_steps, valid_ref[nxt_step] == 1)
            x_copy(i, cur).wait()

            @pl.when(has_next)
            def _():
                x_copy(nxt_step, 1 - cur).start()

            gu = _dot_packed(x_buf[cur], wgu_bf)
            hid = (jax.nn.silu(gu[:, 0:ff]) * gu[:, ff:]).astype(BF16)
            y_buf[cur] = _pack_bf16_pair(jnp.dot(hid, wd_bf[...], preferred_element_type=F32))

            @pl.when(i > 0)
            def _():
                y_copy(i - 1, 1 - cur).wait()

            y_copy(i, cur).start()

            @pl.when(jnp.logical_not(has_next))
            def _():
                y_copy(i, cur).wait()

    for u in range(WINDOWS_PER_STEP):
        process(pl.program_id(0) * WINDOWS_PER_STEP + u, u % 2)


def _experts(x_sorted, steps, w_gate, w_up, w_down):
    n_rows, half = x_sorted.shape
    d = 2 * half
    ff = w_gate.shape[-1]
    hbm = pl.BlockSpec(memory_space=pl.ANY)
    grid_spec = pltpu.PrefetchScalarGridSpec(
        num_scalar_prefetch=len(steps),
        grid=(steps[0].shape[0] // WINDOWS_PER_STEP,),
        in_specs=[hbm, hbm, hbm, hbm],
        out_specs=hbm,
        scratch_shapes=[pltpu.VMEM((2, EXPERT_ROWS, half), U32), pltpu.VMEM((2, EXPERT_ROWS, half), U32),
                        pltpu.VMEM((2, d, ff), F32), pltpu.VMEM((2, d, ff), F32), pltpu.VMEM((2, ff, d), F32),
                        pltpu.VMEM((d, 2 * ff), BF16), pltpu.VMEM((ff, d), BF16),
                        pltpu.SemaphoreType.DMA((2,)), pltpu.SemaphoreType.DMA((2,)),
                        pltpu.SemaphoreType.DMA((2, 3 * WEIGHT_DMA_CHUNKS))],
    )
    return pl.pallas_call(
        _experts_kernel,
        grid_spec=grid_spec,
        out_shape=jax.ShapeDtypeStruct((n_rows, half), U32),
        compiler_params=_cparams(("arbitrary",), 52, disable_bounds_checks=True),
        name="experts",
    )(*steps, x_sorted, w_gate, w_up, w_down)


def _final_kernel(dest_ref, gate_ref, hf_ref, hp_ref, y_ref, wgu_ref, wd_ref, g_ref, b_ref, op_ref, os_ref,
                  ybuf, sem, *, tp):
    tm = hf_ref.shape[0]
    ff = wd_ref.shape[0]

    def row_copy(r, k):
        return pltpu.make_async_copy(y_ref.at[pl.ds(dest_ref[0, k * tm + r], 1)], ybuf.at[k, pl.ds(r, 1)], sem)

    def start(r, carry):
        for k in range(TOP_K):
            row_copy(r, k).start(priority=k % DMA_THREADS)
        return carry

    def wait(r, carry):
        for k in range(TOP_K):
            row_copy(r, k).wait()
        return carry

    lax.fori_loop(0, tm, start, 0)

    gu = _dot_packed(hp_ref[...], wgu_ref)
    hid = (jax.nn.silu(gu[:, 0:ff]) * gu[:, ff:]).astype(BF16)
    acc = DN_ALPHA * hf_ref[...] + jnp.dot(hid, wd_ref[...], preferred_element_type=F32)

    lax.fori_loop(0, tm, wait, 0)
    gate = gate_ref[...]
    half = ybuf.shape[2]
    acc_lo, acc_hi = acc[:, :half], acc[:, half:]
    for k in range(TOP_K):
        packed = ybuf[k]
        g_k = gate[:, k:k + 1]
        acc_lo = acc_lo + lax.bitcast_convert_type(packed << 16, F32) * g_k
        acc_hi = acc_hi + lax.bitcast_convert_type(packed & jnp.uint32(HIGH_HALF), F32) * g_k
    out = _layer_norm_f32(jnp.concatenate([acc_lo, acc_hi], axis=1), g_ref[...], b_ref[...])

    @pl.when(pl.program_id(0) < tp)
    def _():
        op_ref[...] = out

    @pl.when(pl.program_id(0) >= tp)
    def _():
        os_ref[...] = out


def _final(h_f32, h_pk, y_sorted, dest_tiles, gate_t, w_gu, w_d, g, b, n_prompt, tm):
    n_tok, d = h_f32.shape
    tp = n_prompt // tm
    ts = n_tok // tm - tp
    vec = pl.BlockSpec((1, d), lambda i: (0, 0))
    op_spec, os_spec = _two_group_specs(tm, d, tp, False)
    return pl.pallas_call(
        functools.partial(_final_kernel, tp=tp),
        grid=(tp + ts,),
        in_specs=[
            pl.BlockSpec((None, 1, TOP_K * tm), lambda i: (i, 0, 0), memory_space=pltpu.SMEM),
            pl.BlockSpec((tm, TOP_K), lambda i: (i, 0)),
            pl.BlockSpec((tm, d), lambda i: (i, 0)),
            pl.BlockSpec((tm, d // 2), lambda i: (i, 0)),
            pl.BlockSpec(memory_space=pl.ANY),
            _resident(w_gu.shape), _resident(w_d.shape), vec, vec,
        ],
        out_specs=[op_spec, os_spec],
        out_shape=[jax.ShapeDtypeStruct((tp * tm, d), F32), jax.ShapeDtypeStruct((ts * tm, d), F32)],
        scratch_shapes=[pltpu.VMEM((TOP_K, tm, d // 2), U32), pltpu.SemaphoreType.DMA],
        compiler_params=_cparams(("arbitrary",), 48, disable_bounds_checks=True),
        name="final",
    )(dest_tiles, gate_t, h_f32, h_pk, y_sorted, w_gu, w_d, g, b)


def _lower_bounds(gamma):
    return jnp.cumsum(jax.nn.softmax(gamma.astype(F32), axis=0), axis=0)


def _tile_major(dest, tm):
    k, n = dest.shape
    return jnp.transpose(dest.reshape(k, n // tm, tm), (1, 0, 2)).reshape(n // tm, 1, k * tm)


def _expert_steps(counts, n_assign):
    rows = EXPERT_ROWS
    n_steps = (n_assign + N_EXPERTS * (ROW_ALIGN - 1)) // rows + N_EXPERTS
    n_steps = (n_steps + WINDOWS_PER_STEP - 1) // WINDOWS_PER_STEP * WINDOWS_PER_STEP
    padded = (counts + ROW_ALIGN - 1) // ROW_ALIGN * ROW_ALIGN
    pend = jnp.cumsum(padded)
    pstart = pend - padded
    per_e = (padded + rows - 1) // rows
    st_end = jnp.cumsum(per_e)
    st_start = st_end - per_e
    total = st_end[-1]
    i = jnp.arange(n_steps, dtype=jnp.int32)
    ic = jnp.minimum(i, total - 1)
    e = jnp.sum((st_end[None, :] <= ic[:, None]).astype(jnp.int32), axis=1)
    j = ic - st_start[e]
    row0 = pstart[e] + jnp.minimum(j * rows, jnp.maximum(padded[e] - rows, 0))
    valid = i < total
    prev_e = jnp.concatenate([jnp.full((1,), -1, jnp.int32), e[:-1].astype(jnp.int32)])
    new_e = jnp.where(jnp.logical_and(valid, e != prev_e), 1, 0)
    nonempty = counts > 0
    slot = (jnp.cumsum(nonempty.astype(jnp.int32)) - 1) % 2
    ids = jnp.arange(N_EXPERTS, dtype=jnp.int32)
    later = jnp.concatenate([jnp.where(nonempty, ids, N_EXPERTS)[1:], jnp.full((1,), N_EXPERTS, jnp.int32)])
    nxt = jnp.flip(lax.cummin(jnp.flip(later)))
    nxt = jnp.where(nxt < N_EXPERTS, nxt, -1)
    as_i32 = lambda v: v.astype(jnp.int32)
    steps = (as_i32(row0), as_i32(e), as_i32(valid), as_i32(new_e), as_i32(slot[e]), as_i32(nxt[e]))
    n_rows = n_assign + N_EXPERTS * (ROW_ALIGN - 1) + rows
    return as_i32(pstart), as_i32(pend[-1:]), n_rows, steps


def kernel(x_prompt, x_sample, meta_tokens, ln_in_g, ln_in_b, hg_gamma_fwd, hg_gamma_bwd, w_in, lambda_q1, lambda_k1, lambda_q2, lambda_k2, attn_subln_g, hg_norm_g, w_branch_attn, w_branch_hgrn, w_out, ln1_g, ln1_b, w_router, router_bias, w_sh_gate, w_sh_up, w_sh_down, w_ex_gate, w_ex_up, w_ex_down, ln2_g, ln2_b):
    bp, seq, d = x_prompt.shape
    bs, seq_s, _ = x_sample.shape
    assert seq == seq_s and w_in.shape[0] == DEPTH
    n_seq = bp + bs
    n_tok = n_seq * seq
    layer = 0

    xp = x_prompt.reshape(bp * seq, d)
    xs = x_sample.reshape(bs * seq, d)
    row = lambda v: v.reshape(1, -1).astype(F32)
    ln_g, ln_b = row(ln_in_g), row(ln_in_b)
    w_in_bf = w_in[layer].astype(BF16)

    proj = _ln_matmul(xp, xs, ln_g, ln_b, w_in_bf, tm=min(1024, bs * seq), tn=1024)
    meta = meta_tokens.astype(F32)
    meta_proj = _ln_matmul(meta, meta, ln_g, ln_b, w_in_bf, tm=N_META, tn=1024)[:N_META]
    meta_proj = jnp.pad(meta_proj, ((META_PAD - N_META, 0), (0, 0)))

    lam = (jnp.exp(jnp.sum(lambda_q1[layer].astype(F32) * lambda_k1[layer].astype(F32)))
           - jnp.exp(jnp.sum(lambda_q2[layer].astype(F32) * lambda_k2[layer].astype(F32))) + LAMBDA_INIT)
    slopes = 2.0 ** (-8.0 * jnp.arange(1, A_HEADS + 1, dtype=F32) / A_HEADS)
    attn_params = jnp.concatenate([lam.reshape(1), slopes]).astype(F32)
    subln_g = row(attn_subln_g[layer]) * (1.0 - LAMBDA_INIT)
    ao = _diff_attn(proj, meta_proj, attn_params, subln_g, n_seq, seq, tq=min(512, seq))

    lb_fwd = row(_lower_bounds(hg_gamma_fwd)[layer])
    lb_bwd = row(_lower_bounds(hg_gamma_bwd)[layer])
    ho = _hgrn(proj, meta_proj, lb_fwd, lb_bwd, row(hg_norm_g[layer]), n_seq, seq)

    h_f32, h_pk = _merge(ao, ho, proj, xp, xs, w_branch_attn[layer].astype(BF16),
                         w_branch_hgrn[layer].astype(BF16), w_out[layer].astype(BF16), ln_g, ln_b,
                         row(ln1_g[layer]), row(ln1_b[layer]), tm=min(256, bs * seq))

    idx, gate, rank, counts = _router(h_pk, w_router[layer].T.astype(BF16),
                                      router_bias[layer].astype(F32).reshape(N_EXPERTS, 1), tm=min(512, n_tok))
    counts = counts[:, 0].astype(jnp.int32)
    pstart, tail, n_rows, steps = _expert_steps(counts, n_tok * TOP_K)
    dest = _dest(idx, rank, pstart.astype(F32).reshape(N_EXPERTS, 1), tm=min(2048, n_tok))

    tm_d = min(256, n_tok)
    x_sorted = _dispatch(h_pk, _tile_major(dest, tm_d), pstart, counts, tail, n_rows, tm_d)
    y_sorted = _experts(x_sorted, steps, w_ex_gate, w_ex_up, w_ex_down)

    w_sh_gu = jnp.concatenate([w_sh_gate[layer], w_sh_up[layer]], axis=1).astype(BF16)
    tm_f = min(256, bs * seq)
    y_prompt, y_sample = _final(h_f32, h_pk, y_sorted, _tile_major(dest, tm_f), gate.T, w_sh_gu,
                                w_sh_down[layer].astype(BF16), row(ln2_g[layer]), row(ln2_b[layer]),
                                bp * seq, tm_f)
    return (y_prompt.reshape(bp, seq, d), y_sample.reshape(bs, seq, d))
```

```python
import functools
import math

import jax
import jax.numpy as jnp
from jax import lax
from jax.experimental import pallas as pl
from jax.experimental.pallas import tpu as pltpu

F32 = jnp.float32
BF16 = jnp.bfloat16
U32 = jnp.uint32

N_META = 16
A_HEADS = 8
A_HEAD_DIM = 64
H_HEADS = 8
H_EXPAND = 128
H_DV = 128
H_CHUNK = 64
N_EXPERTS = 256
TOP_K = 8
N_GROUPS = 8
TOPK_GROUPS = 4
GROUP_SIZE = N_EXPERTS // N_GROUPS
ROUTED_SCALE = 2.5
DEPTH = 1
DN_ALPHA = (2 * DEPTH) ** 0.25
EPS = 1e-5
LAMBDA_INIT = 0.8 - 0.6 * math.exp(-0.3 * 0)

LANES = 128
HGRN_HEADS_PER_STEP = 4
HGRN_GROUP = 256
DEC_ROWS = 8
HGRN_UNROLL = 16
META_PAD = 128
NEG_BIG = -1e30
EXPERT_ROWS = 256
WEIGHT_DMA_CHUNKS = 4
DMA_THREADS = 2
WEIGHT_DMA_THREAD = 1
ROW_ALIGN = 8
WINDOWS_PER_STEP = 2
ATTN_ROWS = 256
HIGH_HALF = 0xFFFF0000


def _cparams(semantics, vmem_mb, **kw):
    return pltpu.CompilerParams(dimension_semantics=semantics, vmem_limit_bytes=vmem_mb * 1024 * 1024, **kw)


def _layer_norm_f32(x, g, b):
    mu = jnp.mean(x, axis=-1, keepdims=True)
    xc = x - mu
    var = jnp.mean(xc * xc, axis=-1, keepdims=True)
    return xc * lax.rsqrt(var + EPS) * g + b


def _resident(shape):
    nd = len(shape)
    return pl.BlockSpec(shape, lambda *_: (0,) * nd, pipeline_mode=pl.Buffered(1))


def _pack_bf16_pair(x):
    k = x.shape[1] // 2
    lo = lax.bitcast_convert_type(x[:, :k].astype(BF16).astype(F32), U32) >> 16
    hi = lax.bitcast_convert_type(x[:, k:].astype(BF16).astype(F32), U32) & jnp.uint32(HIGH_HALF)
    return lo | hi


def _unpack_bf16_pair(p):
    lo = lax.bitcast_convert_type(p << 16, F32).astype(BF16)
    hi = lax.bitcast_convert_type(p & jnp.uint32(HIGH_HALF), F32).astype(BF16)
    return lo, hi


def _dot_packed(p, w_ref):
    k = p.shape[1]
    lo, hi = _unpack_bf16_pair(p)
    return (jnp.dot(lo, w_ref[0:k, :], preferred_element_type=F32)
            + jnp.dot(hi, w_ref[k:, :], preferred_element_type=F32))


def _two_group_specs(tm, d, tp, extra_grid_axes):
    if extra_grid_axes:
        return (pl.BlockSpec((tm, d), lambda i, j: (jnp.minimum(i, tp - 1), 0), pipeline_mode=pl.Buffered(1)),
                pl.BlockSpec((tm, d), lambda i, j: (jnp.maximum(i - tp, 0), 0), pipeline_mode=pl.Buffered(1)))
    return (pl.BlockSpec((tm, d), lambda i: (jnp.minimum(i, tp - 1), 0)),
            pl.BlockSpec((tm, d), lambda i: (jnp.maximum(i - tp, 0), 0)))


def _ln_matmul_kernel(xp_ref, xs_ref, g_ref, b_ref, w_ref, o_ref, xn_ref, *, tp):
    i = pl.program_id(0)
    first_col = pl.program_id(1) == 0

    @pl.when(jnp.logical_and(first_col, i < tp))
    def _():
        xn_ref[...] = _layer_norm_f32(xp_ref[...], g_ref[...], b_ref[...]).astype(BF16)

    @pl.when(jnp.logical_and(first_col, i >= tp))
    def _():
        xn_ref[...] = _layer_norm_f32(xs_ref[...], g_ref[...], b_ref[...]).astype(BF16)

    o_ref[...] = jnp.dot(xn_ref[...], w_ref[...], preferred_element_type=F32).astype(o_ref.dtype)


def _ln_matmul(xp, xs, g, b, w, tm, tn):
    d = xp.shape[1]
    n = w.shape[1]
    tp, ts = xp.shape[0] // tm, xs.shape[0] // tm
    xp_spec, xs_spec = _two_group_specs(tm, d, tp, True)
    return pl.pallas_call(
        functools.partial(_ln_matmul_kernel, tp=tp),
        grid=(tp + ts, n // tn),
        in_specs=[
            xp_spec, xs_spec,
            pl.BlockSpec((1, d), lambda i, j: (0, 0)),
            pl.BlockSpec((1, d), lambda i, j: (0, 0)),
            pl.BlockSpec((d, tn), lambda i, j: (0, j)),
        ],
        out_specs=pl.BlockSpec((tm, tn), lambda i, j: (i, j)),
        out_shape=jax.ShapeDtypeStruct(((tp + ts) * tm, n), BF16),
        scratch_shapes=[pltpu.VMEM((tm, d), BF16)],
        compiler_params=_cparams(("parallel", "arbitrary"), 44),
        name="ln_matmul",
    )(xp, xs, g, b, w)


def _diff_attn_kernel(par_ref, q_ref, k_ref, v_ref, km_ref, vm_ref, g_ref, o_ref, kall_ref, vall_ref, bias_ref):
    h = pl.program_id(0)
    b = pl.program_id(1)
    qi = pl.program_id(2)
    tq = q_ref.shape[0]
    sub = bias_ref.shape[0]
    n_keys = kall_ref.shape[0]
    shift = bias_ref.shape[1] - n_keys

    @pl.when(jnp.logical_and(b == 0, qi == 0))
    def _():
        i = lax.broadcasted_iota(jnp.int32, (sub, 1), 0)
        j = lax.broadcasted_iota(jnp.int32, (1, bias_ref.shape[1]), 1)
        bias_ref[...] = -par_ref[1 + h] * jnp.abs(i + (META_PAD + shift) - j).astype(F32)

    @pl.when(qi == 0)
    def _():
        kall_ref[0:META_PAD, :] = km_ref[...]
        kall_ref[META_PAD:, :] = k_ref[...]
        vall_ref[0:META_PAD, :] = vm_ref[...]
        vall_ref[META_PAD:, :] = v_ref[...]

    lam = par_ref[0]
    lane = lax.broadcasted_iota(jnp.int32, (1, 2 * A_HEAD_DIM), 1)
    scale = jnp.asarray(A_HEAD_DIM ** -0.5, BF16)
    zero = jnp.zeros((), BF16)
    lane_m = lax.broadcasted_iota(jnp.int32, (1, META_PAD), 1)
    pad_mask = jnp.where(lane_m < META_PAD - N_META, NEG_BIG, 0.0)
    nt = (((1,), (1,)), ((), ()))

    def one_pass(r, carry):
        rows = pl.ds(pl.multiple_of(r * sub, sub), sub)
        q = q_ref[rows, :]
        q1 = jnp.where(lane < A_HEAD_DIM, q, zero) * scale
        q2 = jnp.where(lane >= A_HEAD_DIM, q, zero) * scale
        k_all = kall_ref[...]
        s1 = lax.dot_general(q1, k_all, nt, preferred_element_type=F32)
        s2 = lax.dot_general(q2, k_all, nt, preferred_element_type=F32)

        off = pl.multiple_of(shift - (qi * tq + r * sub), LANES)
        bias_meta = bias_ref[:, pl.ds(off, META_PAD)] + pad_mask
        bias_seq = bias_ref[:, pl.ds(off + META_PAD, n_keys - META_PAD)]

        def softmax_parts(s):
            s_meta = s[:, 0:META_PAD] + bias_meta
            s_seq = s[:, META_PAD:] + bias_seq
            m = jnp.maximum(jnp.max(s_meta, axis=-1, keepdims=True), jnp.max(s_seq, axis=-1, keepdims=True))
            p_meta = jnp.exp(s_meta - m)
            p_seq = jnp.exp(s_seq - m)
            l = jnp.sum(p_meta, axis=-1, keepdims=True) + jnp.sum(p_seq, axis=-1, keepdims=True)
            return p_meta.astype(BF16), p_seq.astype(BF16), l

        p1_meta, p1_seq, l1 = softmax_parts(s1)
        p2_meta, p2_seq, l2 = softmax_parts(s2)
        a1 = (1.0 / l1).astype(BF16)
        a2 = (lam / l2).astype(BF16)
        w_meta = p1_meta * a1 - p2_meta * a2
        w_seq = p1_seq * a1 - p2_seq * a2
        o = (jnp.dot(w_meta, vall_ref[0:META_PAD, :], preferred_element_type=F32)
             + jnp.dot(w_seq, vall_ref[META_PAD:, :], preferred_element_type=F32))
        ms = jnp.mean(o * o, axis=-1, keepdims=True)
        o_ref[rows, :] = (o * lax.rsqrt(ms + EPS) * g_ref[...]).astype(o_ref.dtype)
        return carry

    lax.fori_loop(0, tq // sub, one_pass, 0, unroll=True)


def _diff_attn(proj, meta_proj, params, subln_g, n_seq, seq, tq):
    n_tok = proj.shape[0]
    nq = seq // tq
    sub = min(ATTN_ROWS, tq)
    hw = 2 * A_HEAD_DIM
    return pl.pallas_call(
        _diff_attn_kernel,
        grid=(A_HEADS, n_seq, nq),
        in_specs=[
            pl.BlockSpec(memory_space=pltpu.SMEM),
            pl.BlockSpec((tq, hw), lambda h, b, i: (b * nq + i, h)),
            pl.BlockSpec((seq, hw), lambda h, b, i: (b, A_HEADS + h)),
            pl.BlockSpec((seq, hw), lambda h, b, i: (b, 2 * A_HEADS + h)),
            pl.BlockSpec((META_PAD, hw), lambda h, b, i: (0, A_HEADS + h)),
            pl.BlockSpec((META_PAD, hw), lambda h, b, i: (0, 2 * A_HEADS + h)),
            pl.BlockSpec((1, hw), lambda h, b, i: (0, 0)),
        ],
        out_specs=pl.BlockSpec((tq, hw), lambda h, b, i: (b * nq + i, h)),
        out_shape=jax.ShapeDtypeStruct((n_tok, A_HEADS * hw), BF16),
        scratch_shapes=[pltpu.VMEM((META_PAD + seq, hw), BF16), pltpu.VMEM((META_PAD + seq, hw), BF16),
                        pltpu.VMEM((sub, META_PAD + 2 * seq - sub), F32)],
        compiler_params=_cparams(("arbitrary", "arbitrary", "arbitrary"), 48),
        name="diff_attn",
    )(params, proj, proj, proj, meta_proj, meta_proj, subln_g)


def _split_cumsum(tri, x):
    hi = x.astype(BF16)
    lo = (x - hi.astype(F32)).astype(BF16)
    return (jnp.dot(tri, hi, preferred_element_type=F32) + jnp.dot(tri, lo, preferred_element_type=F32))


def _hgrn_chunk(q, k, logf, v_bf, state_t, reverse):
    c = q.shape[0]
    r = lax.broadcasted_iota(jnp.int32, (c, c), 0)
    s = lax.broadcasted_iota(jnp.int32, (c, c), 1)
    if reverse:
        visible = s >= r
        ref_row, last_row = c - 1 - c // 2, 0
    else:
        visible = s <= r
        ref_row, last_row = c // 2, c - 1
    tri = jnp.where(visible, 1.0, 0.0).astype(BF16)
    b = _split_cumsum(tri, logf)
    b_ref = b[ref_row:ref_row + 1, :]
    b_last = b[last_row:last_row + 1, :]
    nt = (((1,), (1,)), ((), ()))
    tn = (((0,), (0,)), ((), ()))
    qd = (q * jnp.exp(b - b_ref)).astype(BF16)
    kd = (k * jnp.exp(b_ref - b)).astype(BF16)
    scores = lax.dot_general(qd, kd, nt, preferred_element_type=F32)
    scores = jnp.where(visible, scores, 0.0).astype(BF16)
    o_intra = jnp.dot(scores, v_bf, preferred_element_type=F32)
    qe = (q * jnp.exp(b)).astype(BF16)
    o_inter = lax.dot_general(qe, state_t.astype(BF16), nt, preferred_element_type=F32)
    ks = (k * jnp.exp(b_last - b)).astype(BF16)
    d_state_t = lax.dot_general(v_bf, ks, tn, preferred_element_type=F32)
    new_state_t = state_t * jnp.exp(b_last) + d_state_t
    return o_intra + o_inter, new_state_t


def _forget_gate(logit, lb):
    f = lb + (1.0 - lb) * jax.nn.sigmoid(logit)
    return 1.0 - f, jnp.log(f)


def _chunk_sum_operators(reverse):
    g = HGRN_GROUP
    r = lax.broadcasted_iota(jnp.int32, (g, g), 0)
    s = lax.broadcasted_iota(jnp.int32, (g, g), 1)
    shift = H_CHUNK.bit_length() - 1
    same = (r >> shift) == (s >> shift)
    s_loc = s & (H_CHUNK - 1)
    if reverse:
        cum, ref = s >= r, s_loc >= H_CHUNK - 1 - H_CHUNK // 2
    else:
        cum, ref = s <= r, s_loc <= H_CHUNK // 2
    ops = [jnp.logical_and(same, cum), jnp.logical_and(same, ref), same]
    return jnp.concatenate([jnp.where(o, 1.0, 0.0) for o in ops], axis=0).astype(BF16)


def _hgrn_kernel(q_ref, ff_ref, fb_ref, v_ref, hg_ref, mff_ref, mv_ref, lbf_ref, lbb_ref, g_ref, o_ref,
                 qdf_ref, kdf_ref, qdb_ref, kdb_ref, qe_ref, ks_ref, dec_ref, ds_ref, sp_ref, acc_ref):
    seq = q_ref.shape[0]
    n_heads = q_ref.shape[1] // H_DV
    n_chunks = seq // H_CHUNK
    dk = H_EXPAND
    nt = (((1,), (1,)), ((), ()))
    tn = (((0,), (0,)), ((), ()))
    ops_f = _chunk_sum_operators(False)
    ops_b = _chunk_sum_operators(True)
    r = lax.broadcasted_iota(jnp.int32, (2 * H_CHUNK, 2 * H_CHUNK), 0)
    s = lax.broadcasted_iota(jnp.int32, (2 * H_CHUNK, 2 * H_CHUNK), 1)
    same_chunk = (r >= H_CHUNK) == (s >= H_CHUNK)
    vis_f = jnp.logical_and(same_chunk, s <= r)
    vis_b = jnp.logical_and(same_chunk, s >= r)
    mrow = lax.broadcasted_iota(jnp.int32, (H_CHUNK, 1), 0)
    is_meta = mrow >= H_CHUNK - N_META
    zeros_q = jnp.zeros((H_CHUNK, dk), F32)
    state0 = jnp.zeros((H_DV, dk), F32)
    chunks_per_group = HGRN_GROUP // H_CHUNK

    def chunk_rows(x):
        return jnp.concatenate([x[c * H_CHUNK:c * H_CHUNK + 1] for c in range(chunks_per_group)], axis=0)

    def per_chunk(y):
        return jnp.concatenate([jnp.broadcast_to(y[c:c + 1], (H_CHUNK, y.shape[1]))
                                for c in range(chunks_per_group)], axis=0)

    def dec_row(c):
        return pl.ds(pl.multiple_of(c * DEC_ROWS, DEC_ROWS), DEC_ROWS)

    for j in range(n_heads):
        head = pl.ds(j * H_DV, H_DV)
        lbf = lbf_ref[:, head]
        lbb = lbb_ref[:, head]

        def prepare(g, carry):
            rows = pl.ds(pl.multiple_of(g * HGRN_GROUP, HGRN_GROUP), HGRN_GROUP)
            q = jax.nn.silu(q_ref[rows, head].astype(F32))
            for logit_ref, lb, ops, qd_ref, kd_ref, col in ((ff_ref, lbf, ops_f, qdf_ref, kdf_ref, 0),
                                                           (fb_ref, lbb, ops_b, qdb_ref, kdb_ref, dk)):
                k, logf = _forget_gate(logit_ref[rows, head].astype(F32), lb)
                hi = logf.astype(BF16)
                lo = (logf - hi.astype(F32)).astype(BF16)
                sums = jnp.dot(ops, jnp.concatenate([hi, lo], axis=1), preferred_element_type=F32)
                sums = sums[:, 0:dk] + sums[:, dk:]
                b = sums[0:HGRN_GROUP]
                b_mid = sums[HGRN_GROUP:2 * HGRN_GROUP]
                b_all = sums[2 * HGRN_GROUP:]
                e_mid = per_chunk(jnp.exp(chunk_rows(b_mid)))
                e_all = jnp.exp(chunk_rows(b_all))
                e_rest = per_chunk(jnp.exp(chunk_rows(b_all) - chunk_rows(b_mid)))
                qd = q * jnp.exp(b - b_mid)
                kd = k * jnp.exp(b_mid - b)
                qd_ref[rows, :] = qd.astype(BF16)
                kd_ref[rows, :] = kd.astype(BF16)
                qe_ref[rows, col:col + dk] = (qd * e_mid).astype(BF16)
                ks_ref[rows, col:col + dk] = (kd * e_rest).astype(BF16)
                group_rows = chunks_per_group * DEC_ROWS
                dec_ref[pl.ds(pl.multiple_of(g * group_rows, group_rows), group_rows), col:col + dk] = (
                    jnp.concatenate([jnp.broadcast_to(e_all[c:c + 1], (DEC_ROWS, dk))
                                     for c in range(chunks_per_group)], axis=0))
            return carry

        lax.fori_loop(0, seq // HGRN_GROUP, prepare, 0, unroll=2)

        def intra(p, carry):
            rows = pl.ds(pl.multiple_of(p * (2 * H_CHUNK), 2 * H_CHUNK), 2 * H_CHUNK)
            s_f = lax.dot_general(qdf_ref[rows, :], kdf_ref[rows, :], nt, preferred_element_type=F32)
            s_b = lax.dot_general(qdb_ref[rows, :], kdb_ref[rows, :], nt, preferred_element_type=F32)
            w = (jnp.where(vis_f, s_f, 0.0) + jnp.where(vis_b, s_b, 0.0)).astype(BF16)
            acc_ref[rows, head] = jnp.dot(w, v_ref[rows, head], preferred_element_type=F32)
            for half in range(2):
                c = 2 * p + half
                rows_c = pl.ds(pl.multiple_of(c * H_CHUNK, H_CHUNK), H_CHUNK)
                ds_ref[pl.ds(pl.multiple_of(c * H_DV, H_DV), H_DV), :] = lax.dot_general(
                    v_ref[rows_c, head], ks_ref[rows_c, :], tn, preferred_element_type=F32)
            return carry

        lax.fori_loop(0, n_chunks // 2, intra, 0, unroll=HGRN_UNROLL // 2)

        mk, mlogf = _forget_gate(mff_ref[:, head].astype(F32), lbf)
        mk = jnp.where(is_meta, mk, 0.0)
        mlogf = jnp.where(is_meta, mlogf, 0.0)
        state_f0 = _hgrn_chunk(zeros_q, mk, mlogf, mv_ref[:, head], state0, reverse=False)[1]

        def scan(i, carry):
            st_f, st_b = carry
            cf = i
            cb = n_chunks - 1 - i
            blk_f = pl.ds(pl.multiple_of(cf * H_DV, H_DV), H_DV)
            blk_b = pl.ds(pl.multiple_of(cb * H_DV, H_DV), H_DV)
            sp_ref[blk_f, 0:dk] = st_f.astype(BF16)
            sp_ref[blk_b, dk:] = st_b.astype(BF16)
            st_f = st_f * dec_ref[dec_row(cf), 0:dk][0:1] + ds_ref[blk_f, 0:dk]
            st_b = st_b * dec_ref[dec_row(cb), dk:][0:1] + ds_ref[blk_b, dk:]
            return st_f, st_b

        lax.fori_loop(0, n_chunks, scan, (state_f0, state0))

        def inter(c, carry):
            rows = pl.ds(pl.multiple_of(c * H_CHUNK, H_CHUNK), H_CHUNK)
            state = sp_ref[pl.ds(pl.multiple_of(c * H_DV, H_DV), H_DV), :]
            acc_ref[rows, head] += lax.dot_general(qe_ref[rows, :], state, nt, preferred_element_type=F32)
            return carry

        lax.fori_loop(0, n_chunks, inter, 0, unroll=HGRN_UNROLL)

    for j in range(n_heads):
        head = pl.ds(j * H_DV, H_DV)
        o = acc_ref[:, head]
        ms = jnp.mean(o * o, axis=-1, keepdims=True)
        normed = o * lax.rsqrt(ms + EPS) * g_ref[...]
        o_ref[:, head] = (normed * jax.nn.sigmoid(hg_ref[:, head].astype(F32))).astype(o_ref.dtype)


def _hgrn(proj, meta_proj, lb_fwd, lb_bwd, norm_g, n_seq, seq):
    n_tok = proj.shape[0]
    hps = HGRN_HEADS_PER_STEP
    w = hps * H_DV
    col0 = 3 * A_HEADS * (2 * A_HEAD_DIM) // w
    per_sec = H_HEADS // hps
    meta_blk = META_PAD // H_CHUNK - 1

    def sec(k):
        return pl.BlockSpec((seq, w), lambda b, h, k=k: (b, col0 + k * per_sec + h))

    def meta_sec(k):
        return pl.BlockSpec((H_CHUNK, w), lambda b, h, k=k: (meta_blk, col0 + k * per_sec + h))

    return pl.pallas_call(
        _hgrn_kernel,
        grid=(n_seq, per_sec),
        in_specs=[
            sec(0), sec(1), sec(2), sec(3), sec(4),
            meta_sec(1), meta_sec(3),
            pl.BlockSpec((1, w), lambda b, h: (0, h)),
            pl.BlockSpec((1, w), lambda b, h: (0, h)),
            pl.BlockSpec((1, H_DV), lambda b, h: (0, 0)),
        ],
        out_specs=pl.BlockSpec((seq, w), lambda b, h: (b, h)),
        out_shape=jax.ShapeDtypeStruct((n_tok, H_HEADS * H_DV), BF16),
        scratch_shapes=[
            pltpu.VMEM((seq, H_EXPAND), BF16), pltpu.VMEM((seq, H_EXPAND), BF16),
            pltpu.VMEM((seq, H_EXPAND), BF16), pltpu.VMEM((seq, H_EXPAND), BF16),
            pltpu.VMEM((seq, 2 * H_EXPAND), BF16),
            pltpu.VMEM((seq, 2 * H_EXPAND), BF16),
            pltpu.VMEM((seq // H_CHUNK * DEC_ROWS, 2 * H_EXPAND), F32),
            pltpu.VMEM((seq // H_CHUNK * H_DV, 2 * H_EXPAND), F32),
            pltpu.VMEM((seq // H_CHUNK * H_DV, 2 * H_EXPAND), BF16),
            pltpu.VMEM((seq, w), F32),
        ],
        compiler_params=_cparams(("parallel", "parallel"), 48),
        name="hgrn",
    )(proj, proj, proj, proj, proj, meta_proj, meta_proj, lb_fwd, lb_bwd, norm_g)


def _merge_kernel(ao_ref, ho_ref, ga_ref, gh_ref, xp_ref, xs_ref, wa_ref, wh_ref, wo_ref, lng_ref, lnb_ref,
                  l1g_ref, l1b_ref, hf_ref, hp_ref, *, tp):
    pa = jnp.dot(ao_ref[...], wa_ref[...], preferred_element_type=F32)
    ph = jnp.dot(ho_ref[...], wh_ref[...], preferred_element_type=F32)
    merged = jax.nn.sigmoid(ga_ref[...].astype(F32)) * pa + jax.nn.sigmoid(gh_ref[...].astype(F32)) * ph
    t = jnp.dot(merged.astype(BF16), wo_ref[...], preferred_element_type=F32)

    def finish(x_ref):
        x_in = _layer_norm_f32(x_ref[...], lng_ref[...], lnb_ref[...])
        h = _layer_norm_f32(DN_ALPHA * x_in + t, l1g_ref[...], l1b_ref[...])
        hf_ref[...] = h
        hp_ref[...] = _pack_bf16_pair(h)

    @pl.when(pl.program_id(0) < tp)
    def _():
        finish(xp_ref)

    @pl.when(pl.program_id(0) >= tp)
    def _():
        finish(xs_ref)


def _merge(ao, ho, proj, xp, xs, wa, wh, wo, ln_g, ln_b, l1g, l1b, tm):
    d = xp.shape[1]
    tp, ts = xp.shape[0] // tm, xs.shape[0] // tm
    n_tok = (tp + ts) * tm
    gate_blk0 = (proj.shape[1] - 2 * d) // d
    vec = pl.BlockSpec((1, d), lambda i: (0, 0))
    xp_spec, xs_spec = _two_group_specs(tm, d, tp, False)
    return pl.pallas_call(
        functools.partial(_merge_kernel, tp=tp),
        grid=(tp + ts,),
        in_specs=[
            pl.BlockSpec((tm, ao.shape[1]), lambda i: (i, 0)),
            pl.BlockSpec((tm, ho.shape[1]), lambda i: (i, 0)),
            pl.BlockSpec((tm, d), lambda i: (i, gate_blk0)),
            pl.BlockSpec((tm, d), lambda i: (i, gate_blk0 + 1)),
            xp_spec, xs_spec,
            _resident(wa.shape), _resident(wh.shape), _resident(wo.shape),
            vec, vec, vec, vec,
        ],
        out_specs=[pl.BlockSpec((tm, d), lambda i: (i, 0)), pl.BlockSpec((tm, d // 2), lambda i: (i, 0))],
        out_shape=[jax.ShapeDtypeStruct((n_tok, d), F32), jax.ShapeDtypeStruct((n_tok, d // 2), U32)],
        compiler_params=_cparams(("parallel",), 56),
        name="merge",
    )(ao, ho, proj, proj, xp, xs, wa, wh, wo, ln_g, ln_b, l1g, l1b)


def _first_argmax(x, iota, size):
    m = jnp.max(x, axis=0, keepdims=True)
    idx = jnp.min(jnp.where(x == m, iota, size), axis=0, keepdims=True)
    return m, idx


def _router_kernel(h_ref, wrt_ref, bias_ref, idx_ref, gate_ref, rank_ref, cnt_ref, base_ref):
    tm = h_ref.shape[0]
    half = h_ref.shape[1]

    @pl.when(pl.program_id(0) == 0)
    def _():
        base_ref[...] = jnp.zeros_like(base_ref)

    nt = (((1,), (1,)), ((), ()))
    h_lo, h_hi = _unpack_bf16_pair(h_ref[...])
    logits = (lax.dot_general(wrt_ref[:, 0:half], h_lo, nt, preferred_element_type=F32)
              + lax.dot_general(wrt_ref[:, half:], h_hi, nt, preferred_element_type=F32))
    scores = jax.nn.sigmoid(logits)
    biased = scores + bias_ref[...]
    neg_inf = jnp.asarray(-jnp.inf, F32)

    iota_g = lax.broadcasted_iota(jnp.int32, (GROUP_SIZE, tm), 0)
    iota_n = lax.broadcasted_iota(jnp.int32, (N_GROUPS, tm), 0)
    grp = jnp.full((N_GROUPS, tm), neg_inf, F32)
    for g in range(N_GROUPS):
        xg = biased[g * GROUP_SIZE:(g + 1) * GROUP_SIZE, :]
        m1, i1 = _first_argmax(xg, iota_g, GROUP_SIZE)
        m2 = jnp.max(jnp.where(iota_g == i1, neg_inf, xg), axis=0, keepdims=True)
        grp = jnp.where(iota_n == g, m1 + m2, grp)

    keep_f = jnp.zeros((N_GROUPS, tm), F32)
    for _ in range(TOPK_GROUPS):
        _, ig = _first_argmax(grp, iota_n, N_GROUPS)
        sel = iota_n == ig
        keep_f = jnp.where(sel, 1.0, keep_f)
        grp = jnp.where(sel, neg_inf, grp)

    masked = jnp.concatenate(
        [jnp.where(keep_f[g:g + 1, :] > 0.5, biased[g * GROUP_SIZE:(g + 1) * GROUP_SIZE, :], neg_inf)
         for g in range(N_GROUPS)], axis=0)

    iota_e = lax.broadcasted_iota(jnp.int32, (N_EXPERTS, tm), 0)
    chosen = jnp.zeros((N_EXPERTS, tm), F32)
    idxs, gates = [], []
    for _ in range(TOP_K):
        _, ie = _first_argmax(masked, iota_e, N_EXPERTS)
        sel = iota_e == ie
        gates.append(jnp.sum(jnp.where(sel, scores, 0.0), axis=0, keepdims=True))
        idxs.append(ie)
        chosen = jnp.where(sel, 1.0, chosen)
        masked = jnp.where(sel, neg_inf, masked)

    denom = gates[0]
    for gk in gates[1:]:
        denom = denom + gk

    r = lax.broadcasted_iota(jnp.int32, (tm, tm), 0)
    c = lax.broadcasted_iota(jnp.int32, (tm, tm), 1)
    before = jnp.where(r < c, 1.0, 0.0).astype(BF16)
    pos = jnp.dot(chosen.astype(BF16), before, preferred_element_type=F32) + base_ref[...]
    for k in range(TOP_K):
        sel = iota_e == idxs[k]
        rank_ref[k:k + 1, :] = jnp.sum(jnp.where(sel, pos, 0.0), axis=0, keepdims=True).astype(jnp.int32)
        idx_ref[k:k + 1, :] = idxs[k]
        gate_ref[k:k + 1, :] = gates[k] / denom * ROUTED_SCALE

    base_ref[...] = base_ref[...] + jnp.sum(chosen, axis=1, keepdims=True)
    cnt_ref[...] = base_ref[...]


def _router(h_pk, w_router_t, bias_col, tm):
    n_tok, half = h_pk.shape
    row_blk = pl.BlockSpec((TOP_K, tm), lambda i: (0, i))
    return pl.pallas_call(
        _router_kernel,
        grid=(n_tok // tm,),
        in_specs=[
            pl.BlockSpec((tm, half), lambda i: (i, 0)),
            _resident(w_router_t.shape),
            pl.BlockSpec((N_EXPERTS, 1), lambda i: (0, 0)),
        ],
        out_specs=[row_blk, row_blk, row_blk, pl.BlockSpec((N_EXPERTS, 1), lambda i: (0, 0))],
        out_shape=[
            jax.ShapeDtypeStruct((TOP_K, n_tok), jnp.int32),
            jax.ShapeDtypeStruct((TOP_K, n_tok), F32),
            jax.ShapeDtypeStruct((TOP_K, n_tok), jnp.int32),
            jax.ShapeDtypeStruct((N_EXPERTS, 1), F32),
        ],
        scratch_shapes=[pltpu.VMEM((N_EXPERTS, 1), F32)],
        compiler_params=_cparams(("arbitrary",), 32),
        name="router",
    )(h_pk, w_router_t, bias_col)


def _dest_kernel(idx_ref, rank_ref, start_ref, dest_ref):
    tm = idx_ref.shape[1]
    iota_e = lax.broadcasted_iota(jnp.int32, (N_EXPERTS, tm), 0)
    start = start_ref[...]
    for k in range(TOP_K):
        sel = iota_e == idx_ref[k:k + 1, :]
        base = jnp.sum(jnp.where(sel, start, 0.0), axis=0, keepdims=True)
        dest_ref[k:k + 1, :] = base.astype(jnp.int32) + rank_ref[k:k + 1, :]


def _dest(idx, rank, start_col, tm):
    n_tok = idx.shape[1]
    row_blk = pl.BlockSpec((TOP_K, tm), lambda i: (0, i))
    return pl.pallas_call(
        _dest_kernel,
        grid=(n_tok // tm,),
        in_specs=[row_blk, row_blk, pl.BlockSpec((N_EXPERTS, 1), lambda i: (0, 0))],
        out_specs=row_blk,
        out_shape=jax.ShapeDtypeStruct((TOP_K, n_tok), jnp.int32),
        compiler_params=_cparams(("parallel",), 32),
        name="dest",
    )(idx, rank, start_col)


def _dispatch_kernel(pstart_ref, cnt_ref, tail_ref, dest_ref, h_ref, xs_ref, zero_ref, sem, pad_sem, tail_sem):
    tm = h_ref.shape[0]

    def pad_copy(e, j):
        return pltpu.make_async_copy(zero_ref.at[pl.ds(0, 1)],
                                     xs_ref.at[pl.ds(pstart_ref[e] + cnt_ref[e] + j, 1)], pad_sem)

    def tail_copy():
        return pltpu.make_async_copy(
            zero_ref, xs_ref.at[pl.ds(pl.multiple_of(tail_ref[0], ROW_ALIGN), EXPERT_ROWS)], tail_sem)

    def for_each_pad_row(action):
        def body(e, carry):
            n_pad = (ROW_ALIGN - cnt_ref[e] % ROW_ALIGN) % ROW_ALIGN
            for j in range(ROW_ALIGN - 1):
                @pl.when(j < n_pad)
                def _():
                    action(pad_copy(e, j))
            return carry
        lax.fori_loop(0, N_EXPERTS, body, 0)

    @pl.when(pl.program_id(0) == 0)
    def _():
        zero_ref[...] = jnp.zeros_like(zero_ref)
        tail_copy().start()
        for_each_pad_row(lambda c: c.start())
        for_each_pad_row(lambda c: c.wait())
        tail_copy().wait()

    def row_copy(r, k):
        return pltpu.make_async_copy(h_ref.at[pl.ds(r, 1)], xs_ref.at[pl.ds(dest_ref[0, k * tm + r], 1)], sem)

    def start(r, carry):
        for k in range(TOP_K):
            row_copy(r, k).start(priority=k % DMA_THREADS)
        return carry

    def wait(r, carry):
        for k in range(TOP_K):
            row_copy(r, k).wait()
        return carry

    lax.fori_loop(0, tm, start, 0)
    lax.fori_loop(0, tm, wait, 0)


def _dispatch(h_pk, dest_tiles, pstart, counts, tail, n_rows, tm):
    n_tok, half = h_pk.shape
    grid_spec = pltpu.PrefetchScalarGridSpec(
        num_scalar_prefetch=3,
        grid=(n_tok // tm,),
        in_specs=[
            pl.BlockSpec((None, 1, TOP_K * tm), lambda i, *_: (i, 0, 0), memory_space=pltpu.SMEM),
            pl.BlockSpec((tm, half), lambda i, *_: (i, 0)),
        ],
        out_specs=pl.BlockSpec(memory_space=pl.ANY),
        scratch_shapes=[pltpu.VMEM((EXPERT_ROWS, half), U32), pltpu.SemaphoreType.DMA,
                        pltpu.SemaphoreType.DMA, pltpu.SemaphoreType.DMA],
    )
    return pl.pallas_call(
        _dispatch_kernel,
        grid_spec=grid_spec,
        out_shape=jax.ShapeDtypeStruct((n_rows, half), U32),
        compiler_params=_cparams(("arbitrary",), 32, disable_bounds_checks=True),
        name="dispatch",
    )(pstart, counts, tail, dest_tiles, h_pk)


def _experts_kernel(row_ref, e_ref, valid_ref, newe_ref, slot_ref, nexte_ref,
                    x_hbm, wg_hbm, wu_hbm, wd_hbm, y_hbm,
                    x_buf, y_buf, wg_buf, wu_buf, wd_buf, wgu_bf, wd_bf, x_sems, y_sems, sems):
    n_steps = pl.num_programs(0) * WINDOWS_PER_STEP
    ff = wg_buf.shape[2]

    def window(step):
        return pl.ds(pl.multiple_of(row_ref[step], ROW_ALIGN), EXPERT_ROWS)

    def x_copy(step, slot):
        return pltpu.make_async_copy(x_hbm.at[window(step)], x_buf.at[slot], x_sems.at[slot])

    def y_copy(step, slot):
        return pltpu.make_async_copy(y_buf.at[slot], y_hbm.at[window(step)], y_sems.at[slot])

    def weight_copies(e, slot):
        copies = []
        for m, (src, dst) in enumerate(((wg_hbm, wg_buf), (wu_hbm, wu_buf), (wd_hbm, wd_buf))):
            rows = dst.shape[1] // WEIGHT_DMA_CHUNKS
            for c in range(WEIGHT_DMA_CHUNKS):
                part = pl.ds(c * rows, rows)
                copies.append(pltpu.make_async_copy(src.at[0, e, part], dst.at[slot, part],
                                                    sems.at[slot, m * WEIGHT_DMA_CHUNKS + c]))
        return copies

    def process(i, cur):
        @pl.when(i == 0)
        def _():
            x_copy(0, 0).start()

        @pl.when(newe_ref[i] == 1)
        def _():
            slot = slot_ref[i]
            e = e_ref[i]

            @pl.when(i == 0)
            def _():
                for c in weight_copies(e, slot):
                    c.start(priority=WEIGHT_DMA_THREAD)

            for c in weight_copies(e, slot):
                c.wait()
            nxt = nexte_ref[i]

            @pl.when(nxt >= 0)
            def _():
                for c in weight_copies(nxt, 1 - slot):
                    c.start(priority=WEIGHT_DMA_THREAD)

            wgu_bf[:, 0:ff] = wg_buf[slot].astype(BF16)
            wgu_bf[:, ff:] = wu_buf[slot].astype(BF16)
            wd_bf[...] = wd_buf[slot].astype(BF16)

        @pl.when(valid_ref[i] == 1)
        def _():
            nxt_step = jnp.minimum(i + 1, n_steps - 1)
            has_next = jnp.logical_and(i + 1 < n_steps, valid_ref[nxt_step] == 1)
            x_copy(i, cur).wait()

            @pl.when(has_next)
            def _():
                x_copy(nxt_step, 1 - cur).start()

            gu = _dot_packed(x_buf[cur], wgu_bf)
            hid = (jax.nn.silu(gu[:, 0:ff]) * gu[:, ff:]).astype(BF16)
            y_buf[cur] = _pack_bf16_pair(jnp.dot(hid, wd_bf[...], preferred_element_type=F32))

            @pl.when(i > 0)
            def _():
                y_copy(i - 1, 1 - cur).wait()

            y_copy(i, cur).start()

            @pl.when(jnp.logical_not(has_next))
            def _():
                y_copy(i, cur).wait()

    for u in range(WINDOWS_PER_STEP):
        process(pl.program_id(0) * WINDOWS_PER_STEP + u, u % 2)


def _experts(x_sorted, steps, w_gate, w_up, w_down):
    n_rows, half = x_sorted.shape
    d = 2 * half
    ff = w_gate.shape[-1]
    hbm = pl.BlockSpec(memory_space=pl.ANY)
    grid_spec = pltpu.PrefetchScalarGridSpec(
        num_scalar_prefetch=len(steps),
        grid=(steps[0].shape[0] // WINDOWS_PER_STEP,),
        in_specs=[hbm, hbm, hbm, hbm],
        out_specs=hbm,
        scratch_shapes=[pltpu.VMEM((2, EXPERT_ROWS, half), U32), pltpu.VMEM((2, EXPERT_ROWS, half), U32),
                        pltpu.VMEM((2, d, ff), F32), pltpu.VMEM((2, d, ff), F32), pltpu.VMEM((2, ff, d), F32),
                        pltpu.VMEM((d, 2 * ff), BF16), pltpu.VMEM((ff, d), BF16),
                        pltpu.SemaphoreType.DMA((2,)), pltpu.SemaphoreType.DMA((2,)),
                        pltpu.SemaphoreType.DMA((2, 3 * WEIGHT_DMA_CHUNKS))],
    )
    return pl.pallas_call(
        _experts_kernel,
        grid_spec=grid_spec,
        out_shape=jax.ShapeDtypeStruct((n_rows, half), U32),
        compiler_params=_cparams(("arbitrary",), 52, disable_bounds_checks=True),
        name="experts",
    )(*steps, x_sorted, w_gate, w_up, w_down)


def _final_kernel(dest_ref, gate_ref, hf_ref, hp_ref, y_ref, wgu_ref, wd_ref, g_ref, b_ref, op_ref, os_ref,
                  ybuf, sem, *, tp):
    tm = hf_ref.shape[0]
    ff = wd_ref.shape[0]

    def row_copy(r, k):
        return pltpu.make_async_copy(y_ref.at[pl.ds(dest_ref[0, k * tm + r], 1)], ybuf.at[k, pl.ds(r, 1)], sem)

    def start(r, carry):
        for k in range(TOP_K):
            row_copy(r, k).start(priority=k % DMA_THREADS)
        return carry

    def wait(r, carry):
        for k in range(TOP_K):
            row_copy(r, k).wait()
        return carry

    lax.fori_loop(0, tm, start, 0)

    gu = _dot_packed(hp_ref[...], wgu_ref)
    hid = (jax.nn.silu(gu[:, 0:ff]) * gu[:, ff:]).astype(BF16)
    acc = DN_ALPHA * hf_ref[...] + jnp.dot(hid, wd_ref[...], preferred_element_type=F32)

    lax.fori_loop(0, tm, wait, 0)
    gate = gate_ref[...]
    half = ybuf.shape[2]
    acc_lo, acc_hi = acc[:, :half], acc[:, half:]
    for k in range(TOP_K):
        packed = ybuf[k]
        g_k = gate[:, k:k + 1]
        acc_lo = acc_lo + lax.bitcast_convert_type(packed << 16, F32) * g_k
        acc_hi = acc_hi + lax.bitcast_convert_type(packed & jnp.uint32(HIGH_HALF), F32) * g_k
    out = _layer_norm_f32(jnp.concatenate([acc_lo, acc_hi], axis=1), g_ref[...], b_ref[...])

    @pl.when(pl.program_id(0) < tp)
    def _():
        op_ref[...] = out

    @pl.when(pl.program_id(0) >= tp)
    def _():
        os_ref[...] = out


def _final(h_f32, h_pk, y_sorted, dest_tiles, gate_t, w_gu, w_d, g, b, n_prompt, tm):
    n_tok, d = h_f32.shape
    tp = n_prompt // tm
    ts = n_tok // tm - tp
    vec = pl.BlockSpec((1, d), lambda i: (0, 0))
    op_spec, os_spec = _two_group_specs(tm, d, tp, False)
    return pl.pallas_call(
        functools.partial(_final_kernel, tp=tp),
        grid=(tp + ts,),
        in_specs=[
            pl.BlockSpec((None, 1, TOP_K * tm), lambda i: (i, 0, 0), memory_space=pltpu.SMEM),
            pl.BlockSpec((tm, TOP_K), lambda i: (i, 0)),
            pl.BlockSpec((tm, d), lambda i: (i, 0)),
            pl.BlockSpec((tm, d // 2), lambda i: (i, 0)),
            pl.BlockSpec(memory_space=pl.ANY),
            _resident(w_gu.shape), _resident(w_d.shape), vec, vec,
        ],
        out_specs=[op_spec, os_spec],
        out_shape=[jax.ShapeDtypeStruct((tp * tm, d), F32), jax.ShapeDtypeStruct((ts * tm, d), F32)],
        scratch_shapes=[pltpu.VMEM((TOP_K, tm, d // 2), U32), pltpu.SemaphoreType.DMA],
        compiler_params=_cparams(("arbitrary",), 48, disable_bounds_checks=True),
        name="final",
    )(dest_tiles, gate_t, h_f32, h_pk, y_sorted, w_gu, w_d, g, b)


def _lower_bounds(gamma):
    return jnp.cumsum(jax.nn.softmax(gamma.astype(F32), axis=0), axis=0)


def _tile_major(dest, tm):
    k, n = dest.shape
    return jnp.transpose(dest.reshape(k, n // tm, tm), (1, 0, 2)).reshape(n // tm, 1, k * tm)


def _expert_steps(counts, n_assign):
    rows = EXPERT_ROWS
    n_steps = (n_assign + N_EXPERTS * (ROW_ALIGN - 1)) // rows + N_EXPERTS
    n_steps = (n_steps + WINDOWS_PER_STEP - 1) // WINDOWS_PER_STEP * WINDOWS_PER_STEP
    padded = (counts + ROW_ALIGN - 1) // ROW_ALIGN * ROW_ALIGN
    pend = jnp.cumsum(padded)
    pstart = pend - padded
    per_e = (padded + rows - 1) // rows
    st_end = jnp.cumsum(per_e)
    st_start = st_end - per_e
    total = st_end[-1]
    i = jnp.arange(n_steps, dtype=jnp.int32)
    ic = jnp.minimum(i, total - 1)
    e = jnp.sum((st_end[None, :] <= ic[:, None]).astype(jnp.int32), axis=1)
    j = ic - st_start[e]
    row0 = pstart[e] + jnp.minimum(j * rows, jnp.maximum(padded[e] - rows, 0))
    valid = i < total
    prev_e = jnp.concatenate([jnp.full((1,), -1, jnp.int32), e[:-1].astype(jnp.int32)])
    new_e = jnp.where(jnp.logical_and(valid, e != prev_e), 1, 0)
    nonempty = counts > 0
    slot = (jnp.cumsum(nonempty.astype(jnp.int32)) - 1) % 2
    ids = jnp.arange(N_EXPERTS, dtype=jnp.int32)
    later = jnp.concatenate([jnp.where(nonempty, ids, N_EXPERTS)[1:], jnp.full((1,), N_EXPERTS, jnp.int32)])
    nxt = jnp.flip(lax.cummin(jnp.flip(later)))
    nxt = jnp.where(nxt < N_EXPERTS, nxt, -1)
    as_i32 = lambda v: v.astype(jnp.int32)
    steps = (as_i32(row0), as_i32(e), as_i32(valid), as_i32(new_e), as_i32(slot[e]), as_i32(nxt[e]))
    n_rows = n_assign + N_EXPERTS * (ROW_ALIGN - 1) + rows
    return as_i32(pstart), as_i32(pend[-1:]), n_rows, steps


def kernel(x_prompt, x_sample, meta_tokens, ln_in_g, ln_in_b, hg_gamma_fwd, hg_gamma_bwd, w_in, lambda_q1, lambda_k1, lambda_q2, lambda_k2, attn_subln_g, hg_norm_g, w_branch_attn, w_branch_hgrn, w_out, ln1_g, ln1_b, w_router, router_bias, w_sh_gate, w_sh_up, w_sh_down, w_ex_gate, w_ex_up, w_ex_down, ln2_g, ln2_b):
    bp, seq, d = x_prompt.shape
    bs, seq_s, _ = x_sample.shape
    assert seq == seq_s and w_in.shape[0] == DEPTH
    n_seq = bp + bs
    n_tok = n_seq * seq
    layer = 0

    xp = x_prompt.reshape(bp * seq, d)
    xs = x_sample.reshape(bs * seq, d)
    row = lambda v: v.reshape(1, -1).astype(F32)
    ln_g, ln_b = row(ln_in_g), row(ln_in_b)
    w_in_bf = w_in[layer].astype(BF16)

    proj = _ln_matmul(xp, xs, ln_g, ln_b, w_in_bf, tm=min(1024, bs * seq), tn=1024)
    meta = meta_tokens.astype(F32)
    meta_proj = _ln_matmul(meta, meta, ln_g, ln_b, w_in_bf, tm=N_META, tn=1024)[:N_META]
    meta_proj = jnp.pad(meta_proj, ((META_PAD - N_META, 0), (0, 0)))

    lam = (jnp.exp(jnp.sum(lambda_q1[layer].astype(F32) * lambda_k1[layer].astype(F32)))
           - jnp.exp(jnp.sum(lambda_q2[layer].astype(F32) * lambda_k2[layer].astype(F32))) + LAMBDA_INIT)
    slopes = 2.0 ** (-8.0 * jnp.arange(1, A_HEADS + 1, dtype=F32) / A_HEADS)
    attn_params = jnp.concatenate([lam.reshape(1), slopes]).astype(F32)
    subln_g = row(attn_subln_g[layer]) * (1.0 - LAMBDA_INIT)
    ao = _diff_attn(proj, meta_proj, attn_params, subln_g, n_seq, seq, tq=min(2048, seq))

    lb_fwd = row(_lower_bounds(hg_gamma_fwd)[layer])
    lb_bwd = row(_lower_bounds(hg_gamma_bwd)[layer])
    ho = _hgrn(proj, meta_proj, lb_fwd, lb_bwd, row(hg_norm_g[layer]), n_seq, seq)

    h_f32, h_pk = _merge(ao, ho, proj, xp, xs, w_branch_attn[layer].astype(BF16),
                         w_branch_hgrn[layer].astype(BF16), w_out[layer].astype(BF16), ln_g, ln_b,
                         row(ln1_g[layer]), row(ln1_b[layer]), tm=min(256, bs * seq))

    idx, gate, rank, counts = _router(h_pk, w_router[layer].T.astype(BF16),
                                      router_bias[layer].astype(F32).reshape(N_EXPERTS, 1), tm=min(512, n_tok))
    counts = counts[:, 0].astype(jnp.int32)
    pstart, tail, n_rows, steps = _expert_steps(counts, n_tok * TOP_K)
    dest = _dest(idx, rank, pstart.astype(F32).reshape(N_EXPERTS, 1), tm=min(2048, n_tok))

    tm_d = min(256, n_tok)
    x_sorted = _dispatch(h_pk, _tile_major(dest, tm_d), pstart, counts, tail, n_rows, tm_d)
    y_sorted = _experts(x_sorted, steps, w_ex_gate, w_ex_up, w_ex_down)

    w_sh_gu = jnp.concatenate([w_sh_gate[layer], w_sh_up[layer]], axis=1).astype(BF16)
    tm_f = min(256, bs * seq)
    y_prompt, y_sample = _final(h_f32, h_pk, y_sorted, _tile_major(dest, tm_f), gate.T, w_sh_gu,
                                w_sh_down[layer].astype(BF16), row(ln2_g[layer]), row(ln2_b[layer]),
                                bp * seq, tm_f)
    return (y_prompt.reshape(bp, seq, d), y_sample.reshape(bs, seq, d))
```

```python
import functools
import math

import jax
import jax.numpy as jnp
from jax import lax
from jax.experimental import pallas as pl
from jax.experimental.pallas import tpu as pltpu

F32 = jnp.float32
BF16 = jnp.bfloat16
U32 = jnp.uint32

N_META = 16
A_HEADS = 8
A_HEAD_DIM = 64
H_HEADS = 8
H_EXPAND = 128
H_DV = 128
H_CHUNK = 64
N_EXPERTS = 256
TOP_K = 8
N_GROUPS = 8
TOPK_GROUPS = 4
GROUP_SIZE = N_EXPERTS // N_GROUPS
ROUTED_SCALE = 2.5
DEPTH = 1
DN_ALPHA = (2 * DEPTH) ** 0.25
EPS = 1e-5
LAMBDA_INIT = 0.8 - 0.6 * math.exp(-0.3 * 0)

LANES = 128
HGRN_HEADS_PER_STEP = 4
HGRN_GROUP = 256
DEC_ROWS = 8
HGRN_UNROLL = 32
META_PAD = 128
NEG_BIG = -1e30
EXPERT_ROWS = 256
WEIGHT_DMA_CHUNKS = 4
DMA_THREADS = 2
WEIGHT_DMA_THREAD = 1
ROW_ALIGN = 8
WINDOWS_PER_STEP = 2
ATTN_ROWS = 256
HIGH_HALF = 0xFFFF0000


def _cparams(semantics, vmem_mb, **kw):
    return pltpu.CompilerParams(dimension_semantics=semantics, vmem_limit_bytes=vmem_mb * 1024 * 1024, **kw)


def _layer_norm_f32(x, g, b):
    mu = jnp.mean(x, axis=-1, keepdims=True)
    xc = x - mu
    var = jnp.mean(xc * xc, axis=-1, keepdims=True)
    return xc * lax.rsqrt(var + EPS) * g + b


def _resident(shape):
    nd = len(shape)
    return pl.BlockSpec(shape, lambda *_: (0,) * nd, pipeline_mode=pl.Buffered(1))


def _pack_bf16_pair(x):
    k = x.shape[1] // 2
    lo = lax.bitcast_convert_type(x[:, :k].astype(BF16).astype(F32), U32) >> 16
    hi = lax.bitcast_convert_type(x[:, k:].astype(BF16).astype(F32), U32) & jnp.uint32(HIGH_HALF)
    return lo | hi


def _unpack_bf16_pair(p):
    lo = lax.bitcast_convert_type(p << 16, F32).astype(BF16)
    hi = lax.bitcast_convert_type(p & jnp.uint32(HIGH_HALF), F32).astype(BF16)
    return lo, hi


def _dot_packed(p, w_ref):
    k = p.shape[1]
    lo, hi = _unpack_bf16_pair(p)
    return (jnp.dot(lo, w_ref[0:k, :], preferred_element_type=F32)
            + jnp.dot(hi, w_ref[k:, :], preferred_element_type=F32))


def _two_group_specs(tm, d, tp, extra_grid_axes):
    if extra_grid_axes:
        return (pl.BlockSpec((tm, d), lambda i, j: (jnp.minimum(i, tp - 1), 0), pipeline_mode=pl.Buffered(1)),
                pl.BlockSpec((tm, d), lambda i, j: (jnp.maximum(i - tp, 0), 0), pipeline_mode=pl.Buffered(1)))
    return (pl.BlockSpec((tm, d), lambda i: (jnp.minimum(i, tp - 1), 0)),
            pl.BlockSpec((tm, d), lambda i: (jnp.maximum(i - tp, 0), 0)))


def _ln_matmul_kernel(xp_ref, xs_ref, g_ref, b_ref, w_ref, o_ref, xn_ref, *, tp):
    i = pl.program_id(0)
    first_col = pl.program_id(1) == 0

    @pl.when(jnp.logical_and(first_col, i < tp))
    def _():
        xn_ref[...] = _layer_norm_f32(xp_ref[...], g_ref[...], b_ref[...]).astype(BF16)

    @pl.when(jnp.logical_and(first_col, i >= tp))
    def _():
        xn_ref[...] = _layer_norm_f32(xs_ref[...], g_ref[...], b_ref[...]).astype(BF16)

    o_ref[...] = jnp.dot(xn_ref[...], w_ref[...], preferred_element_type=F32).astype(o_ref.dtype)


def _ln_matmul(xp, xs, g, b, w, tm, tn):
    d = xp.shape[1]
    n = w.shape[1]
    tp, ts = xp.shape[0] // tm, xs.shape[0] // tm
    xp_spec, xs_spec = _two_group_specs(tm, d, tp, True)
    return pl.pallas_call(
        functools.partial(_ln_matmul_kernel, tp=tp),
        grid=(tp + ts, n // tn),
        in_specs=[
            xp_spec, xs_spec,
            pl.BlockSpec((1, d), lambda i, j: (0, 0)),
            pl.BlockSpec((1, d), lambda i, j: (0, 0)),
            pl.BlockSpec((d, tn), lambda i, j: (0, j)),
        ],
        out_specs=pl.BlockSpec((tm, tn), lambda i, j: (i, j)),
        out_shape=jax.ShapeDtypeStruct(((tp + ts) * tm, n), BF16),
        scratch_shapes=[pltpu.VMEM((tm, d), BF16)],
        compiler_params=_cparams(("parallel", "arbitrary"), 44),
        name="ln_matmul",
    )(xp, xs, g, b, w)


def _diff_attn_kernel(par_ref, q_ref, k_ref, v_ref, km_ref, vm_ref, g_ref, o_ref, kall_ref, vall_ref, bias_ref):
    h = pl.program_id(0)
    b = pl.program_id(1)
    qi = pl.program_id(2)
    tq = q_ref.shape[0]
    sub = bias_ref.shape[0]
    n_keys = kall_ref.shape[0]
    shift = bias_ref.shape[1] - n_keys

    @pl.when(jnp.logical_and(b == 0, qi == 0))
    def _():
        i = lax.broadcasted_iota(jnp.int32, (sub, 1), 0)
        j = lax.broadcasted_iota(jnp.int32, (1, bias_ref.shape[1]), 1)
        bias_ref[...] = -par_ref[1 + h] * jnp.abs(i + (META_PAD + shift) - j).astype(F32)

    @pl.when(qi == 0)
    def _():
        kall_ref[0:META_PAD, :] = km_ref[...]
        kall_ref[META_PAD:, :] = k_ref[...]
        vall_ref[0:META_PAD, :] = vm_ref[...]
        vall_ref[META_PAD:, :] = v_ref[...]

    lam = par_ref[0]
    lane = lax.broadcasted_iota(jnp.int32, (1, 2 * A_HEAD_DIM), 1)
    scale = jnp.asarray(A_HEAD_DIM ** -0.5, BF16)
    zero = jnp.zeros((), BF16)
    lane_m = lax.broadcasted_iota(jnp.int32, (1, META_PAD), 1)
    pad_mask = jnp.where(lane_m < META_PAD - N_META, NEG_BIG, 0.0)
    nt = (((1,), (1,)), ((), ()))

    def one_pass(r, carry):
        rows = pl.ds(pl.multiple_of(r * sub, sub), sub)
        q = q_ref[rows, :]
        q1 = jnp.where(lane < A_HEAD_DIM, q, zero) * scale
        q2 = jnp.where(lane >= A_HEAD_DIM, q, zero) * scale
        k_all = kall_ref[...]
        s1 = lax.dot_general(q1, k_all, nt, preferred_element_type=F32)
        s2 = lax.dot_general(q2, k_all, nt, preferred_element_type=F32)

        off = pl.multiple_of(shift - (qi * tq + r * sub), LANES)
        bias_meta = bias_ref[:, pl.ds(off, META_PAD)] + pad_mask
        bias_seq = bias_ref[:, pl.ds(off + META_PAD, n_keys - META_PAD)]

        def softmax_parts(s):
            s_meta = s[:, 0:META_PAD] + bias_meta
            s_seq = s[:, META_PAD:] + bias_seq
            m = jnp.maximum(jnp.max(s_meta, axis=-1, keepdims=True), jnp.max(s_seq, axis=-1, keepdims=True))
            p_meta = jnp.exp(s_meta - m)
            p_seq = jnp.exp(s_seq - m)
            l = jnp.sum(p_meta, axis=-1, keepdims=True) + jnp.sum(p_seq, axis=-1, keepdims=True)
            return p_meta.astype(BF16), p_seq.astype(BF16), l

        p1_meta, p1_seq, l1 = softmax_parts(s1)
        p2_meta, p2_seq, l2 = softmax_parts(s2)
        a1 = (1.0 / l1).astype(BF16)
        a2 = (lam / l2).astype(BF16)
        w_meta = p1_meta * a1 - p2_meta * a2
        w_seq = p1_seq * a1 - p2_seq * a2
        o = (jnp.dot(w_meta, vall_ref[0:META_PAD, :], preferred_element_type=F32)
             + jnp.dot(w_seq, vall_ref[META_PAD:, :], preferred_element_type=F32))
        ms = jnp.mean(o * o, axis=-1, keepdims=True)
        o_ref[rows, :] = (o * lax.rsqrt(ms + EPS) * g_ref[...]).astype(o_ref.dtype)
        return carry

    lax.fori_loop(0, tq // sub, one_pass, 0, unroll=True)


def _diff_attn(proj, meta_proj, params, subln_g, n_seq, seq, tq):
    n_tok = proj.shape[0]
    nq = seq // tq
    sub = min(ATTN_ROWS, tq)
    hw = 2 * A_HEAD_DIM
    return pl.pallas_call(
        _diff_attn_kernel,
        grid=(A_HEADS, n_seq, nq),
        in_specs=[
            pl.BlockSpec(memory_space=pltpu.SMEM),
            pl.BlockSpec((tq, hw), lambda h, b, i: (b * nq + i, h)),
            pl.BlockSpec((seq, hw), lambda h, b, i: (b, A_HEADS + h)),
            pl.BlockSpec((seq, hw), lambda h, b, i: (b, 2 * A_HEADS + h)),
            pl.BlockSpec((META_PAD, hw), lambda h, b, i: (0, A_HEADS + h)),
            pl.BlockSpec((META_PAD, hw), lambda h, b, i: (0, 2 * A_HEADS + h)),
            pl.BlockSpec((1, hw), lambda h, b, i: (0, 0)),
        ],
        out_specs=pl.BlockSpec((tq, hw), lambda h, b, i: (b * nq + i, h)),
        out_shape=jax.ShapeDtypeStruct((n_tok, A_HEADS * hw), BF16),
        scratch_shapes=[pltpu.VMEM((META_PAD + seq, hw), BF16), pltpu.VMEM((META_PAD + seq, hw), BF16),
                        pltpu.VMEM((sub, META_PAD + 2 * seq - sub), F32)],
        compiler_params=_cparams(("arbitrary", "arbitrary", "arbitrary"), 48),
        name="diff_attn",
    )(params, proj, proj, proj, meta_proj, meta_proj, subln_g)


def _split_cumsum(tri, x):
    hi = x.astype(BF16)
    lo = (x - hi.astype(F32)).astype(BF16)
    return (jnp.dot(tri, hi, preferred_element_type=F32) + jnp.dot(tri, lo, preferred_element_type=F32))


def _hgrn_chunk(q, k, logf, v_bf, state_t, reverse):
    c = q.shape[0]
    r = lax.broadcasted_iota(jnp.int32, (c, c), 0)
    s = lax.broadcasted_iota(jnp.int32, (c, c), 1)
    if reverse:
        visible = s >= r
        ref_row, last_row = c - 1 - c // 2, 0
    else:
        visible = s <= r
        ref_row, last_row = c // 2, c - 1
    tri = jnp.where(visible, 1.0, 0.0).astype(BF16)
    b = _split_cumsum(tri, logf)
    b_ref = b[ref_row:ref_row + 1, :]
    b_last = b[last_row:last_row + 1, :]
    nt = (((1,), (1,)), ((), ()))
    tn = (((0,), (0,)), ((), ()))
    qd = (q * jnp.exp(b - b_ref)).astype(BF16)
    kd = (k * jnp.exp(b_ref - b)).astype(BF16)
    scores = lax.dot_general(qd, kd, nt, preferred_element_type=F32)
    scores = jnp.where(visible, scores, 0.0).astype(BF16)
    o_intra = jnp.dot(scores, v_bf, preferred_element_type=F32)
    qe = (q * jnp.exp(b)).astype(BF16)
    o_inter = lax.dot_general(qe, state_t.astype(BF16), nt, preferred_element_type=F32)
    ks = (k * jnp.exp(b_last - b)).astype(BF16)
    d_state_t = lax.dot_general(v_bf, ks, tn, preferred_element_type=F32)
    new_state_t = state_t * jnp.exp(b_last) + d_state_t
    return o_intra + o_inter, new_state_t


def _forget_gate(logit, lb):
    f = lb + (1.0 - lb) * jax.nn.sigmoid(logit)
    return 1.0 - f, jnp.log(f)


def _chunk_sum_operators(reverse):
    g = HGRN_GROUP
    r = lax.broadcasted_iota(jnp.int32, (g, g), 0)
    s = lax.broadcasted_iota(jnp.int32, (g, g), 1)
    shift = H_CHUNK.bit_length() - 1
    same = (r >> shift) == (s >> shift)
    s_loc = s & (H_CHUNK - 1)
    if reverse:
        cum, ref = s >= r, s_loc >= H_CHUNK - 1 - H_CHUNK // 2
    else:
        cum, ref = s <= r, s_loc <= H_CHUNK // 2
    ops = [jnp.logical_and(same, cum), jnp.logical_and(same, ref), same]
    return jnp.concatenate([jnp.where(o, 1.0, 0.0) for o in ops], axis=0).astype(BF16)


def _hgrn_kernel(q_ref, ff_ref, fb_ref, v_ref, hg_ref, mff_ref, mv_ref, lbf_ref, lbb_ref, g_ref, o_ref,
                 qdf_ref, kdf_ref, qdb_ref, kdb_ref, qe_ref, ks_ref, dec_ref, ds_ref, sp_ref, acc_ref):
    seq = q_ref.shape[0]
    n_heads = q_ref.shape[1] // H_DV
    n_chunks = seq // H_CHUNK
    dk = H_EXPAND
    nt = (((1,), (1,)), ((), ()))
    tn = (((0,), (0,)), ((), ()))
    ops_f = _chunk_sum_operators(False)
    ops_b = _chunk_sum_operators(True)
    r = lax.broadcasted_iota(jnp.int32, (2 * H_CHUNK, 2 * H_CHUNK), 0)
    s = lax.broadcasted_iota(jnp.int32, (2 * H_CHUNK, 2 * H_CHUNK), 1)
    same_chunk = (r >= H_CHUNK) == (s >= H_CHUNK)
    vis_f = jnp.logical_and(same_chunk, s <= r)
    vis_b = jnp.logical_and(same_chunk, s >= r)
    mrow = lax.broadcasted_iota(jnp.int32, (H_CHUNK, 1), 0)
    is_meta = mrow >= H_CHUNK - N_META
    zeros_q = jnp.zeros((H_CHUNK, dk), F32)
    state0 = jnp.zeros((H_DV, dk), F32)
    chunks_per_group = HGRN_GROUP // H_CHUNK

    def chunk_rows(x):
        return jnp.concatenate([x[c * H_CHUNK:c * H_CHUNK + 1] for c in range(chunks_per_group)], axis=0)

    def per_chunk(y):
        return jnp.concatenate([jnp.broadcast_to(y[c:c + 1], (H_CHUNK, y.shape[1]))
                                for c in range(chunks_per_group)], axis=0)

    def dec_row(c):
        return pl.ds(pl.multiple_of(c * DEC_ROWS, DEC_ROWS), DEC_ROWS)

    for j in range(n_heads):
        head = pl.ds(j * H_DV, H_DV)
        lbf = lbf_ref[:, head]
        lbb = lbb_ref[:, head]

        def prepare(g, carry):
            rows = pl.ds(pl.multiple_of(g * HGRN_GROUP, HGRN_GROUP), HGRN_GROUP)
            q = jax.nn.silu(q_ref[rows, head].astype(F32))
            for logit_ref, lb, ops, qd_ref, kd_ref, col in ((ff_ref, lbf, ops_f, qdf_ref, kdf_ref, 0),
                                                           (fb_ref, lbb, ops_b, qdb_ref, kdb_ref, dk)):
                k, logf = _forget_gate(logit_ref[rows, head].astype(F32), lb)
                hi = logf.astype(BF16)
                lo = (logf - hi.astype(F32)).astype(BF16)
                sums = jnp.dot(ops, jnp.concatenate([hi, lo], axis=1), preferred_element_type=F32)
                sums = sums[:, 0:dk] + sums[:, dk:]
                b = sums[0:HGRN_GROUP]
                b_mid = sums[HGRN_GROUP:2 * HGRN_GROUP]
                b_all = sums[2 * HGRN_GROUP:]
                e_mid = per_chunk(jnp.exp(chunk_rows(b_mid)))
                e_all = jnp.exp(chunk_rows(b_all))
                e_rest = per_chunk(jnp.exp(chunk_rows(b_all) - chunk_rows(b_mid)))
                qd = q * jnp.exp(b - b_mid)
                kd = k * jnp.exp(b_mid - b)
                qd_ref[rows, :] = qd.astype(BF16)
                kd_ref[rows, :] = kd.astype(BF16)
                qe_ref[rows, col:col + dk] = (qd * e_mid).astype(BF16)
                ks_ref[rows, col:col + dk] = (kd * e_rest).astype(BF16)
                group_rows = chunks_per_group * DEC_ROWS
                dec_ref[pl.ds(pl.multiple_of(g * group_rows, group_rows), group_rows), col:col + dk] = (
                    jnp.concatenate([jnp.broadcast_to(e_all[c:c + 1], (DEC_ROWS, dk))
                                     for c in range(chunks_per_group)], axis=0))
            return carry

        lax.fori_loop(0, seq // HGRN_GROUP, prepare, 0, unroll=2)

        def intra(p, carry):
            rows = pl.ds(pl.multiple_of(p * (2 * H_CHUNK), 2 * H_CHUNK), 2 * H_CHUNK)
            s_f = lax.dot_general(qdf_ref[rows, :], kdf_ref[rows, :], nt, preferred_element_type=F32)
            s_b = lax.dot_general(qdb_ref[rows, :], kdb_ref[rows, :], nt, preferred_element_type=F32)
            w = (jnp.where(vis_f, s_f, 0.0) + jnp.where(vis_b, s_b, 0.0)).astype(BF16)
            acc_ref[rows, head] = jnp.dot(w, v_ref[rows, head], preferred_element_type=F32)
            for half in range(2):
                c = 2 * p + half
                rows_c = pl.ds(pl.multiple_of(c * H_CHUNK, H_CHUNK), H_CHUNK)
                ds_ref[pl.ds(pl.multiple_of(c * H_DV, H_DV), H_DV), :] = lax.dot_general(
                    v_ref[rows_c, head], ks_ref[rows_c, :], tn, preferred_element_type=F32)
            return carry

        lax.fori_loop(0, n_chunks // 2, intra, 0, unroll=HGRN_UNROLL // 2)

        mk, mlogf = _forget_gate(mff_ref[:, head].astype(F32), lbf)
        mk = jnp.where(is_meta, mk, 0.0)
        mlogf = jnp.where(is_meta, mlogf, 0.0)
        state_f0 = _hgrn_chunk(zeros_q, mk, mlogf, mv_ref[:, head], state0, reverse=False)[1]

        def scan(i, carry):
            st_f, st_b = carry
            cf = i
            cb = n_chunks - 1 - i
            blk_f = pl.ds(pl.multiple_of(cf * H_DV, H_DV), H_DV)
            blk_b = pl.ds(pl.multiple_of(cb * H_DV, H_DV), H_DV)
            sp_ref[blk_f, 0:dk] = st_f.astype(BF16)
            sp_ref[blk_b, dk:] = st_b.astype(BF16)
            st_f = st_f * dec_ref[dec_row(cf), 0:dk][0:1] + ds_ref[blk_f, 0:dk]
            st_b = st_b * dec_ref[dec_row(cb), dk:][0:1] + ds_ref[blk_b, dk:]
            return st_f, st_b

        lax.fori_loop(0, n_chunks, scan, (state_f0, state0))

        def inter(c, carry):
            rows = pl.ds(pl.multiple_of(c * H_CHUNK, H_CHUNK), H_CHUNK)
            state = sp_ref[pl.ds(pl.multiple_of(c * H_DV, H_DV), H_DV), :]
            acc_ref[rows, head] += lax.dot_general(qe_ref[rows, :], state, nt, preferred_element_type=F32)
            return carry

        lax.fori_loop(0, n_chunks, inter, 0, unroll=HGRN_UNROLL)

    for j in range(n_heads):
        head = pl.ds(j * H_DV, H_DV)
        o = acc_ref[:, head]
        ms = jnp.mean(o * o, axis=-1, keepdims=True)
        normed = o * lax.rsqrt(ms + EPS) * g_ref[...]
        o_ref[:, head] = (normed * jax.nn.sigmoid(hg_ref[:, head].astype(F32))).astype(o_ref.dtype)


def _hgrn(proj, meta_proj, lb_fwd, lb_bwd, norm_g, n_seq, seq):
    n_tok = proj.shape[0]
    hps = HGRN_HEADS_PER_STEP
    w = hps * H_DV
    col0 = 3 * A_HEADS * (2 * A_HEAD_DIM) // w
    per_sec = H_HEADS // hps
    meta_blk = META_PAD // H_CHUNK - 1

    def sec(k):
        return pl.BlockSpec((seq, w), lambda b, h, k=k: (b, col0 + k * per_sec + h))

    def meta_sec(k):
        return pl.BlockSpec((H_CHUNK, w), lambda b, h, k=k: (meta_blk, col0 + k * per_sec + h))

    return pl.pallas_call(
        _hgrn_kernel,
        grid=(n_seq, per_sec),
        in_specs=[
            sec(0), sec(1), sec(2), sec(3), sec(4),
            meta_sec(1), meta_sec(3),
            pl.BlockSpec((1, w), lambda b, h: (0, h)),
            pl.BlockSpec((1, w), lambda b, h: (0, h)),
            pl.BlockSpec((1, H_DV), lambda b, h: (0, 0)),
        ],
        out_specs=pl.BlockSpec((seq, w), lambda b, h: (b, h)),
        out_shape=jax.ShapeDtypeStruct((n_tok, H_HEADS * H_DV), BF16),
        scratch_shapes=[
            pltpu.VMEM((seq, H_EXPAND), BF16), pltpu.VMEM((seq, H_EXPAND), BF16),
            pltpu.VMEM((seq, H_EXPAND), BF16), pltpu.VMEM((seq, H_EXPAND), BF16),
            pltpu.VMEM((seq, 2 * H_EXPAND), BF16),
            pltpu.VMEM((seq, 2 * H_EXPAND), BF16),
            pltpu.VMEM((seq // H_CHUNK * DEC_ROWS, 2 * H_EXPAND), F32),
            pltpu.VMEM((seq // H_CHUNK * H_DV, 2 * H_EXPAND), F32),
            pltpu.VMEM((seq // H_CHUNK * H_DV, 2 * H_EXPAND), BF16),
            pltpu.VMEM((seq, w), F32),
        ],
        compiler_params=_cparams(("parallel", "parallel"), 48),
        name="hgrn",
    )(proj, proj, proj, proj, proj, meta_proj, meta_proj, lb_fwd, lb_bwd, norm_g)


def _merge_kernel(ao_ref, ho_ref, ga_ref, gh_ref, xp_ref, xs_ref, wa_ref, wh_ref, wo_ref, lng_ref, lnb_ref,
                  l1g_ref, l1b_ref, hf_ref, hp_ref, *, tp):
    pa = jnp.dot(ao_ref[...], wa_ref[...], preferred_element_type=F32)
    ph = jnp.dot(ho_ref[...], wh_ref[...], preferred_element_type=F32)
    merged = jax.nn.sigmoid(ga_ref[...].astype(F32)) * pa + jax.nn.sigmoid(gh_ref[...].astype(F32)) * ph
    t = jnp.dot(merged.astype(BF16), wo_ref[...], preferred_element_type=F32)

    def finish(x_ref):
        x_in = _layer_norm_f32(x_ref[...], lng_ref[...], lnb_ref[...])
        h = _layer_norm_f32(DN_ALPHA * x_in + t, l1g_ref[...], l1b_ref[...])
        hf_ref[...] = h
        hp_ref[...] = _pack_bf16_pair(h)

    @pl.when(pl.program_id(0) < tp)
    def _():
        finish(xp_ref)

    @pl.when(pl.program_id(0) >= tp)
    def _():
        finish(xs_ref)


def _merge(ao, ho, proj, xp, xs, wa, wh, wo, ln_g, ln_b, l1g, l1b, tm):
    d = xp.shape[1]
    tp, ts = xp.shape[0] // tm, xs.shape[0] // tm
    n_tok = (tp + ts) * tm
    gate_blk0 = (proj.shape[1] - 2 * d) // d
    vec = pl.BlockSpec((1, d), lambda i: (0, 0))
    xp_spec, xs_spec = _two_group_specs(tm, d, tp, False)
    return pl.pallas_call(
        functools.partial(_merge_kernel, tp=tp),
        grid=(tp + ts,),
        in_specs=[
            pl.BlockSpec((tm, ao.shape[1]), lambda i: (i, 0)),
            pl.BlockSpec((tm, ho.shape[1]), lambda i: (i, 0)),
            pl.BlockSpec((tm, d), lambda i: (i, gate_blk0)),
            pl.BlockSpec((tm, d), lambda i: (i, gate_blk0 + 1)),
            xp_spec, xs_spec,
            _resident(wa.shape), _resident(wh.shape), _resident(wo.shape),
            vec, vec, vec, vec,
        ],
        out_specs=[pl.BlockSpec((tm, d), lambda i: (i, 0)), pl.BlockSpec((tm, d // 2), lambda i: (i, 0))],
        out_shape=[jax.ShapeDtypeStruct((n_tok, d), F32), jax.ShapeDtypeStruct((n_tok, d // 2), U32)],
        compiler_params=_cparams(("parallel",), 56),
        name="merge",
    )(ao, ho, proj, proj, xp, xs, wa, wh, wo, ln_g, ln_b, l1g, l1b)


def _first_argmax(x, iota, size):
    m = jnp.max(x, axis=0, keepdims=True)
    idx = jnp.min(jnp.where(x == m, iota, size), axis=0, keepdims=True)
    return m, idx


def _router_kernel(h_ref, wrt_ref, bias_ref, idx_ref, gate_ref, rank_ref, cnt_ref, base_ref):
    tm = h_ref.shape[0]
    half = h_ref.shape[1]

    @pl.when(pl.program_id(0) == 0)
    def _():
        base_ref[...] = jnp.zeros_like(base_ref)

    nt = (((1,), (1,)), ((), ()))
    h_lo, h_hi = _unpack_bf16_pair(h_ref[...])
    logits = (lax.dot_general(wrt_ref[:, 0:half], h_lo, nt, preferred_element_type=F32)
              + lax.dot_general(wrt_ref[:, half:], h_hi, nt, preferred_element_type=F32))
    scores = jax.nn.sigmoid(logits)
    biased = scores + bias_ref[...]
    neg_inf = jnp.asarray(-jnp.inf, F32)

    iota_g = lax.broadcasted_iota(jnp.int32, (GROUP_SIZE, tm), 0)
    iota_n = lax.broadcasted_iota(jnp.int32, (N_GROUPS, tm), 0)
    grp = jnp.full((N_GROUPS, tm), neg_inf, F32)
    for g in range(N_GROUPS):
        xg = biased[g * GROUP_SIZE:(g + 1) * GROUP_SIZE, :]
        m1, i1 = _first_argmax(xg, iota_g, GROUP_SIZE)
        m2 = jnp.max(jnp.where(iota_g == i1, neg_inf, xg), axis=0, keepdims=True)
        grp = jnp.where(iota_n == g, m1 + m2, grp)

    keep_f = jnp.zeros((N_GROUPS, tm), F32)
    for _ in range(TOPK_GROUPS):
        _, ig = _first_argmax(grp, iota_n, N_GROUPS)
        sel = iota_n == ig
        keep_f = jnp.where(sel, 1.0, keep_f)
        grp = jnp.where(sel, neg_inf, grp)

    masked = jnp.concatenate(
        [jnp.where(keep_f[g:g + 1, :] > 0.5, biased[g * GROUP_SIZE:(g + 1) * GROUP_SIZE, :], neg_inf)
         for g in range(N_GROUPS)], axis=0)

    iota_e = lax.broadcasted_iota(jnp.int32, (N_EXPERTS, tm), 0)
    chosen = jnp.zeros((N_EXPERTS, tm), F32)
    idxs, gates = [], []
    for _ in range(TOP_K):
        _, ie = _first_argmax(masked, iota_e, N_EXPERTS)
        sel = iota_e == ie
        gates.append(jnp.sum(jnp.where(sel, scores, 0.0), axis=0, keepdims=True))
        idxs.append(ie)
        chosen = jnp.where(sel, 1.0, chosen)
        masked = jnp.where(sel, neg_inf, masked)

    denom = gates[0]
    for gk in gates[1:]:
        denom = denom + gk

    r = lax.broadcasted_iota(jnp.int32, (tm, tm), 0)
    c = lax.broadcasted_iota(jnp.int32, (tm, tm), 1)
    before = jnp.where(r < c, 1.0, 0.0).astype(BF16)
    pos = jnp.dot(chosen.astype(BF16), before, preferred_element_type=F32) + base_ref[...]
    for k in range(TOP_K):
        sel = iota_e == idxs[k]
        rank_ref[k:k + 1, :] = jnp.sum(jnp.where(sel, pos, 0.0), axis=0, keepdims=True).astype(jnp.int32)
        idx_ref[k:k + 1, :] = idxs[k]
        gate_ref[k:k + 1, :] = gates[k] / denom * ROUTED_SCALE

    base_ref[...] = base_ref[...] + jnp.sum(chosen, axis=1, keepdims=True)
    cnt_ref[...] = base_ref[...]


def _router(h_pk, w_router_t, bias_col, tm):
    n_tok, half = h_pk.shape
    row_blk = pl.BlockSpec((TOP_K, tm), lambda i: (0, i))
    return pl.pallas_call(
        _router_kernel,
        grid=(n_tok // tm,),
        in_specs=[
            pl.BlockSpec((tm, half), lambda i: (i, 0)),
            _resident(w_router_t.shape),
            pl.BlockSpec((N_EXPERTS, 1), lambda i: (0, 0)),
        ],
        out_specs=[row_blk, row_blk, row_blk, pl.BlockSpec((N_EXPERTS, 1), lambda i: (0, 0))],
        out_shape=[
            jax.ShapeDtypeStruct((TOP_K, n_tok), jnp.int32),
            jax.ShapeDtypeStruct((TOP_K, n_tok), F32),
            jax.ShapeDtypeStruct((TOP_K, n_tok), jnp.int32),
            jax.ShapeDtypeStruct((N_EXPERTS, 1), F32),
        ],
        scratch_shapes=[pltpu.VMEM((N_EXPERTS, 1), F32)],
        compiler_params=_cparams(("arbitrary",), 32),
        name="router",
    )(h_pk, w_router_t, bias_col)


def _dest_kernel(idx_ref, rank_ref, start_ref, dest_ref):
    tm = idx_ref.shape[1]
    iota_e = lax.broadcasted_iota(jnp.int32, (N_EXPERTS, tm), 0)
    start = start_ref[...]
    for k in range(TOP_K):
        sel = iota_e == idx_ref[k:k + 1, :]
        base = jnp.sum(jnp.where(sel, start, 0.0), axis=0, keepdims=True)
        dest_ref[k:k + 1, :] = base.astype(jnp.int32) + rank_ref[k:k + 1, :]


def _dest(idx, rank, start_col, tm):
    n_tok = idx.shape[1]
    row_blk = pl.BlockSpec((TOP_K, tm), lambda i: (0, i))
    return pl.pallas_call(
        _dest_kernel,
        grid=(n_tok // tm,),
        in_specs=[row_blk, row_blk, pl.BlockSpec((N_EXPERTS, 1), lambda i: (0, 0))],
        out_specs=row_blk,
        out_shape=jax.ShapeDtypeStruct((TOP_K, n_tok), jnp.int32),
        compiler_params=_cparams(("parallel",), 32),
        name="dest",
    )(idx, rank, start_col)


def _dispatch_kernel(pstart_ref, cnt_ref, tail_ref, dest_ref, h_ref, xs_ref, zero_ref, sem, pad_sem, tail_sem):
    tm = h_ref.shape[0]

    def pad_copy(e, j):
        return pltpu.make_async_copy(zero_ref.at[pl.ds(0, 1)],
                                     xs_ref.at[pl.ds(pstart_ref[e] + cnt_ref[e] + j, 1)], pad_sem)

    def tail_copy():
        return pltpu.make_async_copy(
            zero_ref, xs_ref.at[pl.ds(pl.multiple_of(tail_ref[0], ROW_ALIGN), EXPERT_ROWS)], tail_sem)

    def for_each_pad_row(action):
        def body(e, carry):
            n_pad = (ROW_ALIGN - cnt_ref[e] % ROW_ALIGN) % ROW_ALIGN
            for j in range(ROW_ALIGN - 1):
                @pl.when(j < n_pad)
                def _():
                    action(pad_copy(e, j))
            return carry
        lax.fori_loop(0, N_EXPERTS, body, 0)

    @pl.when(pl.program_id(0) == 0)
    def _():
        zero_ref[...] = jnp.zeros_like(zero_ref)
        tail_copy().start()
        for_each_pad_row(lambda c: c.start())
        for_each_pad_row(lambda c: c.wait())
        tail_copy().wait()

    def row_copy(r, k):
        return pltpu.make_async_copy(h_ref.at[pl.ds(r, 1)], xs_ref.at[pl.ds(dest_ref[0, k * tm + r], 1)], sem)

    def start(r, carry):
        for k in range(TOP_K):
            row_copy(r, k).start(priority=k % DMA_THREADS)
        return carry

    def wait(r, carry):
        for k in range(TOP_K):
            row_copy(r, k).wait()
        return carry

    lax.fori_loop(0, tm, start, 0)
    lax.fori_loop(0, tm, wait, 0)


def _dispatch(h_pk, dest_tiles, pstart, counts, tail, n_rows, tm):
    n_tok, half = h_pk.shape
    grid_spec = pltpu.PrefetchScalarGridSpec(
        num_scalar_prefetch=3,
        grid=(n_tok // tm,),
        in_specs=[
            pl.BlockSpec((None, 1, TOP_K * tm), lambda i, *_: (i, 0, 0), memory_space=pltpu.SMEM),
            pl.BlockSpec((tm, half), lambda i, *_: (i, 0)),
        ],
        out_specs=pl.BlockSpec(memory_space=pl.ANY),
        scratch_shapes=[pltpu.VMEM((EXPERT_ROWS, half), U32), pltpu.SemaphoreType.DMA,
                        pltpu.SemaphoreType.DMA, pltpu.SemaphoreType.DMA],
    )
    return pl.pallas_call(
        _dispatch_kernel,
        grid_spec=grid_spec,
        out_shape=jax.ShapeDtypeStruct((n_rows, half), U32),
        compiler_params=_cparams(("arbitrary",), 32, disable_bounds_checks=True),
        name="dispatch",
    )(pstart, counts, tail, dest_tiles, h_pk)


def _experts_kernel(row_ref, e_ref, valid_ref, newe_ref, slot_ref, nexte_ref,
                    x_hbm, wg_hbm, wu_hbm, wd_hbm, y_hbm,
                    x_buf, y_buf, wg_buf, wu_buf, wd_buf, wgu_bf, wd_bf, x_sems, y_sems, sems):
    n_steps = pl.num_programs(0) * WINDOWS_PER_STEP
    ff = wg_buf.shape[2]

    def window(step):
        return pl.ds(pl.multiple_of(row_ref[step], ROW_ALIGN), EXPERT_ROWS)

    def x_copy(step, slot):
        return pltpu.make_async_copy(x_hbm.at[window(step)], x_buf.at[slot], x_sems.at[slot])

    def y_copy(step, slot):
        return pltpu.make_async_copy(y_buf.at[slot], y_hbm.at[window(step)], y_sems.at[slot])

    def weight_copies(e, slot):
        copies = []
        for m, (src, dst) in enumerate(((wg_hbm, wg_buf), (wu_hbm, wu_buf), (wd_hbm, wd_buf))):
            rows = dst.shape[1] // WEIGHT_DMA_CHUNKS
            for c in range(WEIGHT_DMA_CHUNKS):
                part = pl.ds(c * rows, rows)
                copies.append(pltpu.make_async_copy(src.at[0, e, part], dst.at[slot, part],
                                                    sems.at[slot, m * WEIGHT_DMA_CHUNKS + c]))
        return copies

    def process(i, cur):
        @pl.when(i == 0)
        def _():
            x_copy(0, 0).start()

        @pl.when(newe_ref[i] == 1)
        def _():
            slot = slot_ref[i]
            e = e_ref[i]

            @pl.when(i == 0)
            def _():
                for c in weight_copies(e, slot):
                    c.start(priority=WEIGHT_DMA_THREAD)

            for c in weight_copies(e, slot):
                c.wait()
            nxt = nexte_ref[i]

            @pl.when(nxt >= 0)
            def _():
                for c in weight_copies(nxt, 1 - slot):
                    c.start(priority=WEIGHT_DMA_THREAD)

            wgu_bf[:, 0:ff] = wg_buf[slot].astype(BF16)
            wgu_bf[:, ff:] = wu_buf[slot].astype(BF16)
            wd_bf[...] = wd_buf[slot].astype(BF16)

        @pl.when(valid_ref[i] == 1)
        def _():
            nxt_step = jnp.minimum(i + 1, n_steps - 1)
            has_next = jnp.logical_and(i + 1 < n_steps, valid_ref[nxt_step] == 1)
            x_copy(i, cur).wait()

            @pl.when(has_next)
            def _():
                x_copy(nxt_step, 1 - cur).start()

            gu = _dot_packed(x_buf[cur], wgu_bf)
            hid = (jax.nn.silu(gu[:, 0:ff]) * gu[:, ff:]).astype(BF16)
            y_buf[cur] = _pack_bf16_pair(jnp.dot(hid, wd_bf[...], preferred_element_type=F32))

            @pl.when(i > 0)
            def _():
                y_copy(i - 1, 1 - cur).wait()

            y_copy(i, cur).start()

            @pl.when(jnp.logical_not(has_next))
            def _():
                y_copy(i, cur).wait()

    for u in range(WINDOWS_PER_STEP):
        process(pl.program_id(0) * WINDOWS_PER_STEP + u, u % 2)


def _experts(x_sorted, steps, w_gate, w_up, w_down):
    n_rows, half = x_sorted.shape
    d = 2 * half
    ff = w_gate.shape[-1]
    hbm = pl.BlockSpec(memory_space=pl.ANY)
    grid_spec = pltpu.PrefetchScalarGridSpec(
        num_scalar_prefetch=len(steps),
        grid=(steps[0].shape[0] // WINDOWS_PER_STEP,),
        in_specs=[hbm, hbm, hbm, hbm],
        out_specs=hbm,
        scratch_shapes=[pltpu.VMEM((2, EXPERT_ROWS, half), U32), pltpu.VMEM((2, EXPERT_ROWS, half), U32),
                        pltpu.VMEM((2, d, ff), F32), pltpu.VMEM((2, d, ff), F32), pltpu.VMEM((2, ff, d), F32),
                        pltpu.VMEM((d, 2 * ff), BF16), pltpu.VMEM((ff, d), BF16),
                        pltpu.SemaphoreType.DMA((2,)), pltpu.SemaphoreType.DMA((2,)),
                        pltpu.SemaphoreType.DMA((2, 3 * WEIGHT_DMA_CHUNKS))],
    )
    return pl.pallas_call(
        _experts_kernel,
        grid_spec=grid_spec,
        out_shape=jax.ShapeDtypeStruct((n_rows, half), U32),
        compiler_params=_cparams(("arbitrary",), 52, disable_bounds_checks=True),
        name="experts",
    )(*steps, x_sorted, w_gate, w_up, w_down)


def _final_kernel(dest_ref, gate_ref, hf_ref, hp_ref, y_ref, wgu_ref, wd_ref, g_ref, b_ref, op_ref, os_ref,
                  ybuf, sem, *, tp):
    tm = hf_ref.shape[0]
    ff = wd_ref.shape[0]

    def row_copy(r, k):
        return pltpu.make_async_copy(y_ref.at[pl.ds(dest_ref[0, k * tm + r], 1)], ybuf.at[k, pl.ds(r, 1)], sem)

    def start(r, carry):
        for k in range(TOP_K):
            row_copy(r, k).start(priority=k % DMA_THREADS)
        return carry

    def wait(r, carry):
        for k in range(TOP_K):
            row_copy(r, k).wait()
        return carry

    lax.fori_loop(0, tm, start, 0)

    gu = _dot_packed(hp_ref[...], wgu_ref)
    hid = (jax.nn.silu(gu[:, 0:ff]) * gu[:, ff:]).astype(BF16)
    acc = DN_ALPHA * hf_ref[...] + jnp.dot(hid, wd_ref[...], preferred_element_type=F32)

    lax.fori_loop(0, tm, wait, 0)
    gate = gate_ref[...]
    half = ybuf.shape[2]
    acc_lo, acc_hi = acc[:, :half], acc[:, half:]
    for k in range(TOP_K):
        packed = ybuf[k]
        g_k = gate[:, k:k + 1]
        acc_lo = acc_lo + lax.bitcast_convert_type(packed << 16, F32) * g_k
        acc_hi = acc_hi + lax.bitcast_convert_type(packed & jnp.uint32(HIGH_HALF), F32) * g_k
    out = _layer_norm_f32(jnp.concatenate([acc_lo, acc_hi], axis=1), g_ref[...], b_ref[...])

    @pl.when(pl.program_id(0) < tp)
    def _():
        op_ref[...] = out

    @pl.when(pl.program_id(0) >= tp)
    def _():
        os_ref[...] = out


def _final(h_f32, h_pk, y_sorted, dest_tiles, gate_t, w_gu, w_d, g, b, n_prompt, tm):
    n_tok, d = h_f32.shape
    tp = n_prompt // tm
    ts = n_tok // tm - tp
    vec = pl.BlockSpec((1, d), lambda i: (0, 0))
    op_spec, os_spec = _two_group_specs(tm, d, tp, False)
    return pl.pallas_call(
        functools.partial(_final_kernel, tp=tp),
        grid=(tp + ts,),
        in_specs=[
            pl.BlockSpec((None, 1, TOP_K * tm), lambda i: (i, 0, 0), memory_space=pltpu.SMEM),
            pl.BlockSpec((tm, TOP_K), lambda i: (i, 0)),
            pl.BlockSpec((tm, d), lambda i: (i, 0)),
            pl.BlockSpec((tm, d // 2), lambda i: (i, 0)),
            pl.BlockSpec(memory_space=pl.ANY),
            _resident(w_gu.shape), _resident(w_d.shape), vec, vec,
        ],
        out_specs=[op_spec, os_spec],
        out_shape=[jax.ShapeDtypeStruct((tp * tm, d), F32), jax.ShapeDtypeStruct((ts * tm, d), F32)],
        scratch_shapes=[pltpu.VMEM((TOP_K, tm, d // 2), U32), pltpu.SemaphoreType.DMA],
        compiler_params=_cparams(("arbitrary",), 48, disable_bounds_checks=True),
        name="final",
    )(dest_tiles, gate_t, h_f32, h_pk, y_sorted, w_gu, w_d, g, b)


def _lower_bounds(gamma):
    return jnp.cumsum(jax.nn.softmax(gamma.astype(F32), axis=0), axis=0)


def _tile_major(dest, tm):
    k, n = dest.shape
    return jnp.transpose(dest.reshape(k, n // tm, tm), (1, 0, 2)).reshape(n // tm, 1, k * tm)


def _expert_steps(counts, n_assign):
    rows = EXPERT_ROWS
    n_steps = (n_assign + N_EXPERTS * (ROW_ALIGN - 1)) // rows + N_EXPERTS
    n_steps = (n_steps + WINDOWS_PER_STEP - 1) // WINDOWS_PER_STEP * WINDOWS_PER_STEP
    padded = (counts + ROW_ALIGN - 1) // ROW_ALIGN * ROW_ALIGN
    pend = jnp.cumsum(padded)
    pstart = pend - padded
    per_e = (padded + rows - 1) // rows
    st_end = jnp.cumsum(per_e)
    st_start = st_end - per_e
    total = st_end[-1]
    i = jnp.arange(n_steps, dtype=jnp.int32)
    ic = jnp.minimum(i, total - 1)
    e = jnp.sum((st_end[None, :] <= ic[:, None]).astype(jnp.int32), axis=1)
    j = ic - st_start[e]
    row0 = pstart[e] + jnp.minimum(j * rows, jnp.maximum(padded[e] - rows, 0))
    valid = i < total
    prev_e = jnp.concatenate([jnp.full((1,), -1, jnp.int32), e[:-1].astype(jnp.int32)])
    new_e = jnp.where(jnp.logical_and(valid, e != prev_e), 1, 0)
    nonempty = counts > 0
    slot = (jnp.cumsum(nonempty.astype(jnp.int32)) - 1) % 2
    ids = jnp.arange(N_EXPERTS, dtype=jnp.int32)
    later = jnp.concatenate([jnp.where(nonempty, ids, N_EXPERTS)[1:], jnp.full((1,), N_EXPERTS, jnp.int32)])
    nxt = jnp.flip(lax.cummin(jnp.flip(later)))
    nxt = jnp.where(nxt < N_EXPERTS, nxt, -1)
    as_i32 = lambda v: v.astype(jnp.int32)
    steps = (as_i32(row0), as_i32(e), as_i32(valid), as_i32(new_e), as_i32(slot[e]), as_i32(nxt[e]))
    n_rows = n_assign + N_EXPERTS * (ROW_ALIGN - 1) + rows
    return as_i32(pstart), as_i32(pend[-1:]), n_rows, steps


def kernel(x_prompt, x_sample, meta_tokens, ln_in_g, ln_in_b, hg_gamma_fwd, hg_gamma_bwd, w_in, lambda_q1, lambda_k1, lambda_q2, lambda_k2, attn_subln_g, hg_norm_g, w_branch_attn, w_branch_hgrn, w_out, ln1_g, ln1_b, w_router, router_bias, w_sh_gate, w_sh_up, w_sh_down, w_ex_gate, w_ex_up, w_ex_down, ln2_g, ln2_b):
    bp, seq, d = x_prompt.shape
    bs, seq_s, _ = x_sample.shape
    assert seq == seq_s and w_in.shape[0] == DEPTH
    n_seq = bp + bs
    n_tok = n_seq * seq
    layer = 0

    xp = x_prompt.reshape(bp * seq, d)
    xs = x_sample.reshape(bs * seq, d)
    row = lambda v: v.reshape(1, -1).astype(F32)
    ln_g, ln_b = row(ln_in_g), row(ln_in_b)
    w_in_bf = w_in[layer].astype(BF16)

    proj = _ln_matmul(xp, xs, ln_g, ln_b, w_in_bf, tm=min(1024, bs * seq), tn=1024)
    meta = meta_tokens.astype(F32)
    meta_proj = _ln_matmul(meta, meta, ln_g, ln_b, w_in_bf, tm=N_META, tn=1024)[:N_META]
    meta_proj = jnp.pad(meta_proj, ((META_PAD - N_META, 0), (0, 0)))

    lam = (jnp.exp(jnp.sum(lambda_q1[layer].astype(F32) * lambda_k1[layer].astype(F32)))
           - jnp.exp(jnp.sum(lambda_q2[layer].astype(F32) * lambda_k2[layer].astype(F32))) + LAMBDA_INIT)
    slopes = 2.0 ** (-8.0 * jnp.arange(1, A_HEADS + 1, dtype=F32) / A_HEADS)
    attn_params = jnp.concatenate([lam.reshape(1), slopes]).astype(F32)
    subln_g = row(attn_subln_g[layer]) * (1.0 - LAMBDA_INIT)
    ao = _diff_attn(proj, meta_proj, attn_params, subln_g, n_seq, seq, tq=min(2048, seq))

    lb_fwd = row(_lower_bounds(hg_gamma_fwd)[layer])
    lb_bwd = row(_lower_bounds(hg_gamma_bwd)[layer])
    ho = _hgrn(proj, meta_proj, lb_fwd, lb_bwd, row(hg_norm_g[layer]), n_seq, seq)

    h_f32, h_pk = _merge(ao, ho, proj, xp, xs, w_branch_attn[layer].astype(BF16),
                         w_branch_hgrn[layer].astype(BF16), w_out[layer].astype(BF16), ln_g, ln_b,
                         row(ln1_g[layer]), row(ln1_b[layer]), tm=min(256, bs * seq))

    idx, gate, rank, counts = _router(h_pk, w_router[layer].T.astype(BF16),
                                      router_bias[layer].astype(F32).reshape(N_EXPERTS, 1), tm=min(512, n_tok))
    counts = counts[:, 0].astype(jnp.int32)
    pstart, tail, n_rows, steps = _expert_steps(counts, n_tok * TOP_K)
    dest = _dest(idx, rank, pstart.astype(F32).reshape(N_EXPERTS, 1), tm=min(2048, n_tok))

    tm_d = min(1024, n_tok)
    x_sorted = _dispatch(h_pk, _tile_major(dest, tm_d), pstart, counts, tail, n_rows, tm_d)
    y_sorted = _experts(x_sorted, steps, w_ex_gate, w_ex_up, w_ex_down)

    w_sh_gu = jnp.concatenate([w_sh_gate[layer], w_sh_up[layer]], axis=1).astype(BF16)
    tm_f = min(256, bs * seq)
    y_prompt, y_sample = _final(h_f32, h_pk, y_sorted, _tile_major(dest, tm_f), gate.T, w_sh_gu,
                                w_sh_down[layer].astype(BF16), row(ln2_g[layer]), row(ln2_b[layer]),
                                bp * seq, tm_f)
    return (y_prompt.reshape(bp, seq, d), y_sample.reshape(bs, seq, d))
```
